```python
import math
import jax, jax.numpy as jnp
from jax import lax
import numpy as np

D_MODEL = 2048
BATCH = 16
SEQ = 2048
DEPTH = 2
DEC_BATCH = 16
DEC_SEQ = 64
PAST_LEN = 4096

CHUNK = 64
GDN_HEADS = 16
GDN_DK = 128
GDN_DV = 128
GDN_WIDTH = GDN_HEADS * GDN_DV
CONV_W = 4
CONV_CH = 2 * GDN_HEADS * GDN_DK + GDN_WIDTH
ATT_HEADS = 16
KV_HEADS = 2
HEAD_DIM = 128
ATT_WIDTH = ATT_HEADS * HEAD_DIM
IDX_HEADS = 16
IDX_DIM = 64
TOPK_MAX = 256
Q_BLOCK = 128
REL_BUCKETS = 32
REL_MAX_DIST = 128
EPS = 1e-6

IN_SIZES = (CONV_CH, GDN_WIDTH, GDN_HEADS, GDN_HEADS,
            ATT_WIDTH, KV_HEADS * HEAD_DIM, KV_HEADS * HEAD_DIM, ATT_WIDTH,
            IDX_HEADS * IDX_DIM, IDX_DIM, IDX_HEADS,
            D_MODEL, D_MODEL)
IN_DIM = sum(IN_SIZES)

kernel_name = 'hybrid_gdn_dsa_stream_step'


def rmsnorm(x, w):
    xf = x.astype(jnp.float32)
    y = xf * lax.rsqrt(jnp.mean(xf * xf, axis=-1, keepdims=True) + EPS)
    return (y * w.astype(jnp.float32)).astype(x.dtype)


def l2norm(x):
    return x * lax.rsqrt(jnp.sum(x * x, axis=-1, keepdims=True) + EPS)


def causal_conv(x, prev, w):
    xp = jnp.concatenate([prev.astype(x.dtype), x], axis=1)
    T = x.shape[1]
    y = xp[:, 0:T] * w[0]
    for j in range(1, CONV_W):
        y = y + xp[:, j:j + T] * w[j]
    return y, xp[:, -(CONV_W - 1):]


def rel_bucket(rel):
    nb = REL_BUCKETS // 2
    max_exact = nb // 2
    n = jnp.abs(rel)
    nf = jnp.maximum(n, 1).astype(jnp.float32)
    large = max_exact + (jnp.log(nf / max_exact) / math.log(REL_MAX_DIST / max_exact)
                         * (nb - max_exact)).astype(jnp.int32)
    large = jnp.minimum(large, nb - 1)
    return jnp.where(rel > 0, nb, 0) + jnp.where(n < max_exact, n, large)


def gated_delta_rule(q, k, v, g, beta, S0):
    B, T, H, DK = q.shape
    C = min(CHUNK, T)
    N = T // C

    def chunks(a):
        a = a.reshape((B, N, C, H) + a.shape[3:])
        return jnp.moveaxis(a, (1, 3), (0, 2))

    q, k, v, g, beta = chunks(q), chunks(k), chunks(v), chunks(g), chunks(beta)
    gc = jnp.cumsum(g, axis=-1)
    idx = jnp.arange(C)
    tril = idx[:, None] >= idx[None, :]
    strict = idx[:, None] > idx[None, :]
    decay = jnp.exp(jnp.where(tril, gc[..., :, None] - gc[..., None, :], -jnp.inf))
    kb = k * beta[..., None]
    A = jnp.where(strict, jnp.einsum('nbhid,nbhjd->nbhij', kb, k) * decay, 0.0)
    eye = jnp.eye(C, dtype=jnp.float32)
    Tm = lax.linalg.triangular_solve(eye + A, jnp.broadcast_to(eye, A.shape),
                                     left_side=True, lower=True, unit_diagonal=True)
    u = jnp.einsum('nbhij,nbhjd->nbhid', Tm, v * beta[..., None])
    w = jnp.einsum('nbhij,nbhjd->nbhid', Tm, kb * jnp.exp(gc)[..., None])
    qk = jnp.einsum('nbhid,nbhjd->nbhij', q, k) * decay
    q_dec = q * jnp.exp(gc)[..., None]
    k_dec = k * jnp.exp(gc[..., -1:] - gc)[..., None]
    g_last = jnp.exp(gc[..., -1])

    def step(S, xs):
        u_i, w_i, qk_i, qd_i, kd_i, gl_i = xs
        v_new = u_i - jnp.einsum('bhcd,bhde->bhce', w_i, S)
        o_i = jnp.einsum('bhcd,bhde->bhce', qd_i, S) + jnp.einsum('bhij,bhje->bhie', qk_i, v_new)
        S = S * gl_i[..., None, None] + jnp.einsum('bhcd,bhce->bhde', kd_i, v_new)
        return S, o_i

    S, o = lax.scan(step, S0, (u, w, qk, q_dec, k_dec, g_last))
    o = jnp.moveaxis(o, (0, 2), (1, 3)).reshape(B, T, H, -1)
    return o, S


def sparse_attention(q, qi, wi, q_pos, K, V, KI, topk, rel_bias):
    B, Tq = q.shape[:2]
    L = K.shape[1]
    k_pos = jnp.arange(L)
    dots = jnp.einsum('bthd,bsd->bths', qi.astype(jnp.float32), KI.astype(jnp.float32))
    score = jnp.einsum('bth,bths->bts', wi.astype(jnp.float32), jax.nn.relu(dots))
    visible = (k_pos[None, :] // CHUNK) <= (q_pos[:, None] // CHUNK)
    score = jnp.where(visible[None], score, -jnp.inf)
    top_val, top_idx = lax.top_k(score, topk)
    valid = jnp.isfinite(top_val)
    Ks = jax.vmap(lambda kb, ib: kb[ib])(K, top_idx)
    Vs = jax.vmap(lambda vb, ib: vb[ib])(V, top_idx)
    G = ATT_HEADS // KV_HEADS
    qg = q.reshape(B, Tq, KV_HEADS, G, HEAD_DIM)
    logits = jnp.einsum('btngd,btsnd->btngs', qg, Ks).astype(jnp.float32) * (HEAD_DIM ** -0.5)
    rel = top_idx - q_pos[None, :, None]
    bias = rel_bias[rel_bucket(rel)].astype(jnp.float32)
    bias = jnp.moveaxis(bias.reshape(B, Tq, topk, KV_HEADS, G), 2, 4)
    logits = jnp.where(valid[:, :, None, None, :], logits + bias, -jnp.inf)
    p = jax.nn.softmax(logits, axis=-1).astype(V.dtype)
    o = jnp.einsum('btngs,btsnd->btngd', p, Vs)
    return o.reshape(B, Tq, ATT_WIDTH)


def trunk_layer(x, c, conv_prev, S0, past, norm_w, w_ada, b_ada, w_in, w_conv, a_log, dt_bias,
                gdn_norm_w, w_branch_a, w_branch_b, w_out, rel_bias):
    B, T, _ = x.shape
    shift, scale, gate = jnp.split(jax.nn.silu(c) @ w_ada + b_ada, 3, axis=-1)
    h = rmsnorm(x, norm_w) * (1.0 + scale[:, None]) + shift[:, None]
    proj = h @ w_in
    (qkv_a, z_a, b_a, a_a, q_b, k_b, v_b, z_b, q_i, k_i, w_i, gl_a, gl_b) = jnp.split(
        proj, np.cumsum(IN_SIZES)[:-1].tolist(), axis=-1)

    conv_out, conv_new = causal_conv(qkv_a, conv_prev, w_conv)
    conv_out = jax.nn.silu(conv_out).astype(jnp.float32)
    qa, ka, va = jnp.split(conv_out, [GDN_HEADS * GDN_DK, 2 * GDN_HEADS * GDN_DK], axis=-1)
    qa = l2norm(qa.reshape(B, T, GDN_HEADS, GDN_DK)) * (GDN_DK ** -0.5)
    ka = l2norm(ka.reshape(B, T, GDN_HEADS, GDN_DK))
    va = va.reshape(B, T, GDN_HEADS, GDN_DV)
    beta = jax.nn.sigmoid(b_a.astype(jnp.float32))
    g = -jnp.exp(a_log.astype(jnp.float32)) * jax.nn.softplus(a_a.astype(jnp.float32) + dt_bias.astype(jnp.float32))
    o_a, S_new = gated_delta_rule(qa, ka, va, g, beta, S0.astype(jnp.float32))
    o_a = rmsnorm(o_a, gdn_norm_w) * jax.nn.silu(z_a.astype(jnp.float32)).reshape(B, T, GDN_HEADS, GDN_DV)
    y_a = o_a.reshape(B, T, GDN_WIDTH).astype(x.dtype) @ w_branch_a

    q_b = q_b.reshape(B, T, ATT_HEADS, HEAD_DIM)
    k_b = k_b.reshape(B, T, KV_HEADS, HEAD_DIM)
    v_b = v_b.reshape(B, T, KV_HEADS, HEAD_DIM)
    q_i = q_i.reshape(B, T, IDX_HEADS, IDX_DIM)
    w_i = w_i * ((IDX_HEADS ** -0.5) * (IDX_DIM ** -0.5))
    if past is None:
        K, V, KI = k_b, v_b, k_i
        topk = min(TOPK_MAX, K.shape[1] // 4)
        qb = min(Q_BLOCK, T)
        nblk = T // qb

        def blk(a):
            return jnp.moveaxis(a.reshape((B, nblk, qb) + a.shape[2:]), 1, 0)

        pos = jnp.arange(T).reshape(nblk, qb)
        o_b = lax.map(lambda xs: sparse_attention(xs[0], xs[1], xs[2], xs[3], K, V, KI, topk, rel_bias),
                      (blk(q_b), blk(q_i), blk(w_i), pos))
        o_b = jnp.moveaxis(o_b, 0, 1).reshape(B, T, ATT_WIDTH)
    else:
        k_past, v_past, ki_past = past
        P = k_past.shape[1]
        K = jnp.concatenate([k_past.astype(k_b.dtype), k_b], axis=1)
        V = jnp.concatenate([v_past.astype(v_b.dtype), v_b], axis=1)
        KI = jnp.concatenate([ki_past.astype(k_i.dtype), k_i], axis=1)
        topk = min(TOPK_MAX, K.shape[1] // 4)
        o_b = sparse_attention(q_b, q_i, w_i, P + jnp.arange(T), K, V, KI, topk, rel_bias)
    y_b = (o_b * jax.nn.silu(z_b)) @ w_branch_b

    merged = jax.nn.sigmoid(gl_a) * y_a + jax.nn.sigmoid(gl_b) * y_b
    x = x + gate[:, None] * (merged @ w_out)
    return x, (k_b, v_b, k_i, S_new, conv_new)


def setup_inputs(seed: int = 0) -> dict:
    key = jax.random.key(seed)
    ks = jax.random.split(key, 24)
    f32 = jnp.float32

    def nrm(k, shape, s):
        return jax.random.normal(k, shape, f32) * s

    dt = jnp.exp(jax.random.uniform(ks[15], (DEPTH, GDN_HEADS), f32, math.log(1e-3), math.log(1e-1)))
    return {
        'x_prompt': nrm(ks[0], (BATCH, SEQ, D_MODEL), 1.0),
        'x_sample': nrm(ks[1], (DEC_BATCH, DEC_SEQ, D_MODEL), 1.0),
        'c_prompt': nrm(ks[2], (BATCH, D_MODEL), 1.0),
        'c_sample': nrm(ks[3], (DEC_BATCH, D_MODEL), 1.0),
        'cache_k': nrm(ks[4], (DEPTH, DEC_BATCH, PAST_LEN, KV_HEADS, HEAD_DIM), 1.0),
        'cache_v': nrm(ks[5], (DEPTH, DEC_BATCH, PAST_LEN, KV_HEADS, HEAD_DIM), 1.0),
        'cache_idx_k': nrm(ks[6], (DEPTH, DEC_BATCH, PAST_LEN, IDX_DIM), 1.0),
        'state_gdn': nrm(ks[7], (DEPTH, DEC_BATCH, GDN_HEADS, GDN_DK, GDN_DV), 0.1),
        'state_conv': nrm(ks[8], (DEPTH, DEC_BATCH, CONV_W - 1, CONV_CH), 1.0),
        'norm_w': 1.0 + nrm(ks[9], (DEPTH, D_MODEL), 0.02),
        'w_ada': nrm(ks[10], (DEPTH, D_MODEL, 3 * D_MODEL), 0.5 * D_MODEL ** -0.5),
        'b_ada': nrm(ks[11], (DEPTH, 3 * D_MODEL), 0.02),
        'w_in': nrm(ks[12], (DEPTH, D_MODEL, IN_DIM), D_MODEL ** -0.5),
        'w_conv': nrm(ks[13], (DEPTH, CONV_W, CONV_CH), CONV_W ** -0.5),
        'a_log': jnp.log(jax.random.uniform(ks[14], (DEPTH, GDN_HEADS), f32, 1.0, 16.0)),
        'dt_bias': dt + jnp.log(-jnp.expm1(-dt)),
        'gdn_norm_w': 1.0 + nrm(ks[16], (DEPTH, GDN_DV), 0.02),
        'w_branch_a': nrm(ks[17], (DEPTH, GDN_WIDTH, D_MODEL), GDN_WIDTH ** -0.5),
        'w_branch_b': nrm(ks[18], (DEPTH, ATT_WIDTH, D_MODEL), ATT_WIDTH ** -0.5),
        'w_out': nrm(ks[19], (DEPTH, D_MODEL, D_MODEL), D_MODEL ** -0.5),
        'rel_bias': nrm(ks[20], (REL_BUCKETS, ATT_HEADS), 0.5),
        'final_norm_w': 1.0 + nrm(ks[21], (D_MODEL,), 0.02),
    }


def reference(x_prompt, x_sample, c_prompt, c_sample, cache_k, cache_v, cache_idx_k, state_gdn, state_conv,
              norm_w, w_ada, b_ada, w_in, w_conv, a_log, dt_bias, gdn_norm_w, w_branch_a, w_branch_b,
              w_out, rel_bias, final_norm_w):
    xp, xs = x_prompt, x_sample
    new_p, new_s = [], []
    for l in range(DEPTH):
        lw = (norm_w[l], w_ada[l], b_ada[l], w_in[l], w_conv[l], a_log[l], dt_bias[l], gdn_norm_w[l],
              w_branch_a[l], w_branch_b[l], w_out[l], rel_bias)
        Bp = xp.shape[0]
        conv0 = jnp.zeros((Bp, CONV_W - 1, CONV_CH), xp.dtype)
        S0 = jnp.zeros((Bp, GDN_HEADS, GDN_DK, GDN_DV), jnp.float32)
        xp, sp = trunk_layer(xp, c_prompt, conv0, S0, None, *lw)
        xs, ss = trunk_layer(xs, c_sample, state_conv[l], state_gdn[l],
                             (cache_k[l], cache_v[l], cache_idx_k[l]), *lw)
        new_p.append(sp)
        new_s.append(ss)
    y_prompt = rmsnorm(xp, final_norm_w)
    y_sample = rmsnorm(xs, final_norm_w)
    new_k_prompt = jnp.stack([s[0] for s in new_p])
    new_v_prompt = jnp.stack([s[1] for s in new_p])
    new_idx_k_prompt = jnp.stack([s[2] for s in new_p])
    new_gdn_prompt = jnp.stack([s[3] for s in new_p])
    new_conv_prompt = jnp.stack([s[4] for s in new_p])
    new_k_sample = jnp.stack([s[0] for s in new_s])
    new_v_sample = jnp.stack([s[1] for s in new_s])
    new_idx_k_sample = jnp.stack([s[2] for s in new_s])
    new_gdn_sample = jnp.stack([s[3] for s in new_s])
    new_conv_sample = jnp.stack([s[4] for s in new_s])
    return (y_prompt, y_sample, new_k_prompt, new_v_prompt, new_idx_k_prompt, new_gdn_prompt, new_conv_prompt,
            new_k_sample, new_v_sample, new_idx_k_sample, new_gdn_sample, new_conv_sample)
```

```python
import functools
import math

import jax
import jax.numpy as jnp
from jax import lax
from jax.experimental import pallas as pl
from jax.experimental.pallas import tpu as pltpu

D_MODEL = 2048
CHUNK = 64
GDN_HEADS = 16
GDN_DK = 128
GDN_DV = 128
CONV_W = 4
ATT_HEADS = 16
KV_HEADS = 2
HEAD_DIM = 128
IDX_HEADS = 16
IDX_DIM = 64
TOPK_MAX = 256
REL_BUCKETS = 32
REL_MAX_DIST = 128
EPS = 1e-6

LANES = 128
VMEM_LIMIT = 56 * 1024 * 1024

W_QKV = 3 * GDN_HEADS * GDN_DK
OFF_QKV = 0
OFF_ZA = OFF_QKV + W_QKV
OFF_QB = OFF_ZA + D_MODEL
OFF_ZB = OFF_QB + D_MODEL
OFF_GLA = OFF_ZB + D_MODEL
OFF_GLB = OFF_GLA + D_MODEL
OFF_QI = OFF_GLB + D_MODEL
OFF_KB = OFF_QI + IDX_HEADS * IDX_DIM
OFF_VB = OFF_KB + KV_HEADS * HEAD_DIM
OFF_SM = OFF_VB + KV_HEADS * HEAD_DIM
SM_KI, SM_BA, SM_AA, SM_WI = 0, 64, 80, 96
N_PROJ = 18432
GROUP = 64

_IN_SIZES = (W_QKV, D_MODEL, GDN_HEADS, GDN_HEADS, D_MODEL, KV_HEADS * HEAD_DIM, KV_HEADS * HEAD_DIM,
             D_MODEL, IDX_HEADS * IDX_DIM, IDX_DIM, IDX_HEADS, D_MODEL, D_MODEL)


def _cparams(sem):
    return pltpu.CompilerParams(dimension_semantics=sem, vmem_limit_bytes=VMEM_LIMIT)


def _bf(x):
    return x.astype(jnp.bfloat16)


def _dot(a, b):
    return jnp.dot(a, b, preferred_element_type=jnp.float32)


def _dot_nt(a, b):
    return lax.dot_general(a, b, (((1,), (1,)), ((), ())), preferred_element_type=jnp.float32)


def _ada_kernel(c_ref, w_ref, b_ref, o_ref):
    c = c_ref[...]
    a = _bf(c * jax.nn.sigmoid(c))
    o_ref[0] = _dot(a, _bf(w_ref[0])) + b_ref[0]


def _ada_call(c_all, w_ada, b_ada):
    depth, d, n = w_ada.shape
    nb = c_all.shape[0]
    tn = 1024
    return pl.pallas_call(
        _ada_kernel,
        grid=(depth, n // tn),
        in_specs=[pl.BlockSpec((nb, d), lambda l, j: (0, 0)),
                  pl.BlockSpec((1, d, tn), lambda l, j: (l, 0, j)),
                  pl.BlockSpec((1, 1, tn), lambda l, j: (l, 0, j))],
        out_specs=pl.BlockSpec((1, nb, tn), lambda l, j: (l, 0, j)),
        out_shape=jax.ShapeDtypeStruct((depth, nb, n), jnp.float32),
        compiler_params=_cparams(("arbitrary", "arbitrary")),
        name="ada_mod",
    )(c_all, w_ada, b_ada.reshape(depth, 1, n))


def _inproj_kernel(x_ref, nw_ref, sc_ref, sh_ref, w_ref, o_ref, h_ref, *, tm):
    @pl.when(pl.program_id(1) == 0)
    def _():
        nw = nw_ref[...]

        def body(g, carry):
            rows = pl.ds(pl.multiple_of(g * GROUP, GROUP), GROUP)
            x = x_ref[rows, :]
            y = x * lax.rsqrt(jnp.mean(x * x, axis=-1, keepdims=True) + EPS) * nw
            hh = y * (1.0 + sc_ref[pl.ds(g, 1), :]) + sh_ref[pl.ds(g, 1), :]
            h_ref[rows, :] = _bf(hh)
            return carry

        lax.fori_loop(0, tm // GROUP, body, 0)

    o_ref[...] = _dot(h_ref[...], w_ref[...])


def _inproj_call(x, norm_w, scale_g, shift_g, w_bf):
    m, d = x.shape
    n = w_bf.shape[1]
    tm = min(1024, m)
    tn = 1024
    gpt = tm // GROUP
    return pl.pallas_call(
        functools.partial(_inproj_kernel, tm=tm),
        grid=(m // tm, n // tn),
        in_specs=[pl.BlockSpec((tm, d), lambda i, j: (i, 0)),
                  pl.BlockSpec((1, d), lambda i, j: (0, 0)),
                  pl.BlockSpec((gpt, d), lambda i, j: (i, 0)),
                  pl.BlockSpec((gpt, d), lambda i, j: (i, 0)),
                  pl.BlockSpec((d, tn), lambda i, j: (0, j))],
        out_specs=pl.BlockSpec((tm, tn), lambda i, j: (i, j)),
        out_shape=jax.ShapeDtypeStruct((m, n), jnp.float32),
        scratch_shapes=[pltpu.VMEM((tm, d), jnp.bfloat16)],
        compiler_params=_cparams(("arbitrary", "arbitrary")),
        name="inproj",
    )(x, norm_w.reshape(1, d), scale_g, shift_g, w_bf)


def _conv_silu(x, tail_ref, w_ref):
    tb = x.shape[0]
    xc = jnp.concatenate([tail_ref[...], x], axis=0)
    w = w_ref[...]
    y = x * w[CONV_W - 1:CONV_W]
    for s in range(1, CONV_W):
        y = y + pltpu.roll(xc, s, 0)[8:] * w[CONV_W - 1 - s:CONV_W - s]
    tail_ref[...] = x[tb - 8:]
    return y * jax.nn.sigmoid(y)


def _l2norm(x):
    return x * lax.rsqrt(jnp.sum(x * x, axis=-1, keepdims=True) + EPS)


def _softplus(x):
    return jnp.maximum(x, 0.0) + jnp.log1p(jnp.exp(-jnp.abs(x)))


def _split(a):
    hi = _bf(a)
    return hi, _bf(a - hi.astype(jnp.float32))


def _dot_hi(a, b):
    ah, al = _split(a)
    bh, bl = _split(b)
    return _dot(ah, bh) + (_dot(ah, bl) + _dot(al, bh))


INV_BASE = 8


def _unit_lower_inverse(A, ii, jj, eye):
    C = A.shape[0]
    sh = INV_BASE.bit_length() - 1
    N = jnp.where((ii >> sh) == (jj >> sh), -A, 0.0)
    P = eye + N
    m = 2
    while m < INV_BASE:
        N = _dot_hi(N, N)
        P = P + _dot_hi(P, N)
        m *= 2
    s = INV_BASE
    while s < C:
        sh = s.bit_length() - 1
        off = ((ii >> (sh + 1)) == (jj >> (sh + 1))) & (((ii >> sh) & 1) == 1) & (((jj >> sh) & 1) == 0)
        P = P - _dot_hi(_dot_hi(P, jnp.where(off, A, 0.0)), P)
        s *= 2
    return P


def _gdn_kernel(q_ref, k_ref, v_ref, z_ref, sm_ref, cq_ref, ck_ref, cv_ref, wq_ref, wk_ref, wv_ref,
                lrow_ref, drow_ref, gw_ref, s0_ref,
                o_ref, sout_ref, tq_ref, tk_ref, tv_ref,
                S_ref, tailq, tailk, tailv, *, tb):
    h = pl.program_id(1)
    i = pl.program_id(2)
    C = CHUNK

    @pl.when(i == 0)
    def _():
        S_ref[...] = s0_ref[0, 0]
        tailq[...] = cq_ref[0]
        tailk[...] = ck_ref[0]
        tailv[...] = cv_ref[0]

    q = _l2norm(_conv_silu(q_ref[...], tailq, wq_ref)) * (GDN_DK ** -0.5)
    k = _l2norm(_conv_silu(k_ref[...], tailk, wk_ref))
    v = _conv_silu(v_ref[...], tailv, wv_ref)

    sm = sm_ref[...]
    lane = lax.broadcasted_iota(jnp.int32, sm.shape, 1)
    beta_all = jax.nn.sigmoid(sm)
    g_all = -jnp.exp(lrow_ref[...]) * _softplus(sm + drow_ref[...])
    beta = jnp.sum(jnp.where(lane == SM_BA + h, beta_all, 0.0), axis=1, keepdims=True)
    g = jnp.sum(jnp.where(lane == SM_AA + h, g_all, 0.0), axis=1, keepdims=True)

    gc = jnp.broadcast_to(g, (tb, LANES))
    rowc = lax.broadcasted_iota(jnp.int32, (tb, LANES), 0) % C
    s = 1
    while s < C:
        gc = gc + jnp.where(rowc >= s, pltpu.roll(gc, s, 0), 0.0)
        s *= 2
    egc = jnp.exp(gc)

    kb = k * beta
    vb = v * beta
    kbg = kb * egc
    qd = q * egc

    ii = lax.broadcasted_iota(jnp.int32, (C, C), 0)
    jj = lax.broadcasted_iota(jnp.int32, (C, C), 1)
    eye = jnp.where(ii == jj, 1.0, 0.0)

    outs = []
    for c in range(tb // C):
        r = slice(c * C, (c + 1) * C)
        gcc = gc[r]
        gcol = gcc[:, :C]
        grow = gcc.T[:C, :]
        decay = jnp.where(ii >= jj, jnp.exp(jnp.minimum(gcol - grow, 0.0)), 0.0)
        k_c = _bf(k[r])
        A = jnp.where(ii > jj, _dot_nt(_bf(kb[r]), k_c) * decay, 0.0)
        P = _unit_lower_inverse(A, ii, jj, eye)
        uw = _dot(_bf(P), _bf(jnp.concatenate([vb[r], kbg[r]], axis=1)))
        u = uw[:, :GDN_DV]
        w = uw[:, GDN_DV:]
        qk = _dot_nt(_bf(q[r]), k_c) * decay
        glast = gcc[C - 1:C, :]
        kd = k[r] * jnp.exp(glast - gcc)
        S = S_ref[...]
        Sb = _bf(S)
        v_new = u - _dot(_bf(w), Sb)
        vnb = _bf(v_new)
        outs.append(_dot(_bf(qd[r]), Sb) + _dot(_bf(qk), vnb))
        S_ref[...] = S * jnp.exp(glast) + _dot(_bf(kd.T), vnb)

    o = outs[0] if len(outs) == 1 else jnp.concatenate(outs, axis=0)
    o = o * lax.rsqrt(jnp.mean(o * o, axis=-1, keepdims=True) + EPS) * gw_ref[...]
    z = z_ref[...]
    o_ref[...] = _bf(o * (z * jax.nn.sigmoid(z)))

    @pl.when(i == pl.num_programs(2) - 1)
    def _():
        sout_ref[0, 0] = S_ref[...]
        tq_ref[0] = tailq[...]
        tk_ref[0] = tailk[...]
        tv_ref[0] = tailv[...]


def _gdn_call(proj, conv_prev8, s0, wconv8, lrow, drow, gw, *, nbatch, t):
    tb = min(256, t)
    nt = t // tb
    hh = GDN_HEADS
    cb = LANES

    def rowmap(col0):
        return lambda b, h, i: (b * nt + i, col0 + h)

    def prevmap(col0):
        return lambda b, h, i: (b, 0, col0 + h)

    def wmap(col0):
        return lambda b, h, i: (0, col0 + h)

    const = lambda b, h, i: (0, 0)
    tail_shape = jax.ShapeDtypeStruct((nbatch, 8, hh * cb), jnp.float32)
    outs = pl.pallas_call(
        functools.partial(_gdn_kernel, tb=tb),
        grid=(nbatch, hh, nt),
        in_specs=[pl.BlockSpec((tb, cb), rowmap(OFF_QKV // cb)),
                  pl.BlockSpec((tb, cb), rowmap(OFF_QKV // cb + hh)),
                  pl.BlockSpec((tb, cb), rowmap(OFF_QKV // cb + 2 * hh)),
                  pl.BlockSpec((tb, cb), rowmap(OFF_ZA // cb)),
                  pl.BlockSpec((tb, cb), lambda b, h, i: (b * nt + i, OFF_SM // cb)),
                  pl.BlockSpec((1, 8, cb), prevmap(0)),
                  pl.BlockSpec((1, 8, cb), prevmap(hh)),
                  pl.BlockSpec((1, 8, cb), prevmap(2 * hh)),
                  pl.BlockSpec((8, cb), wmap(0)),
                  pl.BlockSpec((8, cb), wmap(hh)),
                  pl.BlockSpec((8, cb), wmap(2 * hh)),
                  pl.BlockSpec((1, cb), const),
                  pl.BlockSpec((1, cb), const),
                  pl.BlockSpec((1, cb), const),
                  pl.BlockSpec((1, 1, GDN_DK, GDN_DV), lambda b, h, i: (b, h, 0, 0))],
        out_specs=[pl.BlockSpec((tb, cb), lambda b, h, i: (b * nt + i, h)),
                   pl.BlockSpec((1, 1, GDN_DK, GDN_DV), lambda b, h, i: (b, h, 0, 0)),
                   pl.BlockSpec((1, 8, cb), lambda b, h, i: (b, 0, h)),
                   pl.BlockSpec((1, 8, cb), lambda b, h, i: (b, 0, h)),
                   pl.BlockSpec((1, 8, cb), lambda b, h, i: (b, 0, h))],
        out_shape=[jax.ShapeDtypeStruct((nbatch * t, hh * GDN_DV), jnp.bfloat16),
                   jax.ShapeDtypeStruct((nbatch, hh, GDN_DK, GDN_DV), jnp.float32),
                   tail_shape, tail_shape, tail_shape],
        scratch_shapes=[pltpu.VMEM((GDN_DK, GDN_DV), jnp.float32),
                        pltpu.VMEM((8, cb), jnp.float32),
                        pltpu.VMEM((8, cb), jnp.float32),
                        pltpu.VMEM((8, cb), jnp.float32)],
        compiler_params=_cparams(("arbitrary", "arbitrary", "arbitrary")),
        name="gdn",
    )(proj, proj, proj, proj, proj, conv_prev8, conv_prev8, conv_prev8, wconv8, wconv8, wconv8,
      lrow, drow, gw, s0)
    o, s_new, tq, tk, tv = outs
    return o, s_new, jnp.concatenate([tq, tk, tv], axis=-1)


_INT_MIN = -2147483648


def _attn_kernel(*refs, tq, t_cur, past, lp, kc, topk):
    if past:
        (far_ref, qb_ref, zb_ref, qi_ref, smq_ref, k_ref, v_ref, smk_ref, kp_ref, vp_ref, kip_ref,
         na_ref, nb_ref, o_ref, kbf, vbf, kibf, sc_ref, key_ref, madd_ref, lg_ref) = refs
    else:
        (far_ref, qb_ref, zb_ref, qi_ref, smq_ref, k_ref, v_ref, smk_ref,
         na_ref, nb_ref, o_ref, kbf, vbf, kibf, sc_ref, key_ref, madd_ref, lg_ref) = refs
    i = pl.program_id(1)
    nreal = past + t_cur
    nkc = lp // kc
    nlb = lp // LANES

    @pl.when(i == 0)
    def _():
        if past:
            kbf[0:past, :] = _bf(kp_ref[0])
            vbf[0:past, :] = _bf(vp_ref[0])
            kibf[0:past, :] = _bf(kip_ref[0])
        kbf[past:nreal, :] = _bf(k_ref[...])
        vbf[past:nreal, :] = _bf(v_ref[...])
        kibf[past:nreal, :] = _bf(smk_ref[:, SM_KI:SM_KI + IDX_DIM])
        if lp > nreal:
            kbf[nreal:lp, :] = jnp.zeros((lp - nreal, KV_HEADS * HEAD_DIM), jnp.bfloat16)
            vbf[nreal:lp, :] = jnp.zeros((lp - nreal, KV_HEADS * HEAD_DIM), jnp.bfloat16)
            kibf[nreal:lp, :] = jnp.zeros((lp - nreal, IDX_DIM), jnp.bfloat16)

    q0 = past + i * tq
    trow = lax.broadcasted_iota(jnp.int32, (tq, 1), 0)
    lim = jnp.minimum(((q0 + trow) // CHUNK + 1) * CHUNK, nreal)

    wi = smq_ref[:, SM_WI:SM_WI + IDX_HEADS] * ((IDX_HEADS ** -0.5) * (IDX_DIM ** -0.5))
    qi = _bf(qi_ref[...])

    def score_chunk(c, carry):
        off = pl.multiple_of(c * kc, kc)
        ki = kibf[pl.ds(off, kc), :]
        acc = jnp.zeros((tq, kc), jnp.float32)
        for hh in range(IDX_HEADS):
            d = _dot_nt(qi[:, hh * IDX_DIM:(hh + 1) * IDX_DIM], ki)
            acc = acc + wi[:, hh:hh + 1] * jnp.maximum(d, 0.0)
        spos = off + lax.broadcasted_iota(jnp.int32, (tq, kc), 1)
        acc = jnp.where(spos < lim, acc, -jnp.inf)
        sc_ref[:, pl.ds(off, kc)] = acc
        bits = pltpu.bitcast(acc, jnp.int32)
        key_ref[:, pl.ds(off, kc)] = jnp.where(bits < 0, bits ^ 0x7FFFFFFF, bits)
        return carry

    lax.fori_loop(0, nkc, score_chunk, 0)

    def count_ge(cand):
        candb = jnp.broadcast_to(cand, (tq, LANES))
        acc = jnp.zeros((tq, LANES), jnp.float32)
        for c in range(nlb):
            acc = acc + jnp.where(key_ref[:, c * LANES:(c + 1) * LANES] >= candb, 1.0, 0.0)
        return jnp.sum(acc, axis=1, keepdims=True)

    def bit_step(it, tau_u):
        bit = lax.shift_left(jnp.int32(1), 31 - it)
        cand_u = tau_u | bit
        cnt = count_ge(cand_u ^ _INT_MIN)
        return jnp.where(cnt >= float(topk), cand_u, tau_u)

    tau_u = lax.fori_loop(0, 32, bit_step, jnp.zeros((tq, 1), jnp.int32))
    tau = tau_u ^ _INT_MIN
    cnt_gt = count_ge(tau + 1)
    need = float(topk) - cnt_gt

    ua = lax.broadcasted_iota(jnp.int32, (LANES, LANES), 0)
    ub = lax.broadcasted_iota(jnp.int32, (LANES, LANES), 1)
    upper = jnp.where(ua <= ub, 1.0, 0.0).astype(jnp.bfloat16)
    taub = jnp.broadcast_to(tau, (tq, LANES))

    def sel_block(c, carry):
        off = pl.multiple_of(c * LANES, LANES)
        key = key_ref[:, pl.ds(off, LANES)]
        eq = key == taub
        pref = _dot(jnp.where(eq, 1.0, 0.0).astype(jnp.bfloat16), upper) + carry
        sel = (key > taub) | (eq & (pref <= need))
        spos = off + lax.broadcasted_iota(jnp.int32, (tq, LANES), 1)
        madd_ref[:, pl.ds(off, LANES)] = jnp.where(sel & (spos < lim), 0.0, -jnp.inf)
        return jnp.broadcast_to(pref[:, LANES - 1:LANES], (tq, LANES))

    lax.fori_loop(0, nlb, sel_block, jnp.zeros((tq, LANES), jnp.float32))

    scale = HEAD_DIM ** -0.5
    off_b = pl.multiple_of(q0, LANES)
    off_a = pl.multiple_of(jnp.maximum(q0 - LANES, 0), LANES)

    def head_body(hd, carry):
        n = hd // (ATT_HEADS // KV_HEADS)
        hcol = pl.ds(pl.multiple_of(hd * HEAD_DIM, HEAD_DIM), HEAD_DIM)
        ncol = pl.ds(pl.multiple_of(n * HEAD_DIM, HEAD_DIM), HEAD_DIM)
        qh = _bf(qb_ref[:, hcol])
        far = far_ref[hd]

        def logit_chunk(c, carry2):
            off = pl.multiple_of(c * kc, kc)
            kk = kbf[pl.ds(off, kc), ncol]
            lg_ref[:, pl.ds(off, kc)] = _dot_nt(qh, kk) * scale + far + madd_ref[:, pl.ds(off, kc)]
            return carry2

        lax.fori_loop(0, nkc, logit_chunk, 0)
        lg_ref[:, pl.ds(off_b, LANES)] = lg_ref[:, pl.ds(off_b, LANES)] + nb_ref[hd]

        @pl.when(q0 >= LANES)
        def _():
            lg_ref[:, pl.ds(off_a, LANES)] = lg_ref[:, pl.ds(off_a, LANES)] + na_ref[hd]

        def max_chunk(c, mx):
            off = pl.multiple_of(c * kc, kc)
            return jnp.maximum(mx, jnp.max(lg_ref[:, pl.ds(off, kc)], axis=1, keepdims=True))

        mx = lax.fori_loop(0, nkc, max_chunk, jnp.full((tq, 1), -jnp.inf, jnp.float32))

        def pv_chunk(c, st):
            den, acc = st
            off = pl.multiple_of(c * kc, kc)
            p = jnp.exp(lg_ref[:, pl.ds(off, kc)] - mx)
            den = den + jnp.sum(p, axis=1, keepdims=True)
            acc = acc + _dot(_bf(p), vbf[pl.ds(off, kc), ncol])
            return den, acc

        den, acc = lax.fori_loop(0, nkc, pv_chunk, (jnp.zeros((tq, 1), jnp.float32),
                                                    jnp.zeros((tq, HEAD_DIM), jnp.float32)))
        z = zb_ref[:, hcol]
        o_ref[:, hcol] = _bf((acc / den) * (z * jax.nn.sigmoid(z)))
        return carry

    lax.fori_loop(0, ATT_HEADS, head_body, 0)


def _attn_call(proj, far, near_a, near_b, *, nbatch, t, past_kv=None):
    past = 0 if past_kv is None else past_kv[0].shape[1]
    tq = min(128, t)
    nq = t // tq
    nreal = past + t
    kc = 512 if nreal % 512 == 0 else 384
    if nreal < kc:
        kc = LANES * (-(-nreal // LANES))
    lp = kc * (-(-nreal // kc))
    topk = min(TOPK_MAX, nreal // 4)
    kvw = KV_HEADS * HEAD_DIM

    in_specs = [pl.BlockSpec(memory_space=pltpu.SMEM),
                pl.BlockSpec((tq, D_MODEL), lambda b, i: (b * nq + i, OFF_QB // D_MODEL)),
                pl.BlockSpec((tq, D_MODEL), lambda b, i: (b * nq + i, OFF_ZB // D_MODEL)),
                pl.BlockSpec((tq, IDX_HEADS * IDX_DIM), lambda b, i: (b * nq + i, OFF_QI // (IDX_HEADS * IDX_DIM))),
                pl.BlockSpec((tq, LANES), lambda b, i: (b * nq + i, OFF_SM // LANES)),
                pl.BlockSpec((t, kvw), lambda b, i: (b, OFF_KB // kvw)),
                pl.BlockSpec((t, kvw), lambda b, i: (b, OFF_VB // kvw)),
                pl.BlockSpec((t, LANES), lambda b, i: (b, OFF_SM // LANES))]
    args = [far, proj, proj, proj, proj, proj, proj, proj]
    if past:
        in_specs += [pl.BlockSpec((1, past, kvw), lambda b, i: (b, 0, 0)),
                     pl.BlockSpec((1, past, kvw), lambda b, i: (b, 0, 0)),
                     pl.BlockSpec((1, past, IDX_DIM), lambda b, i: (b, 0, 0))]
        args += list(past_kv)
    in_specs += [pl.BlockSpec((ATT_HEADS, tq, LANES), lambda b, i: (0, 0, 0)),
                 pl.BlockSpec((ATT_HEADS, tq, LANES), lambda b, i: (0, 0, 0))]
    args += [near_a, near_b]

    return pl.pallas_call(
        functools.partial(_attn_kernel, tq=tq, t_cur=t, past=past, lp=lp, kc=kc, topk=topk),
        grid=(nbatch, nq),
        in_specs=in_specs,
        out_specs=pl.BlockSpec((tq, D_MODEL), lambda b, i: (b * nq + i, 0)),
        out_shape=jax.ShapeDtypeStruct((nbatch * t, D_MODEL), jnp.bfloat16),
        scratch_shapes=[pltpu.VMEM((lp, kvw), jnp.bfloat16),
                        pltpu.VMEM((lp, kvw), jnp.bfloat16),
                        pltpu.VMEM((lp, IDX_DIM), jnp.bfloat16),
                        pltpu.VMEM((tq, lp), jnp.float32),
                        pltpu.VMEM((tq, lp), jnp.int32),
                        pltpu.VMEM((tq, lp), jnp.float32),
                        pltpu.VMEM((tq, lp), jnp.float32)],
        compiler_params=_cparams(("arbitrary", "arbitrary")),
        name="attn_past" if past else "attn",
    )(*args)


def _merge_kernel(oa_ref, ob_ref, wa_ref, wb_ref, ga_ref, gb_ref, o_ref):
    ya = _dot(oa_ref[...], wa_ref[...])
    yb = _dot(ob_ref[...], wb_ref[...])
    o_ref[...] = _bf(jax.nn.sigmoid(ga_ref[...]) * ya + jax.nn.sigmoid(gb_ref[...]) * yb)


def _merge_call(oa, ob, wa_bf, wb_bf, proj):
    m, d = oa.shape
    tm = min(1024, m)
    tn = 512
    return pl.pallas_call(
        _merge_kernel,
        grid=(m // tm, d // tn),
        in_specs=[pl.BlockSpec((tm, d), lambda i, j: (i, 0)),
                  pl.BlockSpec((tm, d), lambda i, j: (i, 0)),
                  pl.BlockSpec((d, tn), lambda i, j: (0, j)),
                  pl.BlockSpec((d, tn), lambda i, j: (0, j)),
                  pl.BlockSpec((tm, tn), lambda i, j: (i, OFF_GLA // tn + j)),
                  pl.BlockSpec((tm, tn), lambda i, j: (i, OFF_GLB // tn + j))],
        out_specs=pl.BlockSpec((tm, tn), lambda i, j: (i, j)),
        out_shape=jax.ShapeDtypeStruct((m, d), jnp.bfloat16),
        compiler_params=_cparams(("arbitrary", "arbitrary")),
        name="merge",
    )(oa, ob, wa_bf, wb_bf, proj, proj)


def _outproj_kernel(mg_ref, w_ref, x_ref, gate_ref, o_ref, *, tm):
    y = _dot(mg_ref[...], w_ref[...])
    for g in range(tm // GROUP):
        r = slice(g * GROUP, (g + 1) * GROUP)
        o_ref[r, :] = x_ref[r, :] + gate_ref[g:g + 1, :] * y[r]


def _outproj_call(merged, wo_bf, x, gate_g):
    m, d = x.shape
    tm = min(1024, m)
    tn = 512
    gpt = tm // GROUP
    return pl.pallas_call(
        functools.partial(_outproj_kernel, tm=tm),
        grid=(m // tm, d // tn),
        in_specs=[pl.BlockSpec((tm, d), lambda i, j: (i, 0)),
                  pl.BlockSpec((d, tn), lambda i, j: (0, j)),
                  pl.BlockSpec((tm, tn), lambda i, j: (i, j)),
                  pl.BlockSpec((gpt, tn), lambda i, j: (i, j))],
        out_specs=pl.BlockSpec((tm, tn), lambda i, j: (i, j)),
        out_shape=jax.ShapeDtypeStruct((m, d), jnp.float32),
        compiler_params=_cparams(("arbitrary", "arbitrary")),
        name="outproj",
    )(merged, wo_bf, x, gate_g)


def _rmsnorm_kernel(x_ref, w_ref, o_ref):
    x = x_ref[...]
    o_ref[...] = x * lax.rsqrt(jnp.mean(x * x, axis=-1, keepdims=True) + EPS) * w_ref[...]


def _rmsnorm_call(x, w):
    m, d = x.shape
    tm = min(512, m)
    return pl.pallas_call(
        _rmsnorm_kernel,
        grid=(m // tm,),
        in_specs=[pl.BlockSpec((tm, d), lambda i: (i, 0)),
                  pl.BlockSpec((1, d), lambda i: (0, 0))],
        out_specs=pl.BlockSpec((tm, d), lambda i: (i, 0)),
        out_shape=jax.ShapeDtypeStruct((m, d), jnp.float32),
        compiler_params=_cparams(("arbitrary",)),
        name="final_norm",
    )(x, w.reshape(1, d))


def _relayout_w_in(w):
    offs = [0]
    for s in _IN_SIZES:
        offs.append(offs[-1] + s)
    (qkv, za, ba, aa, qb, kb, vb, zb, qi, ki, wi, gla, glb) = [w[:, offs[n]:offs[n + 1]] for n in range(13)]
    d = w.shape[0]
    pad_sm = jnp.zeros((d, LANES - (IDX_DIM + 3 * GDN_HEADS)), w.dtype)
    cols = [qkv, za, qb, zb, gla, glb, qi, kb, vb, ki, ba, aa, wi, pad_sm]
    out = jnp.concatenate(cols, axis=1)
    pad = jnp.zeros((d, N_PROJ - out.shape[1]), w.dtype)
    return jnp.concatenate([out, pad], axis=1).astype(jnp.bfloat16)


def _rel_bucket(rel):
    nb = REL_BUCKETS // 2
    max_exact = nb // 2
    n = jnp.abs(rel)
    nf = jnp.maximum(n, 1).astype(jnp.float32)
    large = max_exact + (jnp.log(nf / max_exact) / math.log(REL_MAX_DIST / max_exact)
                         * (nb - max_exact)).astype(jnp.int32)
    large = jnp.minimum(large, nb - 1)
    return jnp.where(rel > 0, nb, 0) + jnp.where(n < max_exact, n, large)


def _bias_tables(rel_bias):
    tq = LANES
    trow = jnp.arange(tq)[:, None]
    col = jnp.arange(2 * LANES)[None, :]
    rel = (col - LANES) - trow
    tab = rel_bias[_rel_bucket(rel)]
    far = rel_bias[_rel_bucket(jnp.array(-REL_MAX_DIST))]
    tab = jnp.moveaxis(tab, 2, 0) - far[:, None, None]
    return far, tab[:, :, :LANES], tab[:, :, LANES:]


def _pad_rows8(a):
    z = jnp.zeros(a.shape[:-2] + (8 - a.shape[-2], a.shape[-1]), a.dtype)
    return jnp.concatenate([z, a], axis=-2)


def _lane_row(vals, off):
    r = jnp.zeros((1, LANES), jnp.float32)
    return r.at[0, off:off + vals.shape[0]].set(vals)


def _layer(x, mod, conv_prev, s0, past_kv, lw, tables, *, nbatch, t):
    (norm_w, w_in_bf, wconv8, lrow, drow, gw, wa_bf, wb_bf, wo_bf) = lw
    far, near_a, near_b = tables
    d = D_MODEL
    gpb = t // GROUP
    shift_g = jnp.repeat(mod[:, 0:d], gpb, axis=0)
    scale_g = jnp.repeat(mod[:, d:2 * d], gpb, axis=0)
    gate_g = jnp.repeat(mod[:, 2 * d:3 * d], gpb, axis=0)

    proj = _inproj_call(x, norm_w, scale_g, shift_g, w_in_bf)
    oa, s_new, tails = _gdn_call(proj, _pad_rows8(conv_prev), s0, wconv8, lrow, drow, gw, nbatch=nbatch, t=t)
    tq = min(LANES, t)
    ob = _attn_call(proj, far, near_a[:, :tq], near_b[:, :tq], nbatch=nbatch, t=t, past_kv=past_kv)
    merged = _merge_call(oa, ob, wa_bf, wb_bf, proj)
    x_new = _outproj_call(merged, wo_bf, x, gate_g)

    kvw = KV_HEADS * HEAD_DIM
    k_new = proj[:, OFF_KB:OFF_KB + kvw].reshape(nbatch, t, KV_HEADS, HEAD_DIM)
    v_new = proj[:, OFF_VB:OFF_VB + kvw].reshape(nbatch, t, KV_HEADS, HEAD_DIM)
    ki_new = proj[:, OFF_SM + SM_KI:OFF_SM + SM_KI + IDX_DIM].reshape(nbatch, t, IDX_DIM)
    conv_new = tails[:, 8 - (CONV_W - 1):, :]
    return x_new, (k_new, v_new, ki_new, s_new, conv_new)


def kernel(x_prompt, x_sample, c_prompt, c_sample, cache_k, cache_v, cache_idx_k, state_gdn, state_conv,
           norm_w, w_ada, b_ada, w_in, w_conv, a_log, dt_bias, gdn_norm_w, w_branch_a, w_branch_b,
           w_out, rel_bias, final_norm_w):
    depth = w_in.shape[0]
    bp, tp, d = x_prompt.shape
    bs, ts, _ = x_sample.shape
    past = cache_k.shape[2]
    kvw = KV_HEADS * HEAD_DIM

    mod = _ada_call(jnp.concatenate([c_prompt, c_sample], axis=0), w_ada, b_ada)
    tables = _bias_tables(rel_bias)

    xp = x_prompt.reshape(bp * tp, d)
    xs = x_sample.reshape(bs * ts, d)
    new_p, new_s = [], []
    for l in range(depth):
        wconv8 = jnp.concatenate([w_conv[l], jnp.zeros((8 - CONV_W, w_conv.shape[2]), w_conv.dtype)], axis=0)
        lw = (norm_w[l], _relayout_w_in(w_in[l]), wconv8,
              _lane_row(a_log[l], SM_AA), _lane_row(dt_bias[l], SM_AA), gdn_norm_w[l].reshape(1, GDN_DV),
              w_branch_a[l].astype(jnp.bfloat16), w_branch_b[l].astype(jnp.bfloat16),
              w_out[l].astype(jnp.bfloat16))
        conv0 = jnp.zeros((bp, CONV_W - 1, state_conv.shape[-1]), jnp.float32)
        s0 = jnp.zeros((bp,) + state_gdn.shape[2:], jnp.float32)
        xp, sp = _layer(xp, mod[l, :bp], conv0, s0, None, lw, tables, nbatch=bp, t=tp)
        past_kv = (cache_k[l].reshape(bs, past, kvw), cache_v[l].reshape(bs, past, kvw), cache_idx_k[l])
        xs, ss = _layer(xs, mod[l, bp:], state_conv[l], state_gdn[l], past_kv, lw, tables, nbatch=bs, t=ts)
        new_p.append(sp)
        new_s.append(ss)

    y_prompt = _rmsnorm_call(xp, final_norm_w).reshape(bp, tp, d)
    y_sample = _rmsnorm_call(xs, final_norm_w).reshape(bs, ts, d)
    outs_p = [jnp.stack([s[n] for s in new_p]) for n in range(5)]
    outs_s = [jnp.stack([s[n] for s in new_s]) for n in range(5)]
    return (y_prompt, y_sample, *outs_p, *outs_s)
```

```python
import functools
import math

import jax
import jax.numpy as jnp
from jax import lax
from jax.experimental import pallas as pl
from jax.experimental.pallas import tpu as pltpu

D_MODEL = 2048
CHUNK = 64
GDN_HEADS = 16
GDN_DK = 128
GDN_DV = 128
CONV_W = 4
ATT_HEADS = 16
KV_HEADS = 2
HEAD_DIM = 128
IDX_HEADS = 16
IDX_DIM = 64
TOPK_MAX = 256
REL_BUCKETS = 32
REL_MAX_DIST = 128
EPS = 1e-6

LANES = 128
VMEM_LIMIT = 56 * 1024 * 1024

W_QKV = 3 * GDN_HEADS * GDN_DK
OFF_QKV = 0
OFF_ZA = OFF_QKV + W_QKV
OFF_QB = OFF_ZA + D_MODEL
OFF_ZB = OFF_QB + D_MODEL
OFF_GLA = OFF_ZB + D_MODEL
OFF_GLB = OFF_GLA + D_MODEL
OFF_QI = OFF_GLB + D_MODEL
OFF_KB = OFF_QI + IDX_HEADS * IDX_DIM
OFF_VB = OFF_KB + KV_HEADS * HEAD_DIM
OFF_SM = OFF_VB + KV_HEADS * HEAD_DIM
SM_KI, SM_BA, SM_AA, SM_WI = 0, 64, 80, 96
N_PROJ = 18432
GROUP = 64

_IN_SIZES = (W_QKV, D_MODEL, GDN_HEADS, GDN_HEADS, D_MODEL, KV_HEADS * HEAD_DIM, KV_HEADS * HEAD_DIM,
             D_MODEL, IDX_HEADS * IDX_DIM, IDX_DIM, IDX_HEADS, D_MODEL, D_MODEL)


def _cparams(sem):
    return pltpu.CompilerParams(dimension_semantics=sem, vmem_limit_bytes=VMEM_LIMIT)


def _bf(x):
    return x.astype(jnp.bfloat16)


def _dot(a, b):
    return jnp.dot(a, b, preferred_element_type=jnp.float32)


def _dot_nt(a, b):
    return lax.dot_general(a, b, (((1,), (1,)), ((), ())), preferred_element_type=jnp.float32)


def _ada_kernel(c_ref, w_ref, b_ref, o_ref):
    c = c_ref[...]
    a = _bf(c * jax.nn.sigmoid(c))
    o_ref[0] = _dot(a, _bf(w_ref[0])) + b_ref[0]


def _ada_call(c_all, w_ada, b_ada):
    depth, d, n = w_ada.shape
    nb = c_all.shape[0]
    tn = 1024
    return pl.pallas_call(
        _ada_kernel,
        grid=(depth, n // tn),
        in_specs=[pl.BlockSpec((nb, d), lambda l, j: (0, 0)),
                  pl.BlockSpec((1, d, tn), lambda l, j: (l, 0, j)),
                  pl.BlockSpec((1, 1, tn), lambda l, j: (l, 0, j))],
        out_specs=pl.BlockSpec((1, nb, tn), lambda l, j: (l, 0, j)),
        out_shape=jax.ShapeDtypeStruct((depth, nb, n), jnp.float32),
        compiler_params=_cparams(("arbitrary", "arbitrary")),
        name="ada_mod",
    )(c_all, w_ada, b_ada.reshape(depth, 1, n))


def _inproj_kernel(x_ref, nw_ref, sc_ref, sh_ref, w_ref, o_ref, h_ref, *, tm):
    @pl.when(pl.program_id(1) == 0)
    def _():
        nw = nw_ref[...]

        def body(g, carry):
            rows = pl.ds(pl.multiple_of(g * GROUP, GROUP), GROUP)
            x = x_ref[rows, :]
            y = x * lax.rsqrt(jnp.mean(x * x, axis=-1, keepdims=True) + EPS) * nw
            hh = y * (1.0 + sc_ref[pl.ds(g, 1), :]) + sh_ref[pl.ds(g, 1), :]
            h_ref[rows, :] = _bf(hh)
            return carry

        lax.fori_loop(0, tm // GROUP, body, 0)

    o_ref[...] = _dot(h_ref[...], w_ref[...])


def _inproj_call(x, norm_w, scale_g, shift_g, w_bf):
    m, d = x.shape
    n = w_bf.shape[1]
    tm = min(1024, m)
    tn = 1024
    gpt = tm // GROUP
    return pl.pallas_call(
        functools.partial(_inproj_kernel, tm=tm),
        grid=(m // tm, n // tn),
        in_specs=[pl.BlockSpec((tm, d), lambda i, j: (i, 0)),
                  pl.BlockSpec((1, d), lambda i, j: (0, 0)),
                  pl.BlockSpec((gpt, d), lambda i, j: (i, 0)),
                  pl.BlockSpec((gpt, d), lambda i, j: (i, 0)),
                  pl.BlockSpec((d, tn), lambda i, j: (0, j))],
        out_specs=pl.BlockSpec((tm, tn), lambda i, j: (i, j)),
        out_shape=jax.ShapeDtypeStruct((m, n), jnp.float32),
        scratch_shapes=[pltpu.VMEM((tm, d), jnp.bfloat16)],
        compiler_params=_cparams(("arbitrary", "arbitrary")),
        name="inproj",
    )(x, norm_w.reshape(1, d), scale_g, shift_g, w_bf)


def _conv_silu(x, tail, w):
    xc = jnp.concatenate([tail, x], axis=0)
    y = x * w[CONV_W - 1:CONV_W]
    for s in range(1, CONV_W):
        y = y + pltpu.roll(xc, s, 0)[8:] * w[CONV_W - 1 - s:CONV_W - s]
    return y * jax.nn.sigmoid(y)


def _l2norm(x):
    return x * lax.rsqrt(jnp.sum(x * x, axis=-1, keepdims=True) + EPS)


def _softplus(x):
    return jnp.maximum(x, 0.0) + jnp.log1p(jnp.exp(-jnp.abs(x)))


INV_BASE = 8


def _unit_lower_inverses(As, ii, jj, eye):
    C = As[0].shape[0]
    sh = INV_BASE.bit_length() - 1
    Ns = [jnp.where((ii >> sh) == (jj >> sh), -A, 0.0) for A in As]
    Ps = [eye + N for N in Ns]
    m = 2
    while m < INV_BASE:
        Nbs = [_bf(N) for N in Ns]
        Ns = [_dot(Nb, Nb) for Nb in Nbs]
        Ps = [P + _dot(_bf(P), _bf(N)) for P, N in zip(Ps, Ns)]
        m *= 2
    s = INV_BASE
    while s < C:
        sh = s.bit_length() - 1
        off = ((ii >> (sh + 1)) == (jj >> (sh + 1))) & (((ii >> sh) & 1) == 1) & (((jj >> sh) & 1) == 0)
        Pbs = [_bf(P) for P in Ps]
        Xs = [_dot(Pb, _bf(jnp.where(off, A, 0.0))) for Pb, A in zip(Pbs, As)]
        Ps = [P - _dot(_bf(X), Pb) for P, X, Pb in zip(Ps, Xs, Pbs)]
        s *= 2
    return Ps


def _gdn_kernel(qkv_ref, z_ref, sm_ref, cp_ref, wc_ref, lrow_ref, drow_ref, gw_ref, s0_ref,
                o_ref, sout_ref, tout_ref, S_ref, tail_ref):
    i = pl.program_id(1)
    C = CHUNK
    hw = GDN_HEADS * GDN_DK

    @pl.when(i == 0)
    def _():
        S_ref[...] = s0_ref[0]
        tail_ref[...] = cp_ref[0]

    sm = sm_ref[...]
    beta_all = jax.nn.sigmoid(sm)
    g_all = -jnp.exp(lrow_ref[...]) * _softplus(sm + drow_ref[...])
    rowc = lax.broadcasted_iota(jnp.int32, (C, LANES), 0)
    gc = g_all
    s = 1
    while s < C:
        gc = gc + jnp.where(rowc >= s, pltpu.roll(gc, s, 0), 0.0)
        s *= 2
    glast = gc[C - 1:C, :]
    egc_all = jnp.exp(gc)
    ekd_all = jnp.exp(glast - gc)
    egl_all = jnp.exp(glast)
    gc_t = gc.T

    ii = lax.broadcasted_iota(jnp.int32, (C, C), 0)
    jj = lax.broadcasted_iota(jnp.int32, (C, C), 1)
    eye = jnp.where(ii == jj, 1.0, 0.0)
    gw = gw_ref[...]

    heads = range(GDN_HEADS)

    def col(p, h):
        return slice(p * hw + h * GDN_DK, p * hw + (h + 1) * GDN_DK)

    def conv(p, h):
        x = qkv_ref[:, col(p, h)]
        y = _conv_silu(x, tail_ref[:, col(p, h)], wc_ref[:, col(p, h)])
        tail_ref[:, col(p, h)] = x[C - 8:]
        return y

    def lane(a, l):
        return a[:, l:l + 1]

    ks = [_l2norm(conv(1, h)) for h in heads]
    qs = [_l2norm(conv(0, h)) * (GDN_DK ** -0.5) for h in heads]
    kbs = [k * lane(beta_all, SM_BA + h) for h, k in zip(heads, ks)]
    kqs = [_dot_nt(_bf(jnp.concatenate([kb, q], axis=0)), _bf(k)) for kb, q, k in zip(kbs, qs, ks)]
    decays = [jnp.where(ii >= jj,
                        jnp.exp(jnp.minimum(lane(gc, SM_AA + h) - gc_t[SM_AA + h:SM_AA + h + 1, :], 0.0)), 0.0)
              for h in heads]
    As = [jnp.where(ii > jj, kq[:C] * d, 0.0) for kq, d in zip(kqs, decays)]
    qks = [_bf(kq[C:] * d) for kq, d in zip(kqs, decays)]
    Ps = _unit_lower_inverses(As, ii, jj, eye)
    vs = [conv(2, h) for h in heads]
    rhs = [_bf(jnp.concatenate([v * lane(beta_all, SM_BA + h), kb * lane(egc_all, SM_AA + h)], axis=1))
           for h, v, kb in zip(heads, vs, kbs)]
    uws = [_dot(_bf(P), r) for P, r in zip(Ps, rhs)]
    Ss = [S_ref[h] for h in heads]
    wqs = [_dot(_bf(jnp.concatenate([uw[:, GDN_DV:], q * lane(egc_all, SM_AA + h)], axis=0)), _bf(S))
           for h, uw, q, S in zip(heads, uws, qs, Ss)]
    vnbs = [_bf(uw[:, :GDN_DV] - wq[:C]) for uw, wq in zip(uws, wqs)]
    kdts = [_bf((k * lane(ekd_all, SM_AA + h)).T) for h, k in zip(heads, ks)]
    for h in heads:
        S_ref[h] = Ss[h] * lane(egl_all, SM_AA + h) + _dot(kdts[h], vnbs[h])
    os_ = [wq[C:] + _dot(qk, vnb) for wq, qk, vnb in zip(wqs, qks, vnbs)]
    for h in heads:
        o = os_[h]
        o = o * lax.rsqrt(jnp.mean(o * o, axis=-1, keepdims=True) + EPS) * gw
        z = z_ref[:, col(0, h)]
        o_ref[:, col(0, h)] = _bf(o * (z * jax.nn.sigmoid(z)))

    @pl.when(i == pl.num_programs(1) - 1)
    def _():
        sout_ref[0] = S_ref[...]
        tout_ref[0] = tail_ref[...]


def _gdn_call(proj, conv_prev8, s0, wconv8, lrow, drow, gw, *, nbatch, t):
    nt = t // CHUNK
    hh = GDN_HEADS
    const = lambda b, i: (0, 0)
    state_spec = pl.BlockSpec((1, hh, GDN_DK, GDN_DV), lambda b, i: (b, 0, 0, 0))
    tail_spec = pl.BlockSpec((1, 8, W_QKV), lambda b, i: (b, 0, 0))
    return pl.pallas_call(
        _gdn_kernel,
        grid=(nbatch, nt),
        in_specs=[pl.BlockSpec((CHUNK, W_QKV), lambda b, i: (b * nt + i, OFF_QKV // W_QKV)),
                  pl.BlockSpec((CHUNK, D_MODEL), lambda b, i: (b * nt + i, OFF_ZA // D_MODEL)),
                  pl.BlockSpec((CHUNK, LANES), lambda b, i: (b * nt + i, OFF_SM // LANES)),
                  tail_spec,
                  pl.BlockSpec((8, W_QKV), const),
                  pl.BlockSpec((1, LANES), const),
                  pl.BlockSpec((1, LANES), const),
                  pl.BlockSpec((1, GDN_DV), const),
                  state_spec],
        out_specs=[pl.BlockSpec((CHUNK, hh * GDN_DV), lambda b, i: (b * nt + i, 0)),
                   state_spec, tail_spec],
        out_shape=[jax.ShapeDtypeStruct((nbatch * t, hh * GDN_DV), jnp.bfloat16),
                   jax.ShapeDtypeStruct((nbatch, hh, GDN_DK, GDN_DV), jnp.float32),
                   jax.ShapeDtypeStruct((nbatch, 8, W_QKV), jnp.float32)],
        scratch_shapes=[pltpu.VMEM((hh, GDN_DK, GDN_DV), jnp.float32),
                        pltpu.VMEM((8, W_QKV), jnp.float32)],
        compiler_params=_cparams(("arbitrary", "arbitrary")),
        name="gdn",
    )(proj, proj, proj, conv_prev8, wconv8, lrow, drow, gw, s0)


_INT_MIN = -2147483648
_KEY_NEG_INF = -2139095041
ATT_GROUP = ATT_HEADS // KV_HEADS


def _attn_kernel(*refs, tq, t_cur, past, lp, kc, topk):
    if past:
        (qb_ref, zb_ref, qi_ref, smq_ref, k_ref, v_ref, smk_ref, kp_ref, vp_ref, kip_ref, na_ref, nb_ref,
         o_ref, kbf, vbf, kibf, key_ref, madd_ref, lg_ref, qs_ref, qis_ref, wb_ref, acc_ref, den_ref,
         mb_ref) = refs
    else:
        (qb_ref, zb_ref, qi_ref, smq_ref, k_ref, v_ref, smk_ref, na_ref, nb_ref,
         o_ref, kbf, vbf, kibf, key_ref, madd_ref, lg_ref, qs_ref, qis_ref, wb_ref, acc_ref, den_ref,
         mb_ref) = refs
    i = pl.program_id(1)
    nreal = past + t_cur
    nkc = lp // kc
    nl = kc // LANES
    G = ATT_GROUP
    gt = G * tq

    @pl.when(i == 0)
    def _():
        if past:
            kbf[0:past, :] = _bf(kp_ref[0])
            vbf[0:past, :] = _bf(vp_ref[0])
            kibf[0:past, :] = _bf(kip_ref[0])
        kbf[past:nreal, :] = _bf(k_ref[...])
        vbf[past:nreal, :] = _bf(v_ref[...])
        kibf[past:nreal, :] = _bf(smk_ref[:, SM_KI:SM_KI + IDX_DIM])
        if lp > nreal:
            kbf[nreal:lp, :] = jnp.zeros((lp - nreal, KV_HEADS * HEAD_DIM), jnp.bfloat16)
            vbf[nreal:lp, :] = jnp.zeros((lp - nreal, KV_HEADS * HEAD_DIM), jnp.bfloat16)
            kibf[nreal:lp, :] = jnp.zeros((lp - nreal, IDX_DIM), jnp.bfloat16)

    q0 = past + i * tq
    trow = lax.broadcasted_iota(jnp.int32, (tq, 1), 0)
    lim = jnp.minimum(((q0 + trow) // CHUNK + 1) * CHUNK, nreal)
    lim_max = jnp.minimum(((q0 + tq - 1) // CHUNK + 1) * CHUNK, nreal)
    nvis = jnp.minimum((lim_max + kc - 1) // kc, nkc)

    def lanes_of(c, j=0):
        return pl.ds(pl.multiple_of(c * kc + j * LANES, LANES), LANES)

    def chunk_of(c):
        return pl.ds(pl.multiple_of(c * kc, kc), kc)

    wi = smq_ref[:, SM_WI:SM_WI + IDX_HEADS] * ((IDX_HEADS ** -0.5) * (IDX_DIM ** -0.5))
    qi = qi_ref[...]
    for hh in range(IDX_HEADS):
        r = slice(hh * tq, (hh + 1) * tq)
        qis_ref[r, :] = _bf(qi[:, hh * IDX_DIM:(hh + 1) * IDX_DIM])
        wb_ref[r, :] = jnp.broadcast_to(wi[:, hh:hh + 1], (tq, LANES))

    def score_chunk(c, carry):
        d = _dot_nt(qis_ref[...], kibf[chunk_of(c), :])
        acc = jnp.zeros((tq, kc), jnp.float32)
        for hh in range(IDX_HEADS):
            r = slice(hh * tq, (hh + 1) * tq)
            acc = acc + jnp.concatenate([wb_ref[r, :]] * nl, axis=1) * jnp.maximum(d[r], 0.0)
        spos = c * kc + lax.broadcasted_iota(jnp.int32, (tq, kc), 1)
        acc = jnp.where(spos < lim, acc, -jnp.inf)
        bits = pltpu.bitcast(acc, jnp.int32)
        key_ref[:, chunk_of(c)] = jnp.where(bits < 0, bits ^ 0x7FFFFFFF, bits)
        return carry

    lax.fori_loop(0, nvis, score_chunk, 0)

    def count_ge(cand):
        candb = jnp.broadcast_to(cand, (tq, LANES))

        def body(c, acc):
            for j in range(nl):
                acc = acc + jnp.where(key_ref[:, lanes_of(c, j)] >= candb, 1.0, 0.0)
            return acc

        acc = lax.fori_loop(0, nvis, body, jnp.zeros((tq, LANES), jnp.float32))
        return jnp.sum(acc, axis=1, keepdims=True)

    def bit_step(it, tau_u):
        cand_u = tau_u | lax.shift_left(jnp.int32(1), 31 - it)
        return jnp.where(count_ge(cand_u ^ _INT_MIN) >= float(topk), cand_u, tau_u)

    tau = lax.fori_loop(0, 32, bit_step, jnp.zeros((tq, 1), jnp.int32)) ^ _INT_MIN
    cnt_ge = count_ge(tau)
    cnt_gt = count_ge(tau + 1)
    need = float(topk) - cnt_gt
    taub = jnp.broadcast_to(tau, (tq, LANES))
    excess = (cnt_ge > float(topk)) & (tau > _KEY_NEG_INF)
    any_excess = jnp.max(jnp.where(excess, 1, 0))

    def visible(c, j):
        spos = c * kc + j * LANES + lax.broadcasted_iota(jnp.int32, (tq, LANES), 1)
        return spos < lim

    @pl.when(any_excess == 0)
    def _():
        def body(c, carry):
            for j in range(nl):
                sel = (key_ref[:, lanes_of(c, j)] >= taub) & visible(c, j)
                madd_ref[:, lanes_of(c, j)] = jnp.where(sel, 0.0, -jnp.inf)
            return carry

        lax.fori_loop(0, nvis, body, 0)

    @pl.when(any_excess != 0)
    def _():
        ua = lax.broadcasted_iota(jnp.int32, (LANES, LANES), 0)
        ub = lax.broadcasted_iota(jnp.int32, (LANES, LANES), 1)
        upper = jnp.where(ua <= ub, 1.0, 0.0).astype(jnp.bfloat16)

        def body(c, carry):
            for j in range(nl):
                key = key_ref[:, lanes_of(c, j)]
                eq = key == taub
                pref = _dot(jnp.where(eq, 1.0, 0.0).astype(jnp.bfloat16), upper) + carry
                sel = ((key > taub) | (eq & (pref <= need))) & visible(c, j)
                madd_ref[:, lanes_of(c, j)] = jnp.where(sel, 0.0, -jnp.inf)
                carry = jnp.broadcast_to(pref[:, LANES - 1:LANES], (tq, LANES))
            return carry

        lax.fori_loop(0, nvis, body, jnp.zeros((tq, LANES), jnp.float32))

    scale = HEAD_DIM ** -0.5
    for hd in range(ATT_HEADS):
        qs_ref[hd * tq:(hd + 1) * tq, :] = _bf(qb_ref[:, hd * HEAD_DIM:(hd + 1) * HEAD_DIM] * scale)
    off_b = pl.ds(pl.multiple_of(q0, LANES), LANES)
    off_a = pl.ds(pl.multiple_of(jnp.maximum(q0 - LANES, 0), LANES), LANES)

    for n in range(KV_HEADS):
        ncol = slice(n * HEAD_DIM, (n + 1) * HEAD_DIM)
        grows = slice(n * gt, (n + 1) * gt)

        def logit_chunk(c, carry):
            sc = _dot_nt(qs_ref[grows, :], kbf[chunk_of(c), ncol])
            ma = madd_ref[:, chunk_of(c)]
            for g in range(G):
                lg_ref[g * tq:(g + 1) * tq, chunk_of(c)] = sc[g * tq:(g + 1) * tq] + ma
            return carry

        lax.fori_loop(0, nvis, logit_chunk, 0)

        for g in range(G):
            r = slice(g * tq, (g + 1) * tq)
            lg_ref[r, off_b] = lg_ref[r, off_b] + nb_ref[n * G + g]

        @pl.when(q0 >= LANES)
        def _():
            for g in range(G):
                r = slice(g * tq, (g + 1) * tq)
                lg_ref[r, off_a] = lg_ref[r, off_a] + na_ref[n * G + g]

        for g in range(G):
            r = slice(g * tq, (g + 1) * tq)

            def max_chunk(c, mt):
                for j in range(nl):
                    mt = jnp.maximum(mt, lg_ref[r, lanes_of(c, j)])
                return mt

            mt = lax.fori_loop(0, nvis, max_chunk, jnp.full((tq, LANES), -jnp.inf, jnp.float32))
            mb_ref[r, :] = jnp.broadcast_to(jnp.max(mt, axis=1, keepdims=True), (tq, LANES))

        acc_ref[...] = jnp.zeros((gt, HEAD_DIM), jnp.float32)
        den_ref[...] = jnp.zeros((gt, LANES), jnp.float32)

        def pv_chunk(c, carry):
            p = jnp.exp(lg_ref[:, chunk_of(c)] - jnp.concatenate([mb_ref[...]] * nl, axis=1))
            den = den_ref[...]
            for j in range(nl):
                den = den + p[:, j * LANES:(j + 1) * LANES]
            den_ref[...] = den
            acc_ref[...] = acc_ref[...] + _dot(_bf(p), vbf[chunk_of(c), ncol])
            return carry

        lax.fori_loop(0, nvis, pv_chunk, 0)

        for g in range(G):
            r = slice(g * tq, (g + 1) * tq)
            hcol = slice((n * G + g) * HEAD_DIM, (n * G + g + 1) * HEAD_DIM)
            den = jnp.sum(den_ref[r, :], axis=1, keepdims=True)
            z = zb_ref[:, hcol]
            o_ref[:, hcol] = _bf((acc_ref[r, :] / den) * (z * jax.nn.sigmoid(z)))


def _attn_call(proj, near_a, near_b, *, nbatch, t, past_kv=None):
    past = 0 if past_kv is None else past_kv[0].shape[1]
    tq = min(128, t)
    nq = t // tq
    nreal = past + t
    kc = 512 if nreal % 512 == 0 else 384
    if nreal < kc:
        kc = LANES * (-(-nreal // LANES))
    lp = kc * (-(-nreal // kc))
    assert past % LANES == 0 and past + LANES * (-(-t // LANES)) <= lp
    topk = min(TOPK_MAX, nreal // 4)
    kvw = KV_HEADS * HEAD_DIM
    gt = ATT_GROUP * tq

    in_specs = [pl.BlockSpec((tq, D_MODEL), lambda b, i: (b * nq + i, OFF_QB // D_MODEL)),
                pl.BlockSpec((tq, D_MODEL), lambda b, i: (b * nq + i, OFF_ZB // D_MODEL)),
                pl.BlockSpec((tq, IDX_HEADS * IDX_DIM), lambda b, i: (b * nq + i, OFF_QI // (IDX_HEADS * IDX_DIM))),
                pl.BlockSpec((tq, LANES), lambda b, i: (b * nq + i, OFF_SM // LANES)),
                pl.BlockSpec((t, kvw), lambda b, i: (b, OFF_KB // kvw)),
                pl.BlockSpec((t, kvw), lambda b, i: (b, OFF_VB // kvw)),
                pl.BlockSpec((t, LANES), lambda b, i: (b, OFF_SM // LANES))]
    args = [proj, proj, proj, proj, proj, proj, proj]
    if past:
        in_specs += [pl.BlockSpec((1, past, kvw), lambda b, i: (b, 0, 0)),
                     pl.BlockSpec((1, past, kvw), lambda b, i: (b, 0, 0)),
                     pl.BlockSpec((1, past, IDX_DIM), lambda b, i: (b, 0, 0))]
        args += list(past_kv)
    in_specs += [pl.BlockSpec((ATT_HEADS, tq, LANES), lambda b, i: (0, 0, 0)),
                 pl.BlockSpec((ATT_HEADS, tq, LANES), lambda b, i: (0, 0, 0))]
    args += [near_a, near_b]

    return pl.pallas_call(
        functools.partial(_attn_kernel, tq=tq, t_cur=t, past=past, lp=lp, kc=kc, topk=topk),
        grid=(nbatch, nq),
        in_specs=in_specs,
        out_specs=pl.BlockSpec((tq, D_MODEL), lambda b, i: (b * nq + i, 0)),
        out_shape=jax.ShapeDtypeStruct((nbatch * t, D_MODEL), jnp.bfloat16),
        scratch_shapes=[pltpu.VMEM((lp, kvw), jnp.bfloat16),
                        pltpu.VMEM((lp, kvw), jnp.bfloat16),
                        pltpu.VMEM((lp, IDX_DIM), jnp.bfloat16),
                        pltpu.VMEM((tq, lp), jnp.int32),
                        pltpu.VMEM((tq, lp), jnp.float32),
                        pltpu.VMEM((gt, lp), jnp.float32),
                        pltpu.VMEM((ATT_HEADS * tq, HEAD_DIM), jnp.bfloat16),
                        pltpu.VMEM((IDX_HEADS * tq, IDX_DIM), jnp.bfloat16),
                        pltpu.VMEM((IDX_HEADS * tq, LANES), jnp.float32),
                        pltpu.VMEM((gt, HEAD_DIM), jnp.float32),
                        pltpu.VMEM((gt, LANES), jnp.float32),
                        pltpu.VMEM((gt, LANES), jnp.float32)],
        compiler_params=_cparams(("arbitrary", "arbitrary")),
        name="attn_past" if past else "attn",
    )(*args)


def _merge_kernel(oa_ref, ob_ref, wa_ref, wb_ref, ga_ref, gb_ref, o_ref):
    ya = _dot(oa_ref[...], wa_ref[...])
    yb = _dot(ob_ref[...], wb_ref[...])
    o_ref[...] = _bf(jax.nn.sigmoid(ga_ref[...]) * ya + jax.nn.sigmoid(gb_ref[...]) * yb)


def _merge_call(oa, ob, wa_bf, wb_bf, proj):
    m, d = oa.shape
    tm = min(1024, m)
    tn = 512
    return pl.pallas_call(
        _merge_kernel,
        grid=(m // tm, d // tn),
        in_specs=[pl.BlockSpec((tm, d), lambda i, j: (i, 0)),
                  pl.BlockSpec((tm, d), lambda i, j: (i, 0)),
                  pl.BlockSpec((d, tn), lambda i, j: (0, j)),
                  pl.BlockSpec((d, tn), lambda i, j: (0, j)),
                  pl.BlockSpec((tm, tn), lambda i, j: (i, OFF_GLA // tn + j)),
                  pl.BlockSpec((tm, tn), lambda i, j: (i, OFF_GLB // tn + j))],
        out_specs=pl.BlockSpec((tm, tn), lambda i, j: (i, j)),
        out_shape=jax.ShapeDtypeStruct((m, d), jnp.bfloat16),
        compiler_params=_cparams(("arbitrary", "arbitrary")),
        name="merge",
    )(oa, ob, wa_bf, wb_bf, proj, proj)


def _outproj_kernel(mg_ref, w_ref, x_ref, gate_ref, o_ref, *, tm):
    y = _dot(mg_ref[...], w_ref[...])
    for g in range(tm // GROUP):
        r = slice(g * GROUP, (g + 1) * GROUP)
        o_ref[r, :] = x_ref[r, :] + gate_ref[g:g + 1, :] * y[r]


def _outproj_call(merged, wo_bf, x, gate_g):
    m, d = x.shape
    tm = min(1024, m)
    tn = 512
    gpt = tm // GROUP
    return pl.pallas_call(
        functools.partial(_outproj_kernel, tm=tm),
        grid=(m // tm, d // tn),
        in_specs=[pl.BlockSpec((tm, d), lambda i, j: (i, 0)),
                  pl.BlockSpec((d, tn), lambda i, j: (0, j)),
                  pl.BlockSpec((tm, tn), lambda i, j: (i, j)),
                  pl.BlockSpec((gpt, tn), lambda i, j: (i, j))],
        out_specs=pl.BlockSpec((tm, tn), lambda i, j: (i, j)),
        out_shape=jax.ShapeDtypeStruct((m, d), jnp.float32),
        compiler_params=_cparams(("arbitrary", "arbitrary")),
        name="outproj",
    )(merged, wo_bf, x, gate_g)


def _rmsnorm_kernel(x_ref, w_ref, o_ref):
    x = x_ref[...]
    o_ref[...] = x * lax.rsqrt(jnp.mean(x * x, axis=-1, keepdims=True) + EPS) * w_ref[...]


def _rmsnorm_call(x, w):
    m, d = x.shape
    tm = min(512, m)
    return pl.pallas_call(
        _rmsnorm_kernel,
        grid=(m // tm,),
        in_specs=[pl.BlockSpec((tm, d), lambda i: (i, 0)),
                  pl.BlockSpec((1, d), lambda i: (0, 0))],
        out_specs=pl.BlockSpec((tm, d), lambda i: (i, 0)),
        out_shape=jax.ShapeDtypeStruct((m, d), jnp.float32),
        compiler_params=_cparams(("arbitrary",)),
        name="final_norm",
    )(x, w.reshape(1, d))


def _relayout_w_in(w):
    offs = [0]
    for s in _IN_SIZES:
        offs.append(offs[-1] + s)
    (qkv, za, ba, aa, qb, kb, vb, zb, qi, ki, wi, gla, glb) = [w[:, offs[n]:offs[n + 1]] for n in range(13)]
    d = w.shape[0]
    pad_sm = jnp.zeros((d, LANES - (IDX_DIM + 3 * GDN_HEADS)), w.dtype)
    cols = [qkv, za, qb, zb, gla, glb, qi, kb, vb, ki, ba, aa, wi, pad_sm]
    out = jnp.concatenate(cols, axis=1)
    pad = jnp.zeros((d, N_PROJ - out.shape[1]), w.dtype)
    return jnp.concatenate([out, pad], axis=1).astype(jnp.bfloat16)


def _rel_bucket(rel):
    nb = REL_BUCKETS // 2
    max_exact = nb // 2
    n = jnp.abs(rel)
    nf = jnp.maximum(n, 1).astype(jnp.float32)
    large = max_exact + (jnp.log(nf / max_exact) / math.log(REL_MAX_DIST / max_exact)
                         * (nb - max_exact)).astype(jnp.int32)
    large = jnp.minimum(large, nb - 1)
    return jnp.where(rel > 0, nb, 0) + jnp.where(n < max_exact, n, large)


def _bias_tables(rel_bias):
    tq = LANES
    trow = jnp.arange(tq)[:, None]
    col = jnp.arange(2 * LANES)[None, :]
    rel = (col - LANES) - trow
    tab = rel_bias[_rel_bucket(rel)]
    far = rel_bias[_rel_bucket(jnp.array(-REL_MAX_DIST))]
    tab = jnp.moveaxis(tab, 2, 0) - far[:, None, None]
    return tab[:, :, :LANES], tab[:, :, LANES:]


def _pad_rows8(a):
    z = jnp.zeros(a.shape[:-2] + (8 - a.shape[-2], a.shape[-1]), a.dtype)
    return jnp.concatenate([z, a], axis=-2)


def _lane_row(vals, off):
    r = jnp.zeros((1, LANES), jnp.float32)
    return r.at[0, off:off + vals.shape[0]].set(vals)


def _layer(x, mod, conv_prev, s0, past_kv, lw, tables, *, nbatch, t):
    (norm_w, w_in_bf, wconv8, lrow, drow, gw, wa_bf, wb_bf, wo_bf) = lw
    near_a, near_b = tables
    d = D_MODEL
    gpb = t // GROUP
    shift_g = jnp.repeat(mod[:, 0:d], gpb, axis=0)
    scale_g = jnp.repeat(mod[:, d:2 * d], gpb, axis=0)
    gate_g = jnp.repeat(mod[:, 2 * d:3 * d], gpb, axis=0)

    proj = _inproj_call(x, norm_w, scale_g, shift_g, w_in_bf)
    oa, s_new, tails = _gdn_call(proj, _pad_rows8(conv_prev), s0, wconv8, lrow, drow, gw, nbatch=nbatch, t=t)
    tq = min(LANES, t)
    ob = _attn_call(proj, near_a[:, :tq], near_b[:, :tq], nbatch=nbatch, t=t, past_kv=past_kv)
    merged = _merge_call(oa, ob, wa_bf, wb_bf, proj)
    x_new = _outproj_call(merged, wo_bf, x, gate_g)

    kvw = KV_HEADS * HEAD_DIM
    k_new = proj[:, OFF_KB:OFF_KB + kvw].reshape(nbatch, t, KV_HEADS, HEAD_DIM)
    v_new = proj[:, OFF_VB:OFF_VB + kvw].reshape(nbatch, t, KV_HEADS, HEAD_DIM)
    ki_new = proj[:, OFF_SM + SM_KI:OFF_SM + SM_KI + IDX_DIM].reshape(nbatch, t, IDX_DIM)
    conv_new = tails[:, 8 - (CONV_W - 1):, :]
    return x_new, (k_new, v_new, ki_new, s_new, conv_new)


def kernel(x_prompt, x_sample, c_prompt, c_sample, cache_k, cache_v, cache_idx_k, state_gdn, state_conv,
           norm_w, w_ada, b_ada, w_in, w_conv, a_log, dt_bias, gdn_norm_w, w_branch_a, w_branch_b,
           w_out, rel_bias, final_norm_w):
    depth = w_in.shape[0]
    bp, tp, d = x_prompt.shape
    bs, ts, _ = x_sample.shape
    past = cache_k.shape[2]
    kvw = KV_HEADS * HEAD_DIM

    mod = _ada_call(jnp.concatenate([c_prompt, c_sample], axis=0), w_ada, b_ada)
    tables = _bias_tables(rel_bias)

    xp = x_prompt.reshape(bp * tp, d)
    xs = x_sample.reshape(bs * ts, d)
    new_p, new_s = [], []
    for l in range(depth):
        wconv8 = jnp.concatenate([w_conv[l], jnp.zeros((8 - CONV_W, w_conv.shape[2]), w_conv.dtype)], axis=0)
        lw = (norm_w[l], _relayout_w_in(w_in[l]), wconv8,
              _lane_row(a_log[l], SM_AA), _lane_row(dt_bias[l], SM_AA), gdn_norm_w[l].reshape(1, GDN_DV),
              w_branch_a[l].astype(jnp.bfloat16), w_branch_b[l].astype(jnp.bfloat16),
              w_out[l].astype(jnp.bfloat16))
        conv0 = jnp.zeros((bp, CONV_W - 1, state_conv.shape[-1]), jnp.float32)
        s0 = jnp.zeros((bp,) + state_gdn.shape[2:], jnp.float32)
        xp, sp = _layer(xp, mod[l, :bp], conv0, s0, None, lw, tables, nbatch=bp, t=tp)
        past_kv = (cache_k[l].reshape(bs, past, kvw), cache_v[l].reshape(bs, past, kvw), cache_idx_k[l])
        xs, ss = _layer(xs, mod[l, bp:], state_conv[l], state_gdn[l], past_kv, lw, tables, nbatch=bs, t=ts)
        new_p.append(sp)
        new_s.append(ss)

    y_prompt = _rmsnorm_call(xp, final_norm_w).reshape(bp, tp, d)
    y_sample = _rmsnorm_call(xs, final_norm_w).reshape(bs, ts, d)
    outs_p = [jnp.stack([s[n] for s in new_p]) for n in range(5)]
    outs_s = [jnp.stack([s[n] for s in new_s]) for n in range(5)]
    return (y_prompt, y_sample, *outs_p, *outs_s)
```

```python
import functools
import math

import jax
import jax.numpy as jnp
from jax import lax
from jax.experimental import pallas as pl
from jax.experimental.pallas import tpu as pltpu

D_MODEL = 2048
CHUNK = 64
GDN_HEADS = 16
GDN_DK = 128
GDN_DV = 128
CONV_W = 4
ATT_HEADS = 16
KV_HEADS = 2
HEAD_DIM = 128
IDX_HEADS = 16
IDX_DIM = 64
TOPK_MAX = 256
REL_BUCKETS = 32
REL_MAX_DIST = 128
EPS = 1e-6

LANES = 128
VMEM_LIMIT = 56 * 1024 * 1024

W_QKV = 3 * GDN_HEADS * GDN_DK
OFF_QKV = 0
OFF_ZA = OFF_QKV + W_QKV
OFF_QB = OFF_ZA + D_MODEL
OFF_ZB = OFF_QB + D_MODEL
OFF_GLA = OFF_ZB + D_MODEL
OFF_GLB = OFF_GLA + D_MODEL
OFF_QI = OFF_GLB + D_MODEL
OFF_KB = OFF_QI + IDX_HEADS * IDX_DIM
OFF_VB = OFF_KB + KV_HEADS * HEAD_DIM
OFF_SM = OFF_VB + KV_HEADS * HEAD_DIM
SM_KI, SM_BA, SM_AA, SM_WI = 0, 64, 80, 96
N_PROJ = 18432
GROUP = 64

_IN_SIZES = (W_QKV, D_MODEL, GDN_HEADS, GDN_HEADS, D_MODEL, KV_HEADS * HEAD_DIM, KV_HEADS * HEAD_DIM,
             D_MODEL, IDX_HEADS * IDX_DIM, IDX_DIM, IDX_HEADS, D_MODEL, D_MODEL)


def _cparams(sem):
    return pltpu.CompilerParams(dimension_semantics=sem, vmem_limit_bytes=VMEM_LIMIT)


def _bf(x):
    return x.astype(jnp.bfloat16)


def _dot(a, b):
    return jnp.dot(a, b, preferred_element_type=jnp.float32)


def _dot_nt(a, b):
    return lax.dot_general(a, b, (((1,), (1,)), ((), ())), preferred_element_type=jnp.float32)


def _ada_kernel(c_ref, w_ref, b_ref, o_ref):
    c = c_ref[...]
    a = _bf(c * jax.nn.sigmoid(c))
    o_ref[0] = _dot(a, _bf(w_ref[0])) + b_ref[0]


def _ada_call(c_all, w_ada, b_ada):
    depth, d, n = w_ada.shape
    nb = c_all.shape[0]
    tn = 1024
    return pl.pallas_call(
        _ada_kernel,
        grid=(depth, n // tn),
        in_specs=[pl.BlockSpec((nb, d), lambda l, j: (0, 0)),
                  pl.BlockSpec((1, d, tn), lambda l, j: (l, 0, j)),
                  pl.BlockSpec((1, 1, tn), lambda l, j: (l, 0, j))],
        out_specs=pl.BlockSpec((1, nb, tn), lambda l, j: (l, 0, j)),
        out_shape=jax.ShapeDtypeStruct((depth, nb, n), jnp.float32),
        compiler_params=_cparams(("arbitrary", "arbitrary")),
        name="ada_mod",
    )(c_all, w_ada, b_ada.reshape(depth, 1, n))


def _inproj_kernel(x_ref, nw_ref, sc_ref, sh_ref, w_ref, o_ref, h_ref, *, tm):
    @pl.when(pl.program_id(1) == 0)
    def _():
        nw = nw_ref[...]

        def body(g, carry):
            rows = pl.ds(pl.multiple_of(g * GROUP, GROUP), GROUP)
            x = x_ref[rows, :]
            y = x * lax.rsqrt(jnp.mean(x * x, axis=-1, keepdims=True) + EPS) * nw
            hh = y * (1.0 + sc_ref[pl.ds(g, 1), :]) + sh_ref[pl.ds(g, 1), :]
            h_ref[rows, :] = _bf(hh)
            return carry

        lax.fori_loop(0, tm // GROUP, body, 0)

    o_ref[...] = _dot(h_ref[...], w_ref[...])


def _inproj_call(x, norm_w, scale_g, shift_g, w_bf):
    m, d = x.shape
    n = w_bf.shape[1]
    tm = min(1024, m)
    tn = 1024
    gpt = tm // GROUP
    return pl.pallas_call(
        functools.partial(_inproj_kernel, tm=tm),
        grid=(m // tm, n // tn),
        in_specs=[pl.BlockSpec((tm, d), lambda i, j: (i, 0)),
                  pl.BlockSpec((1, d), lambda i, j: (0, 0)),
                  pl.BlockSpec((gpt, d), lambda i, j: (i, 0)),
                  pl.BlockSpec((gpt, d), lambda i, j: (i, 0)),
                  pl.BlockSpec((d, tn), lambda i, j: (0, j))],
        out_specs=pl.BlockSpec((tm, tn), lambda i, j: (i, j)),
        out_shape=jax.ShapeDtypeStruct((m, n), jnp.float32),
        scratch_shapes=[pltpu.VMEM((tm, d), jnp.bfloat16)],
        compiler_params=_cparams(("arbitrary", "arbitrary")),
        name="inproj",
    )(x, norm_w.reshape(1, d), scale_g, shift_g, w_bf)


def _silu(x):
    hx = 0.5 * x
    return hx * jnp.tanh(hx) + hx


def _l2norm(x):
    return x * lax.rsqrt(jnp.sum(x * x, axis=-1, keepdims=True) + EPS)


def _softplus(x):
    return jnp.maximum(x, 0.0) + jnp.log1p(jnp.exp(-jnp.abs(x)))


INV_BASE = 8


def _unit_lower_inverses(As, ii, jj, eye):
    C = As[0].shape[0]
    sh = INV_BASE.bit_length() - 1
    Ns = [jnp.where((ii >> sh) == (jj >> sh), -A, 0.0) for A in As]
    Ps = [eye + N for N in Ns]
    m = 2
    while m < INV_BASE:
        Nbs = [_bf(N) for N in Ns]
        Ns = [_dot(Nb, Nb) for Nb in Nbs]
        Ps = [P + _dot(_bf(P), _bf(N)) for P, N in zip(Ps, Ns)]
        m *= 2
    s = INV_BASE
    while s < C:
        sh = s.bit_length() - 1
        off = ((ii >> (sh + 1)) == (jj >> (sh + 1))) & (((ii >> sh) & 1) == 1) & (((jj >> sh) & 1) == 0)
        Pbs = [_bf(P) for P in Ps]
        Xs = [_dot(Pb, _bf(jnp.where(off, A, 0.0))) for Pb, A in zip(Pbs, As)]
        Ps = [P - _dot(_bf(X), Pb) for P, X, Pb in zip(Ps, Xs, Pbs)]
        s *= 2
    return Ps


def _gdn_kernel(qkv_ref, z_ref, sm_ref, cp_ref, wc_ref, lrow_ref, drow_ref, gw_ref, s0_ref,
                o_ref, sout_ref, tout_ref, S_ref, xe_ref):
    i = pl.program_id(1)
    C = CHUNK
    hw = GDN_HEADS * GDN_DK

    @pl.when(i == 0)
    def _():
        S_ref[...] = s0_ref[0]
        xe_ref[0:8, :] = cp_ref[0]

    @pl.when(i > 0)
    def _():
        xe_ref[0:8, :] = xe_ref[C:C + 8, :]

    xe_ref[8:8 + C, :] = qkv_ref[...]

    sm = sm_ref[...]
    beta_all = jax.nn.sigmoid(sm)
    g_all = -jnp.exp(lrow_ref[...]) * _softplus(sm + drow_ref[...])
    rowc = lax.broadcasted_iota(jnp.int32, (C, LANES), 0)
    gc = g_all
    s = 1
    while s < C:
        gc = gc + jnp.where(rowc >= s, pltpu.roll(gc, s, 0), 0.0)
        s *= 2
    glast = gc[C - 1:C, :]
    egc_all = jnp.exp(gc)
    ekd_all = jnp.exp(glast - gc)
    egl_all = jnp.exp(glast)
    gc_t = gc.T

    ii = lax.broadcasted_iota(jnp.int32, (C, C), 0)
    jj = lax.broadcasted_iota(jnp.int32, (C, C), 1)
    eye = jnp.where(ii == jj, 1.0, 0.0)
    gw = gw_ref[...]

    heads = range(GDN_HEADS)

    def col(p, h):
        return slice(p * hw + h * GDN_DK, p * hw + (h + 1) * GDN_DK)

    def conv(p, h):
        w = wc_ref[:, col(p, h)]
        y = xe_ref[8:8 + C, col(p, h)] * w[CONV_W - 1:CONV_W]
        for s in range(1, CONV_W):
            y = y + xe_ref[8 - s:8 - s + C, col(p, h)] * w[CONV_W - 1 - s:CONV_W - s]
        return _silu(y)

    def lane(a, l):
        return a[:, l:l + 1]

    ks = [_l2norm(conv(1, h)) for h in heads]
    qs = [_l2norm(conv(0, h)) * (GDN_DK ** -0.5) for h in heads]
    kbs = [k * lane(beta_all, SM_BA + h) for h, k in zip(heads, ks)]
    kqs = [_dot_nt(_bf(jnp.concatenate([kb, q], axis=0)), _bf(k)) for kb, q, k in zip(kbs, qs, ks)]
    decays = [jnp.where(ii >= jj,
                        jnp.exp(jnp.minimum(lane(gc, SM_AA + h) - gc_t[SM_AA + h:SM_AA + h + 1, :], 0.0)), 0.0)
              for h in heads]
    As = [jnp.where(ii > jj, kq[:C] * d, 0.0) for kq, d in zip(kqs, decays)]
    qks = [_bf(kq[C:] * d) for kq, d in zip(kqs, decays)]
    Ps = _unit_lower_inverses(As, ii, jj, eye)
    vs = [conv(2, h) for h in heads]
    rhs = [_bf(jnp.concatenate([v * lane(beta_all, SM_BA + h), kb * lane(egc_all, SM_AA + h)], axis=1))
           for h, v, kb in zip(heads, vs, kbs)]
    uws = [_dot(_bf(P), r) for P, r in zip(Ps, rhs)]
    Ss = [S_ref[h] for h in heads]
    wqs = [_dot(_bf(jnp.concatenate([uw[:, GDN_DV:], q * lane(egc_all, SM_AA + h)], axis=0)), _bf(S))
           for h, uw, q, S in zip(heads, uws, qs, Ss)]
    vnbs = [_bf(uw[:, :GDN_DV] - wq[:C]) for uw, wq in zip(uws, wqs)]
    kdts = [_bf((k * lane(ekd_all, SM_AA + h)).T) for h, k in zip(heads, ks)]
    for h in heads:
        S_ref[h] = Ss[h] * lane(egl_all, SM_AA + h) + _dot(kdts[h], vnbs[h])
    os_ = [wq[C:] + _dot(qk, vnb) for wq, qk, vnb in zip(wqs, qks, vnbs)]
    for h in heads:
        o = os_[h]
        o = o * lax.rsqrt(jnp.mean(o * o, axis=-1, keepdims=True) + EPS) * gw
        z = z_ref[:, col(0, h)]
        o_ref[:, col(0, h)] = _bf(o * _silu(z))

    @pl.when(i == pl.num_programs(1) - 1)
    def _():
        sout_ref[0] = S_ref[...]
        tout_ref[0] = xe_ref[C:C + 8, :]


def _gdn_call(proj, conv_prev8, s0, wconv8, lrow, drow, gw, *, nbatch, t):
    nt = t // CHUNK
    hh = GDN_HEADS
    const = lambda b, i: (0, 0)
    state_spec = pl.BlockSpec((1, hh, GDN_DK, GDN_DV), lambda b, i: (b, 0, 0, 0))
    tail_spec = pl.BlockSpec((1, 8, W_QKV), lambda b, i: (b, 0, 0))
    return pl.pallas_call(
        _gdn_kernel,
        grid=(nbatch, nt),
        in_specs=[pl.BlockSpec((CHUNK, W_QKV), lambda b, i: (b * nt + i, OFF_QKV // W_QKV)),
                  pl.BlockSpec((CHUNK, D_MODEL), lambda b, i: (b * nt + i, OFF_ZA // D_MODEL)),
                  pl.BlockSpec((CHUNK, LANES), lambda b, i: (b * nt + i, OFF_SM // LANES)),
                  tail_spec,
                  pl.BlockSpec((8, W_QKV), const),
                  pl.BlockSpec((1, LANES), const),
                  pl.BlockSpec((1, LANES), const),
                  pl.BlockSpec((1, GDN_DV), const),
                  state_spec],
        out_specs=[pl.BlockSpec((CHUNK, hh * GDN_DV), lambda b, i: (b * nt + i, 0)),
                   state_spec, tail_spec],
        out_shape=[jax.ShapeDtypeStruct((nbatch * t, hh * GDN_DV), jnp.bfloat16),
                   jax.ShapeDtypeStruct((nbatch, hh, GDN_DK, GDN_DV), jnp.float32),
                   jax.ShapeDtypeStruct((nbatch, 8, W_QKV), jnp.float32)],
        scratch_shapes=[pltpu.VMEM((hh, GDN_DK, GDN_DV), jnp.float32),
                        pltpu.VMEM((8 + CHUNK, W_QKV), jnp.float32)],
        compiler_params=_cparams(("arbitrary", "arbitrary")),
        name="gdn",
    )(proj, proj, proj, conv_prev8, wconv8, lrow, drow, gw, s0)


_INT_MIN = -2147483648
_KEY_NEG_INF = -2139095041
ATT_GROUP = ATT_HEADS // KV_HEADS


def _visible_chunks(q0, rows, nreal, kc, nkc):
    lim_max = jnp.minimum(((q0 + rows - 1) // CHUNK + 1) * CHUNK, nreal)
    return jnp.minimum((lim_max + kc - 1) // kc, nkc)


def _sel_kernel(*refs, tq, rg, t_cur, past, lp, kc, topk):
    if past:
        qi_ref, smq_ref, smk_ref, kip_ref, m_ref, kibf, key_ref, qis_ref = refs
    else:
        qi_ref, smq_ref, smk_ref, m_ref, kibf, key_ref, qis_ref = refs
    i = pl.program_id(1)
    nreal = past + t_cur
    nkc = lp // kc
    nl = kc // LANES
    groups = range(rg)

    @pl.when(i == 0)
    def _():
        if past:
            kibf[0:past, :] = _bf(kip_ref[0])
        kibf[past:nreal, :] = _bf(smk_ref[:, SM_KI:SM_KI + IDX_DIM])
        if lp > nreal:
            kibf[nreal:lp, :] = jnp.zeros((lp - nreal, IDX_DIM), jnp.bfloat16)

    q0 = past + i * (rg * tq)
    nvis = _visible_chunks(q0, rg * tq, nreal, kc, nkc)
    tlane = lax.broadcasted_iota(jnp.int32, (1, tq), 1)
    lims = [jnp.minimum(((q0 + g * tq + tlane) // CHUNK + 1) * CHUNK, nreal) for g in groups]

    def rows_of(g):
        return slice(g * tq, (g + 1) * tq)

    def keys_of(c, j=None):
        if j is None:
            return pl.ds(pl.multiple_of(c * kc, kc), kc)
        return pl.ds(pl.multiple_of(c * kc + j * LANES, LANES), LANES)

    def key_pos(c, n, j=0):
        return c * kc + j * LANES + lax.broadcasted_iota(jnp.int32, (n, tq), 0)

    for g in groups:
        w_t = (smq_ref[rows_of(g), :] * ((IDX_HEADS ** -0.5) * (IDX_DIM ** -0.5))).T
        for hh in range(IDX_HEADS):
            qis_ref[hh * tq:(hh + 1) * tq, :] = _bf(qi_ref[rows_of(g), hh * IDX_DIM:(hh + 1) * IDX_DIM])

        def score_chunk(c, carry, g=g, w_t=w_t):
            d = _dot_nt(kibf[keys_of(c), :], qis_ref[...])
            acc = jnp.zeros((kc, tq), jnp.float32)
            for hh in range(IDX_HEADS):
                acc = acc + w_t[SM_WI + hh:SM_WI + hh + 1, :] * jnp.maximum(d[:, hh * tq:(hh + 1) * tq], 0.0)
            acc = jnp.where(key_pos(c, kc) < lims[g], acc, -jnp.inf)
            bits = pltpu.bitcast(acc, jnp.int32)
            key_ref[keys_of(c), rows_of(g)] = jnp.where(bits < 0, bits ^ 0x7FFFFFFF, bits)
            return carry

        lax.fori_loop(0, nvis, score_chunk, 0)

    def count_ge(cands):
        def body(c, accs):
            accs = list(accs)
            for g in groups:
                hit = jnp.where(key_ref[keys_of(c), rows_of(g)] >= cands[g], 1.0, 0.0)
                accs[g] = accs[g] + jnp.sum(hit.reshape(kc // 8, 8, tq), axis=0)
            return tuple(accs)

        accs = lax.fori_loop(0, nvis, body, tuple(jnp.zeros((8, tq), jnp.float32) for _ in groups))
        return [jnp.sum(a, axis=0, keepdims=True) for a in accs]

    def bit_step(it, taus_u):
        cands_u = [t | lax.shift_left(jnp.int32(1), 31 - it) for t in taus_u]
        cnts = count_ge([c ^ _INT_MIN for c in cands_u])
        return tuple(jnp.where(n >= float(topk), c, t) for n, c, t in zip(cnts, cands_u, taus_u))

    taus_u = lax.fori_loop(0, 32, bit_step, tuple(jnp.zeros((1, tq), jnp.int32) for _ in groups))
    taus = [t ^ _INT_MIN for t in taus_u]
    cnts_ge = count_ge(taus)
    cnts_gt = count_ge([t + 1 for t in taus])
    needs = [float(topk) - n for n in cnts_gt]
    any_excess = jnp.int32(0)
    for g in groups:
        excess = (cnts_ge[g] > float(topk)) & (taus[g] > _KEY_NEG_INF)
        any_excess = jnp.maximum(any_excess, jnp.max(jnp.where(excess, 1, 0)))

    ea = lax.broadcasted_iota(jnp.int32, (tq, tq), 0)
    eb = lax.broadcasted_iota(jnp.int32, (tq, tq), 1)
    eye = jnp.where(ea == eb, 1.0, 0.0).astype(jnp.bfloat16)

    def store_mask(g, c, j, sel_t):
        sel = _dot_nt(eye, jnp.where(sel_t, 1.0, 0.0).astype(jnp.bfloat16))
        m_ref[rows_of(g), keys_of(c, j)] = _bf(jnp.where(sel > 0.5, 0.0, -jnp.inf))

    @pl.when(any_excess == 0)
    def _():
        def body(c, carry):
            for g in groups:
                sel_t = (key_ref[keys_of(c), rows_of(g)] >= taus[g]) & (key_pos(c, kc) < lims[g])
                store_mask(g, c, None, sel_t)
            return carry

        lax.fori_loop(0, nvis, body, 0)

    @pl.when(any_excess != 0)
    def _():
        la = lax.broadcasted_iota(jnp.int32, (LANES, LANES), 0)
        lb = lax.broadcasted_iota(jnp.int32, (LANES, LANES), 1)
        lower = jnp.where(la >= lb, 1.0, 0.0).astype(jnp.bfloat16)

        def body(c, carries):
            carries = list(carries)
            for j in range(nl):
                for g in groups:
                    key = key_ref[keys_of(c, j), rows_of(g)]
                    eq = key == taus[g]
                    pref = _dot(lower, jnp.where(eq, 1.0, 0.0).astype(jnp.bfloat16)) + carries[g]
                    sel_t = ((key > taus[g]) | (eq & (pref <= needs[g]))) & (key_pos(c, LANES, j) < lims[g])
                    store_mask(g, c, j, sel_t)
                    carries[g] = pref[LANES - 1:LANES, :]
            return tuple(carries)

        lax.fori_loop(0, nvis, body, tuple(jnp.zeros((1, tq), jnp.float32) for _ in groups))

    def fill(c, carry):
        m_ref[:, keys_of(c)] = jnp.full((rg * tq, kc), -jnp.inf, jnp.bfloat16)
        return carry

    lax.fori_loop(nvis, nkc, fill, 0)


def _sel_call(proj, *, nbatch, t, past_ki=None):
    past = 0 if past_ki is None else past_ki.shape[1]
    tq, kc, lp, topk = _attn_geometry(t, past)
    rg = min(4, t // tq)
    rows = rg * tq
    nr = t // rows
    qiw = IDX_HEADS * IDX_DIM
    in_specs = [pl.BlockSpec((rows, qiw), lambda b, i: (b * nr + i, OFF_QI // qiw)),
                pl.BlockSpec((rows, LANES), lambda b, i: (b * nr + i, OFF_SM // LANES)),
                pl.BlockSpec((t, LANES), lambda b, i: (b, OFF_SM // LANES))]
    args = [proj, proj, proj]
    if past:
        in_specs.append(pl.BlockSpec((1, past, IDX_DIM), lambda b, i: (b, 0, 0)))
        args.append(past_ki)
    return pl.pallas_call(
        functools.partial(_sel_kernel, tq=tq, rg=rg, t_cur=t, past=past, lp=lp, kc=kc, topk=topk),
        grid=(nbatch, nr),
        in_specs=in_specs,
        out_specs=pl.BlockSpec((rows, lp), lambda b, i: (b * nr + i, 0)),
        out_shape=jax.ShapeDtypeStruct((nbatch * t, lp), jnp.bfloat16),
        scratch_shapes=[pltpu.VMEM((lp, IDX_DIM), jnp.bfloat16),
                        pltpu.VMEM((lp, rows), jnp.int32),
                        pltpu.VMEM((IDX_HEADS * tq, IDX_DIM), jnp.bfloat16)],
        compiler_params=_cparams(("arbitrary", "arbitrary")),
        name="sel_past" if past else "sel",
    )(*args)


def _attn_geometry(t, past):
    tq = min(128, t)
    nreal = past + t
    kc = 512 if nreal % 512 == 0 else 384
    if nreal < kc:
        kc = LANES * (-(-nreal // LANES))
    lp = kc * (-(-nreal // kc))
    assert past % LANES == 0 and past + LANES * (-(-t // LANES)) <= lp
    return tq, kc, lp, min(TOPK_MAX, nreal // 4)


def _attn_kernel(*refs, tq, t_cur, past, lp, kc):
    if past:
        (qb_ref, zb_ref, madd_ref, k_ref, v_ref, kp_ref, vp_ref, na_ref, nb_ref,
         o_ref, kbf, vbf, lg_ref, qs_ref, acc_ref, den_ref, mb_ref) = refs
    else:
        (qb_ref, zb_ref, madd_ref, k_ref, v_ref, na_ref, nb_ref,
         o_ref, kbf, vbf, lg_ref, qs_ref, acc_ref, den_ref, mb_ref) = refs
    i = pl.program_id(1)
    nreal = past + t_cur
    nkc = lp // kc
    nl = kc // LANES
    G = ATT_GROUP
    gt = G * tq

    @pl.when(i == 0)
    def _():
        if past:
            kbf[0:past, :] = _bf(kp_ref[0])
            vbf[0:past, :] = _bf(vp_ref[0])
        kbf[past:nreal, :] = _bf(k_ref[...])
        vbf[past:nreal, :] = _bf(v_ref[...])
        if lp > nreal:
            kbf[nreal:lp, :] = jnp.zeros((lp - nreal, KV_HEADS * HEAD_DIM), jnp.bfloat16)
            vbf[nreal:lp, :] = jnp.zeros((lp - nreal, KV_HEADS * HEAD_DIM), jnp.bfloat16)

    q0 = past + i * tq
    nvis = _visible_chunks(q0, tq, nreal, kc, nkc)

    def lanes_of(c, j=0):
        return pl.ds(pl.multiple_of(c * kc + j * LANES, LANES), LANES)

    def chunk_of(c):
        return pl.ds(pl.multiple_of(c * kc, kc), kc)

    scale = HEAD_DIM ** -0.5
    for hd in range(ATT_HEADS):
        qs_ref[hd * tq:(hd + 1) * tq, :] = _bf(qb_ref[:, hd * HEAD_DIM:(hd + 1) * HEAD_DIM] * scale)
    off_b = pl.ds(pl.multiple_of(q0, LANES), LANES)
    off_a = pl.ds(pl.multiple_of(jnp.maximum(q0 - LANES, 0), LANES), LANES)

    for n in range(KV_HEADS):
        ncol = slice(n * HEAD_DIM, (n + 1) * HEAD_DIM)
        grows = slice(n * gt, (n + 1) * gt)

        def logit_chunk(c, carry):
            sc = _dot_nt(qs_ref[grows, :], kbf[chunk_of(c), ncol])
            ma = madd_ref[:, chunk_of(c)].astype(jnp.float32)
            for g in range(G):
                lg_ref[g * tq:(g + 1) * tq, chunk_of(c)] = sc[g * tq:(g + 1) * tq] + ma
            return carry

        lax.fori_loop(0, nvis, logit_chunk, 0)

        for g in range(G):
            r = slice(g * tq, (g + 1) * tq)
            lg_ref[r, off_b] = lg_ref[r, off_b] + nb_ref[n * G + g]

        @pl.when(q0 >= LANES)
        def _():
            for g in range(G):
                r = slice(g * tq, (g + 1) * tq)
                lg_ref[r, off_a] = lg_ref[r, off_a] + na_ref[n * G + g]

        for g in range(G):
            r = slice(g * tq, (g + 1) * tq)

            def max_chunk(c, mt):
                for j in range(nl):
                    mt = jnp.maximum(mt, lg_ref[r, lanes_of(c, j)])
                return mt

            mt = lax.fori_loop(0, nvis, max_chunk, jnp.full((tq, LANES), -jnp.inf, jnp.float32))
            mb_ref[r, :] = jnp.broadcast_to(jnp.max(mt, axis=1, keepdims=True), (tq, LANES))

        acc_ref[...] = jnp.zeros((gt, HEAD_DIM), jnp.float32)
        den_ref[...] = jnp.zeros((gt, LANES), jnp.float32)

        def pv_chunk(c, carry):
            p = jnp.exp(lg_ref[:, chunk_of(c)] - jnp.concatenate([mb_ref[...]] * nl, axis=1))
            den = den_ref[...]
            for j in range(nl):
                den = den + p[:, j * LANES:(j + 1) * LANES]
            den_ref[...] = den
            acc_ref[...] = acc_ref[...] + _dot(_bf(p), vbf[chunk_of(c), ncol])
            return carry

        lax.fori_loop(0, nvis, pv_chunk, 0)

        for g in range(G):
            r = slice(g * tq, (g + 1) * tq)
            hcol = slice((n * G + g) * HEAD_DIM, (n * G + g + 1) * HEAD_DIM)
            den = jnp.sum(den_ref[r, :], axis=1, keepdims=True)
            z = zb_ref[:, hcol]
            o_ref[:, hcol] = _bf((acc_ref[r, :] / den) * _silu(z))


def _attn_call(proj, madd, near_a, near_b, *, nbatch, t, past_kv=None):
    past = 0 if past_kv is None else past_kv[0].shape[1]
    tq, kc, lp, _ = _attn_geometry(t, past)
    nq = t // tq
    kvw = KV_HEADS * HEAD_DIM
    gt = ATT_GROUP * tq

    in_specs = [pl.BlockSpec((tq, D_MODEL), lambda b, i: (b * nq + i, OFF_QB // D_MODEL)),
                pl.BlockSpec((tq, D_MODEL), lambda b, i: (b * nq + i, OFF_ZB // D_MODEL)),
                pl.BlockSpec((tq, lp), lambda b, i: (b * nq + i, 0)),
                pl.BlockSpec((t, kvw), lambda b, i: (b, OFF_KB // kvw)),
                pl.BlockSpec((t, kvw), lambda b, i: (b, OFF_VB // kvw))]
    args = [proj, proj, madd, proj, proj]
    if past:
        in_specs += [pl.BlockSpec((1, past, kvw), lambda b, i: (b, 0, 0)),
                     pl.BlockSpec((1, past, kvw), lambda b, i: (b, 0, 0))]
        args += list(past_kv)
    in_specs += [pl.BlockSpec((ATT_HEADS, tq, LANES), lambda b, i: (0, 0, 0)),
                 pl.BlockSpec((ATT_HEADS, tq, LANES), lambda b, i: (0, 0, 0))]
    args += [near_a, near_b]

    return pl.pallas_call(
        functools.partial(_attn_kernel, tq=tq, t_cur=t, past=past, lp=lp, kc=kc),
        grid=(nbatch, nq),
        in_specs=in_specs,
        out_specs=pl.BlockSpec((tq, D_MODEL), lambda b, i: (b * nq + i, 0)),
        out_shape=jax.ShapeDtypeStruct((nbatch * t, D_MODEL), jnp.bfloat16),
        scratch_shapes=[pltpu.VMEM((lp, kvw), jnp.bfloat16),
                        pltpu.VMEM((lp, kvw), jnp.bfloat16),
                        pltpu.VMEM((gt, lp), jnp.float32),
                        pltpu.VMEM((ATT_HEADS * tq, HEAD_DIM), jnp.bfloat16),
                        pltpu.VMEM((gt, HEAD_DIM), jnp.float32),
                        pltpu.VMEM((gt, LANES), jnp.float32),
                        pltpu.VMEM((gt, LANES), jnp.float32)],
        compiler_params=_cparams(("arbitrary", "arbitrary")),
        name="attn_past" if past else "attn",
    )(*args)


def _merge_kernel(oa_ref, ob_ref, wa_ref, wb_ref, ga_ref, gb_ref, o_ref):
    ya = _dot(oa_ref[...], wa_ref[...])
    yb = _dot(ob_ref[...], wb_ref[...])
    o_ref[...] = _bf(jax.nn.sigmoid(ga_ref[...]) * ya + jax.nn.sigmoid(gb_ref[...]) * yb)


def _merge_call(oa, ob, wa_bf, wb_bf, proj):
    m, d = oa.shape
    tm = min(1024, m)
    tn = 512
    return pl.pallas_call(
        _merge_kernel,
        grid=(m // tm, d // tn),
        in_specs=[pl.BlockSpec((tm, d), lambda i, j: (i, 0)),
                  pl.BlockSpec((tm, d), lambda i, j: (i, 0)),
                  pl.BlockSpec((d, tn), lambda i, j: (0, j)),
                  pl.BlockSpec((d, tn), lambda i, j: (0, j)),
                  pl.BlockSpec((tm, tn), lambda i, j: (i, OFF_GLA // tn + j)),
                  pl.BlockSpec((tm, tn), lambda i, j: (i, OFF_GLB // tn + j))],
        out_specs=pl.BlockSpec((tm, tn), lambda i, j: (i, j)),
        out_shape=jax.ShapeDtypeStruct((m, d), jnp.bfloat16),
        compiler_params=_cparams(("arbitrary", "arbitrary")),
        name="merge",
    )(oa, ob, wa_bf, wb_bf, proj, proj)


def _outproj_kernel(mg_ref, w_ref, x_ref, gate_ref, o_ref, *, tm):
    y = _dot(mg_ref[...], w_ref[...])
    for g in range(tm // GROUP):
        r = slice(g * GROUP, (g + 1) * GROUP)
        o_ref[r, :] = x_ref[r, :] + gate_ref[g:g + 1, :] * y[r]


def _outproj_call(merged, wo_bf, x, gate_g):
    m, d = x.shape
    tm = min(1024, m)
    tn = 512
    gpt = tm // GROUP
    return pl.pallas_call(
        functools.partial(_outproj_kernel, tm=tm),
        grid=(m // tm, d // tn),
        in_specs=[pl.BlockSpec((tm, d), lambda i, j: (i, 0)),
                  pl.BlockSpec((d, tn), lambda i, j: (0, j)),
                  pl.BlockSpec((tm, tn), lambda i, j: (i, j)),
                  pl.BlockSpec((gpt, tn), lambda i, j: (i, j))],
        out_specs=pl.BlockSpec((tm, tn), lambda i, j: (i, j)),
        out_shape=jax.ShapeDtypeStruct((m, d), jnp.float32),
        compiler_params=_cparams(("arbitrary", "arbitrary")),
        name="outproj",
    )(merged, wo_bf, x, gate_g)


def _rmsnorm_kernel(x_ref, w_ref, o_ref):
    x = x_ref[...]
    o_ref[...] = x * lax.rsqrt(jnp.mean(x * x, axis=-1, keepdims=True) + EPS) * w_ref[...]


def _rmsnorm_call(x, w):
    m, d = x.shape
    tm = min(512, m)
    return pl.pallas_call(
        _rmsnorm_kernel,
        grid=(m // tm,),
        in_specs=[pl.BlockSpec((tm, d), lambda i: (i, 0)),
                  pl.BlockSpec((1, d), lambda i: (0, 0))],
        out_specs=pl.BlockSpec((tm, d), lambda i: (i, 0)),
        out_shape=jax.ShapeDtypeStruct((m, d), jnp.float32),
        compiler_params=_cparams(("arbitrary",)),
        name="final_norm",
    )(x, w.reshape(1, d))


def _relayout_w_in(w):
    offs = [0]
    for s in _IN_SIZES:
        offs.append(offs[-1] + s)
    (qkv, za, ba, aa, qb, kb, vb, zb, qi, ki, wi, gla, glb) = [w[:, offs[n]:offs[n + 1]] for n in range(13)]
    d = w.shape[0]
    pad_sm = jnp.zeros((d, LANES - (IDX_DIM + 3 * GDN_HEADS)), w.dtype)
    cols = [qkv, za, qb, zb, gla, glb, qi, kb, vb, ki, ba, aa, wi, pad_sm]
    out = jnp.concatenate(cols, axis=1)
    pad = jnp.zeros((d, N_PROJ - out.shape[1]), w.dtype)
    return jnp.concatenate([out, pad], axis=1).astype(jnp.bfloat16)


def _rel_bucket(rel):
    nb = REL_BUCKETS // 2
    max_exact = nb // 2
    n = jnp.abs(rel)
    nf = jnp.maximum(n, 1).astype(jnp.float32)
    large = max_exact + (jnp.log(nf / max_exact) / math.log(REL_MAX_DIST / max_exact)
                         * (nb - max_exact)).astype(jnp.int32)
    large = jnp.minimum(large, nb - 1)
    return jnp.where(rel > 0, nb, 0) + jnp.where(n < max_exact, n, large)


def _bias_tables(rel_bias):
    tq = LANES
    trow = jnp.arange(tq)[:, None]
    col = jnp.arange(2 * LANES)[None, :]
    rel = (col - LANES) - trow
    tab = rel_bias[_rel_bucket(rel)]
    far = rel_bias[_rel_bucket(jnp.array(-REL_MAX_DIST))]
    tab = jnp.moveaxis(tab, 2, 0) - far[:, None, None]
    return tab[:, :, :LANES], tab[:, :, LANES:]


def _pad_rows8(a):
    z = jnp.zeros(a.shape[:-2] + (8 - a.shape[-2], a.shape[-1]), a.dtype)
    return jnp.concatenate([z, a], axis=-2)


def _lane_row(vals, off):
    r = jnp.zeros((1, LANES), jnp.float32)
    return r.at[0, off:off + vals.shape[0]].set(vals)


def _layer(x, mod, conv_prev, s0, past_kv, lw, tables, *, nbatch, t):
    (norm_w, w_in_bf, wconv8, lrow, drow, gw, wa_bf, wb_bf, wo_bf) = lw
    near_a, near_b = tables
    d = D_MODEL
    gpb = t // GROUP
    shift_g = jnp.repeat(mod[:, 0:d], gpb, axis=0)
    scale_g = jnp.repeat(mod[:, d:2 * d], gpb, axis=0)
    gate_g = jnp.repeat(mod[:, 2 * d:3 * d], gpb, axis=0)

    proj = _inproj_call(x, norm_w, scale_g, shift_g, w_in_bf)
    oa, s_new, tails = _gdn_call(proj, _pad_rows8(conv_prev), s0, wconv8, lrow, drow, gw, nbatch=nbatch, t=t)
    tq = min(LANES, t)
    madd = _sel_call(proj, nbatch=nbatch, t=t, past_ki=None if past_kv is None else past_kv[2])
    ob = _attn_call(proj, madd, near_a[:, :tq], near_b[:, :tq], nbatch=nbatch, t=t,
                    past_kv=None if past_kv is None else past_kv[:2])
    merged = _merge_call(oa, ob, wa_bf, wb_bf, proj)
    x_new = _outproj_call(merged, wo_bf, x, gate_g)

    kvw = KV_HEADS * HEAD_DIM
    k_new = proj[:, OFF_KB:OFF_KB + kvw].reshape(nbatch, t, KV_HEADS, HEAD_DIM)
    v_new = proj[:, OFF_VB:OFF_VB + kvw].reshape(nbatch, t, KV_HEADS, HEAD_DIM)
    ki_new = proj[:, OFF_SM + SM_KI:OFF_SM + SM_KI + IDX_DIM].reshape(nbatch, t, IDX_DIM)
    conv_new = tails[:, 8 - (CONV_W - 1):, :]
    return x_new, (k_new, v_new, ki_new, s_new, conv_new)


def kernel(x_prompt, x_sample, c_prompt, c_sample, cache_k, cache_v, cache_idx_k, state_gdn, state_conv,
           norm_w, w_ada, b_ada, w_in, w_conv, a_log, dt_bias, gdn_norm_w, w_branch_a, w_branch_b,
           w_out, rel_bias, final_norm_w):
    depth = w_in.shape[0]
    bp, tp, d = x_prompt.shape
    bs, ts, _ = x_sample.shape
    past = cache_k.shape[2]
    kvw = KV_HEADS * HEAD_DIM

    mod = _ada_call(jnp.concatenate([c_prompt, c_sample], axis=0), w_ada, b_ada)
    tables = _bias_tables(rel_bias)

    xp = x_prompt.reshape(bp * tp, d)
    xs = x_sample.reshape(bs * ts, d)
    new_p, new_s = [], []
    for l in range(depth):
        wconv8 = jnp.concatenate([w_conv[l], jnp.zeros((8 - CONV_W, w_conv.shape[2]), w_conv.dtype)], axis=0)
        lw = (norm_w[l], _relayout_w_in(w_in[l]), wconv8,
              _lane_row(a_log[l], SM_AA), _lane_row(dt_bias[l], SM_AA), gdn_norm_w[l].reshape(1, GDN_DV),
              w_branch_a[l].astype(jnp.bfloat16), w_branch_b[l].astype(jnp.bfloat16),
              w_out[l].astype(jnp.bfloat16))
        conv0 = jnp.zeros((bp, CONV_W - 1, state_conv.shape[-1]), jnp.float32)
        s0 = jnp.zeros((bp,) + state_gdn.shape[2:], jnp.float32)
        xp, sp = _layer(xp, mod[l, :bp], conv0, s0, None, lw, tables, nbatch=bp, t=tp)
        past_kv = (cache_k[l].reshape(bs, past, kvw), cache_v[l].reshape(bs, past, kvw), cache_idx_k[l])
        xs, ss = _layer(xs, mod[l, bp:], state_conv[l], state_gdn[l], past_kv, lw, tables, nbatch=bs, t=ts)
        new_p.append(sp)
        new_s.append(ss)

    y_prompt = _rmsnorm_call(xp, final_norm_w).reshape(bp, tp, d)
    y_sample = _rmsnorm_call(xs, final_norm_w).reshape(bs, ts, d)
    outs_p = [jnp.stack([s[n] for s in new_p]) for n in range(5)]
    outs_s = [jnp.stack([s[n] for s in new_s]) for n in range(5)]
    return (y_prompt, y_sample, *outs_p, *outs_s)
```

```python
import functools
import math

import jax
import jax.numpy as jnp
from jax import lax
from jax.experimental import pallas as pl
from jax.experimental.pallas import tpu as pltpu

D_MODEL = 2048
CHUNK = 64
GDN_HEADS = 16
GDN_DK = 128
GDN_DV = 128
CONV_W = 4
ATT_HEADS = 16
KV_HEADS = 2
HEAD_DIM = 128
IDX_HEADS = 16
IDX_DIM = 64
TOPK_MAX = 256
REL_BUCKETS = 32
REL_MAX_DIST = 128
EPS = 1e-6

LANES = 128
VMEM_LIMIT = 56 * 1024 * 1024

W_QKV = 3 * GDN_HEADS * GDN_DK
OFF_QKV = 0
OFF_ZA = OFF_QKV + W_QKV
OFF_QB = OFF_ZA + D_MODEL
OFF_ZB = OFF_QB + D_MODEL
OFF_GLA = OFF_ZB + D_MODEL
OFF_GLB = OFF_GLA + D_MODEL
OFF_QI = OFF_GLB + D_MODEL
OFF_KB = OFF_QI + IDX_HEADS * IDX_DIM
OFF_VB = OFF_KB + KV_HEADS * HEAD_DIM
OFF_SM = OFF_VB + KV_HEADS * HEAD_DIM
SM_KI, SM_BA, SM_AA, SM_WI = 0, 64, 80, 96
N_PROJ = 18432
GROUP = 64

_IN_SIZES = (W_QKV, D_MODEL, GDN_HEADS, GDN_HEADS, D_MODEL, KV_HEADS * HEAD_DIM, KV_HEADS * HEAD_DIM,
             D_MODEL, IDX_HEADS * IDX_DIM, IDX_DIM, IDX_HEADS, D_MODEL, D_MODEL)


def _cparams(sem):
    return pltpu.CompilerParams(dimension_semantics=sem, vmem_limit_bytes=VMEM_LIMIT)


def _bf(x):
    return x.astype(jnp.bfloat16)


def _dot(a, b):
    return jnp.dot(a, b, preferred_element_type=jnp.float32)


def _dot_nt(a, b):
    return lax.dot_general(a, b, (((1,), (1,)), ((), ())), preferred_element_type=jnp.float32)


def _ada_kernel(c_ref, w_ref, b_ref, o_ref):
    c = c_ref[...]
    a = _bf(c * jax.nn.sigmoid(c))
    o_ref[0] = _dot(a, _bf(w_ref[0])) + b_ref[0]


def _ada_call(c_all, w_ada, b_ada):
    depth, d, n = w_ada.shape
    nb = c_all.shape[0]
    tn = 1024
    return pl.pallas_call(
        _ada_kernel,
        grid=(depth, n // tn),
        in_specs=[pl.BlockSpec((nb, d), lambda l, j: (0, 0)),
                  pl.BlockSpec((1, d, tn), lambda l, j: (l, 0, j)),
                  pl.BlockSpec((1, 1, tn), lambda l, j: (l, 0, j))],
        out_specs=pl.BlockSpec((1, nb, tn), lambda l, j: (l, 0, j)),
        out_shape=jax.ShapeDtypeStruct((depth, nb, n), jnp.float32),
        compiler_params=_cparams(("arbitrary", "arbitrary")),
        name="ada_mod",
    )(c_all, w_ada, b_ada.reshape(depth, 1, n))


def _inproj_kernel(x_ref, nw_ref, sc_ref, sh_ref, w_ref, o_ref, h_ref, *, tm):
    @pl.when(pl.program_id(1) == 0)
    def _():
        nw = nw_ref[...]

        def body(g, carry):
            rows = pl.ds(pl.multiple_of(g * GROUP, GROUP), GROUP)
            x = x_ref[rows, :]
            y = x * lax.rsqrt(jnp.mean(x * x, axis=-1, keepdims=True) + EPS) * nw
            hh = y * (1.0 + sc_ref[pl.ds(g, 1), :]) + sh_ref[pl.ds(g, 1), :]
            h_ref[rows, :] = _bf(hh)
            return carry

        lax.fori_loop(0, tm // GROUP, body, 0)

    o_ref[...] = _dot(h_ref[...], w_ref[...])


def _inproj_call(x, norm_w, scale_g, shift_g, w_bf):
    m, d = x.shape
    n = w_bf.shape[1]
    tm = min(1024, m)
    tn = 1024
    gpt = tm // GROUP
    return pl.pallas_call(
        functools.partial(_inproj_kernel, tm=tm),
        grid=(m // tm, n // tn),
        in_specs=[pl.BlockSpec((tm, d), lambda i, j: (i, 0)),
                  pl.BlockSpec((1, d), lambda i, j: (0, 0)),
                  pl.BlockSpec((gpt, d), lambda i, j: (i, 0)),
                  pl.BlockSpec((gpt, d), lambda i, j: (i, 0)),
                  pl.BlockSpec((d, tn), lambda i, j: (0, j))],
        out_specs=pl.BlockSpec((tm, tn), lambda i, j: (i, j)),
        out_shape=jax.ShapeDtypeStruct((m, n), jnp.float32),
        scratch_shapes=[pltpu.VMEM((tm, d), jnp.bfloat16)],
        compiler_params=_cparams(("arbitrary", "arbitrary")),
        name="inproj",
    )(x, norm_w.reshape(1, d), scale_g, shift_g, w_bf)


def _silu(x):
    hx = 0.5 * x
    return hx * jnp.tanh(hx) + hx


def _l2norm(x):
    return x * lax.rsqrt(jnp.sum(x * x, axis=-1, keepdims=True) + EPS)


def _softplus(x):
    return jnp.maximum(x, 0.0) + jnp.log1p(jnp.exp(-jnp.abs(x)))


INV_BASE = 8


def _unit_lower_inverses(As, ii, jj, eye):
    C = As[0].shape[0]
    sh = INV_BASE.bit_length() - 1
    Ns = [jnp.where((ii >> sh) == (jj >> sh), -A, 0.0) for A in As]
    Ps = [eye + N for N in Ns]
    m = 2
    while m < INV_BASE:
        Nbs = [_bf(N) for N in Ns]
        Ns = [_dot(Nb, Nb) for Nb in Nbs]
        Ps = [P + _dot(_bf(P), _bf(N)) for P, N in zip(Ps, Ns)]
        m *= 2
    s = INV_BASE
    while s < C:
        sh = s.bit_length() - 1
        off = ((ii >> (sh + 1)) == (jj >> (sh + 1))) & (((ii >> sh) & 1) == 1) & (((jj >> sh) & 1) == 0)
        Pbs = [_bf(P) for P in Ps]
        Xs = [_dot(Pb, _bf(jnp.where(off, A, 0.0))) for Pb, A in zip(Pbs, As)]
        Ps = [P - _dot(_bf(X), Pb) for P, X, Pb in zip(Ps, Xs, Pbs)]
        s *= 2
    return Ps


def _gdn_kernel(*refs, has_state):
    if has_state:
        (qkv_ref, z_ref, sm_ref, wc_ref, lrow_ref, drow_ref, gw_ref, cp_ref, s0_ref,
         o_ref, sout_ref, tout_ref, S_ref, xe_ref) = refs
    else:
        (qkv_ref, z_ref, sm_ref, wc_ref, lrow_ref, drow_ref, gw_ref,
         o_ref, sout_ref, tout_ref, S_ref, xe_ref) = refs
    i = pl.program_id(1)
    C = CHUNK
    hw = GDN_HEADS * GDN_DK

    @pl.when(i == 0)
    def _():
        if has_state:
            S_ref[...] = s0_ref[0, 0]
            xe_ref[0:8, :] = cp_ref[0]
        else:
            S_ref[...] = jnp.zeros(S_ref.shape, jnp.float32)
            xe_ref[0:8, :] = jnp.zeros((8, xe_ref.shape[1]), jnp.float32)

    @pl.when(i > 0)
    def _():
        xe_ref[0:8, :] = xe_ref[C:C + 8, :]

    xe_ref[8:8 + C, :] = qkv_ref[...]

    sm = sm_ref[...]
    beta_all = jax.nn.sigmoid(sm)
    g_all = -jnp.exp(lrow_ref[...]) * _softplus(sm + drow_ref[...])
    rowc = lax.broadcasted_iota(jnp.int32, (C, LANES), 0)
    gc = g_all
    s = 1
    while s < C:
        gc = gc + jnp.where(rowc >= s, pltpu.roll(gc, s, 0), 0.0)
        s *= 2
    glast = gc[C - 1:C, :]
    egc_all = jnp.exp(gc)
    ekd_all = jnp.exp(glast - gc)
    egl_all = jnp.exp(glast)
    gc_t = gc.T

    ii = lax.broadcasted_iota(jnp.int32, (C, C), 0)
    jj = lax.broadcasted_iota(jnp.int32, (C, C), 1)
    eye = jnp.where(ii == jj, 1.0, 0.0)
    gw = gw_ref[...]

    heads = range(GDN_HEADS)

    def col(p, h):
        return slice(p * hw + h * GDN_DK, p * hw + (h + 1) * GDN_DK)

    def conv(p, h):
        w = wc_ref[:, col(p, h)]
        y = xe_ref[8:8 + C, col(p, h)] * w[CONV_W - 1:CONV_W]
        for s in range(1, CONV_W):
            y = y + xe_ref[8 - s:8 - s + C, col(p, h)] * w[CONV_W - 1 - s:CONV_W - s]
        return _silu(y)

    def lane(a, l):
        return a[:, l:l + 1]

    ks = [_l2norm(conv(1, h)) for h in heads]
    qs = [_l2norm(conv(0, h)) * (GDN_DK ** -0.5) for h in heads]
    kbs = [k * lane(beta_all, SM_BA + h) for h, k in zip(heads, ks)]
    kqs = [_dot_nt(_bf(jnp.concatenate([kb, q], axis=0)), _bf(k)) for kb, q, k in zip(kbs, qs, ks)]
    decays = [jnp.where(ii >= jj,
                        jnp.exp(jnp.minimum(lane(gc, SM_AA + h) - gc_t[SM_AA + h:SM_AA + h + 1, :], 0.0)), 0.0)
              for h in heads]
    As = [jnp.where(ii > jj, kq[:C] * d, 0.0) for kq, d in zip(kqs, decays)]
    qks = [_bf(kq[C:] * d) for kq, d in zip(kqs, decays)]
    Ps = _unit_lower_inverses(As, ii, jj, eye)
    vs = [conv(2, h) for h in heads]
    rhs = [_bf(jnp.concatenate([v * lane(beta_all, SM_BA + h), kb * lane(egc_all, SM_AA + h)], axis=1))
           for h, v, kb in zip(heads, vs, kbs)]
    uws = [_dot(_bf(P), r) for P, r in zip(Ps, rhs)]
    Ss = [S_ref[h] for h in heads]
    wqs = [_dot(_bf(jnp.concatenate([uw[:, GDN_DV:], q * lane(egc_all, SM_AA + h)], axis=0)), _bf(S))
           for h, uw, q, S in zip(heads, uws, qs, Ss)]
    vnbs = [_bf(uw[:, :GDN_DV] - wq[:C]) for uw, wq in zip(uws, wqs)]
    kdts = [_bf((k * lane(ekd_all, SM_AA + h)).T) for h, k in zip(heads, ks)]
    for h in heads:
        S_ref[h] = Ss[h] * lane(egl_all, SM_AA + h) + _dot(kdts[h], vnbs[h])
    os_ = [wq[C:] + _dot(qk, vnb) for wq, qk, vnb in zip(wqs, qks, vnbs)]
    for h in heads:
        o = os_[h]
        o = o * lax.rsqrt(jnp.mean(o * o, axis=-1, keepdims=True) + EPS) * gw
        z = z_ref[:, col(0, h)]
        o_ref[:, col(0, h)] = _bf(o * _silu(z))

    @pl.when(i == pl.num_programs(1) - 1)
    def _():
        sout_ref[0] = S_ref[...]
        tout_ref[0] = xe_ref[C:C + 8, :]


def _gdn_call(proj, wconv8, lrow, drow, gw, *, nbatch, t, state=None):
    nt = t // CHUNK
    hh = GDN_HEADS
    const = lambda b, i: (0, 0)
    state_spec = pl.BlockSpec((1, hh, GDN_DK, GDN_DV), lambda b, i: (b, 0, 0, 0))
    tail_spec = pl.BlockSpec((1, 8, W_QKV), lambda b, i: (b, 0, 0))
    in_specs = [pl.BlockSpec((CHUNK, W_QKV), lambda b, i: (b * nt + i, OFF_QKV // W_QKV)),
                pl.BlockSpec((CHUNK, D_MODEL), lambda b, i: (b * nt + i, OFF_ZA // D_MODEL)),
                pl.BlockSpec((CHUNK, LANES), lambda b, i: (b * nt + i, OFF_SM // LANES)),
                pl.BlockSpec((8, W_QKV), const),
                pl.BlockSpec((1, LANES), const),
                pl.BlockSpec((1, LANES), const),
                pl.BlockSpec((1, GDN_DV), const)]
    args = [proj, proj, proj, wconv8, lrow, drow, gw]
    if state is not None:
        conv_prev8, s_all, layer = state
        in_specs += [tail_spec,
                     pl.BlockSpec((1, 1, hh, GDN_DK, GDN_DV), lambda b, i: (layer, b, 0, 0, 0))]
        args += [conv_prev8, s_all]
    return pl.pallas_call(
        functools.partial(_gdn_kernel, has_state=state is not None),
        grid=(nbatch, nt),
        in_specs=in_specs,
        out_specs=[pl.BlockSpec((CHUNK, hh * GDN_DV), lambda b, i: (b * nt + i, 0)),
                   state_spec, tail_spec],
        out_shape=[jax.ShapeDtypeStruct((nbatch * t, hh * GDN_DV), jnp.bfloat16),
                   jax.ShapeDtypeStruct((nbatch, hh, GDN_DK, GDN_DV), jnp.float32),
                   jax.ShapeDtypeStruct((nbatch, 8, W_QKV), jnp.float32)],
        scratch_shapes=[pltpu.VMEM((hh, GDN_DK, GDN_DV), jnp.float32),
                        pltpu.VMEM((8 + CHUNK, W_QKV), jnp.float32)],
        compiler_params=_cparams(("arbitrary", "arbitrary")),
        name="gdn",
    )(*args)


_INT_MIN = -2147483648
_KEY_NEG_INF = -2139095041
ATT_GROUP = ATT_HEADS // KV_HEADS


def _visible_chunks(q0, rows, nreal, kc, nkc):
    lim_max = jnp.minimum(((q0 + rows - 1) // CHUNK + 1) * CHUNK, nreal)
    return jnp.minimum((lim_max + kc - 1) // kc, nkc)


def _sel_kernel(*refs, tq, rg, t_cur, past, lp, kc, topk):
    if past:
        qi_ref, smq_ref, smk_ref, kip_ref, m_ref, kibf, key_ref, qis_ref = refs
    else:
        qi_ref, smq_ref, smk_ref, m_ref, kibf, key_ref, qis_ref = refs
    i = pl.program_id(1)
    nreal = past + t_cur
    nkc = lp // kc
    nl = kc // LANES
    groups = range(rg)

    @pl.when(i == 0)
    def _():
        if past:
            kibf[0:past, :] = _bf(kip_ref[0, 0])
        kibf[past:nreal, :] = _bf(smk_ref[:, SM_KI:SM_KI + IDX_DIM])
        if lp > nreal:
            kibf[nreal:lp, :] = jnp.zeros((lp - nreal, IDX_DIM), jnp.bfloat16)

    q0 = past + i * (rg * tq)
    nvis = _visible_chunks(q0, rg * tq, nreal, kc, nkc)
    tlane = lax.broadcasted_iota(jnp.int32, (1, tq), 1)
    lims = [jnp.minimum(((q0 + g * tq + tlane) // CHUNK + 1) * CHUNK, nreal) for g in groups]

    def rows_of(g):
        return slice(g * tq, (g + 1) * tq)

    def keys_of(c, j=None):
        if j is None:
            return pl.ds(pl.multiple_of(c * kc, kc), kc)
        return pl.ds(pl.multiple_of(c * kc + j * LANES, LANES), LANES)

    def key_pos(c, n, j=0):
        return c * kc + j * LANES + lax.broadcasted_iota(jnp.int32, (n, tq), 0)

    for g in groups:
        w_t = (smq_ref[rows_of(g), :] * ((IDX_HEADS ** -0.5) * (IDX_DIM ** -0.5))).T
        for hh in range(IDX_HEADS):
            qis_ref[hh * tq:(hh + 1) * tq, :] = _bf(qi_ref[rows_of(g), hh * IDX_DIM:(hh + 1) * IDX_DIM])

        def score_chunk(c, carry, g=g, w_t=w_t):
            d = _dot_nt(kibf[keys_of(c), :], qis_ref[...])
            acc = jnp.zeros((kc, tq), jnp.float32)
            for hh in range(IDX_HEADS):
                acc = acc + w_t[SM_WI + hh:SM_WI + hh + 1, :] * jnp.maximum(d[:, hh * tq:(hh + 1) * tq], 0.0)
            acc = jnp.where(key_pos(c, kc) < lims[g], acc, -jnp.inf)
            bits = pltpu.bitcast(acc, jnp.int32)
            key_ref[keys_of(c), rows_of(g)] = jnp.where(bits < 0, bits ^ 0x7FFFFFFF, bits)
            return carry

        lax.fori_loop(0, nvis, score_chunk, 0)

    def count_ge(cands):
        def body(c, accs):
            accs = list(accs)
            for g in groups:
                hit = jnp.where(key_ref[keys_of(c), rows_of(g)] >= cands[g], 1.0, 0.0)
                accs[g] = accs[g] + jnp.sum(hit.reshape(kc // 8, 8, tq), axis=0)
            return tuple(accs)

        accs = lax.fori_loop(0, nvis, body, tuple(jnp.zeros((8, tq), jnp.float32) for _ in groups))
        return [jnp.sum(a, axis=0, keepdims=True) for a in accs]

    def bit_step(it, taus_u):
        cands_u = [t | lax.shift_left(jnp.int32(1), 31 - it) for t in taus_u]
        cnts = count_ge([c ^ _INT_MIN for c in cands_u])
        return tuple(jnp.where(n >= float(topk), c, t) for n, c, t in zip(cnts, cands_u, taus_u))

    taus_u = lax.fori_loop(0, 32, bit_step, tuple(jnp.zeros((1, tq), jnp.int32) for _ in groups))
    taus = [t ^ _INT_MIN for t in taus_u]
    cnts_ge = count_ge(taus)
    cnts_gt = count_ge([t + 1 for t in taus])
    needs = [float(topk) - n for n in cnts_gt]
    any_excess = jnp.int32(0)
    for g in groups:
        excess = (cnts_ge[g] > float(topk)) & (taus[g] > _KEY_NEG_INF)
        any_excess = jnp.maximum(any_excess, jnp.max(jnp.where(excess, 1, 0)))

    ea = lax.broadcasted_iota(jnp.int32, (tq, tq), 0)
    eb = lax.broadcasted_iota(jnp.int32, (tq, tq), 1)
    eye = jnp.where(ea == eb, 1.0, 0.0).astype(jnp.bfloat16)

    def store_mask(g, c, j, sel_t):
        sel = _dot_nt(eye, jnp.where(sel_t, 1.0, 0.0).astype(jnp.bfloat16))
        m_ref[rows_of(g), keys_of(c, j)] = _bf(jnp.where(sel > 0.5, 0.0, -jnp.inf))

    @pl.when(any_excess == 0)
    def _():
        def body(c, carry):
            for g in groups:
                sel_t = (key_ref[keys_of(c), rows_of(g)] >= taus[g]) & (key_pos(c, kc) < lims[g])
                store_mask(g, c, None, sel_t)
            return carry

        lax.fori_loop(0, nvis, body, 0)

    @pl.when(any_excess != 0)
    def _():
        la = lax.broadcasted_iota(jnp.int32, (LANES, LANES), 0)
        lb = lax.broadcasted_iota(jnp.int32, (LANES, LANES), 1)
        lower = jnp.where(la >= lb, 1.0, 0.0).astype(jnp.bfloat16)

        def body(c, carries):
            carries = list(carries)
            for j in range(nl):
                for g in groups:
                    key = key_ref[keys_of(c, j), rows_of(g)]
                    eq = key == taus[g]
                    pref = _dot(lower, jnp.where(eq, 1.0, 0.0).astype(jnp.bfloat16)) + carries[g]
                    sel_t = ((key > taus[g]) | (eq & (pref <= needs[g]))) & (key_pos(c, LANES, j) < lims[g])
                    store_mask(g, c, j, sel_t)
                    carries[g] = pref[LANES - 1:LANES, :]
            return tuple(carries)

        lax.fori_loop(0, nvis, body, tuple(jnp.zeros((1, tq), jnp.float32) for _ in groups))

    def fill(c, carry):
        m_ref[:, keys_of(c)] = jnp.full((rg * tq, kc), -jnp.inf, jnp.bfloat16)
        return carry

    lax.fori_loop(nvis, nkc, fill, 0)


def _sel_call(proj, *, nbatch, t, past_ki=None):
    past = 0 if past_ki is None else past_ki[0].shape[2]
    tq, kc, lp, topk = _attn_geometry(t, past)
    rg = min(4, t // tq)
    rows = rg * tq
    nr = t // rows
    qiw = IDX_HEADS * IDX_DIM
    in_specs = [pl.BlockSpec((rows, qiw), lambda b, i: (b * nr + i, OFF_QI // qiw)),
                pl.BlockSpec((rows, LANES), lambda b, i: (b * nr + i, OFF_SM // LANES)),
                pl.BlockSpec((t, LANES), lambda b, i: (b, OFF_SM // LANES))]
    args = [proj, proj, proj]
    if past:
        ki_all, layer = past_ki
        in_specs.append(pl.BlockSpec((1, 1, past, IDX_DIM), lambda b, i: (layer, b, 0, 0)))
        args.append(ki_all)
    return pl.pallas_call(
        functools.partial(_sel_kernel, tq=tq, rg=rg, t_cur=t, past=past, lp=lp, kc=kc, topk=topk),
        grid=(nbatch, nr),
        in_specs=in_specs,
        out_specs=pl.BlockSpec((rows, lp), lambda b, i: (b * nr + i, 0)),
        out_shape=jax.ShapeDtypeStruct((nbatch * t, lp), jnp.bfloat16),
        scratch_shapes=[pltpu.VMEM((lp, IDX_DIM), jnp.bfloat16),
                        pltpu.VMEM((lp, rows), jnp.int32),
                        pltpu.VMEM((IDX_HEADS * tq, IDX_DIM), jnp.bfloat16)],
        compiler_params=_cparams(("arbitrary", "arbitrary")),
        name="sel_past" if past else "sel",
    )(*args)


def _attn_geometry(t, past):
    tq = min(128, t)
    nreal = past + t
    kc = 512 if nreal % 512 == 0 else 384
    if nreal < kc:
        kc = LANES * (-(-nreal // LANES))
    lp = kc * (-(-nreal // kc))
    assert past % LANES == 0 and past + LANES * (-(-t // LANES)) <= lp
    return tq, kc, lp, min(TOPK_MAX, nreal // 4)


def _attn_kernel(*refs, tq, t_cur, past, lp, kc):
    if past:
        (qb_ref, zb_ref, madd_ref, k_ref, v_ref, kp_ref, vp_ref, na_ref, nb_ref,
         o_ref, kbf, vbf, lg_ref, qs_ref, acc_ref, den_ref, mb_ref) = refs
    else:
        (qb_ref, zb_ref, madd_ref, k_ref, v_ref, na_ref, nb_ref,
         o_ref, kbf, vbf, lg_ref, qs_ref, acc_ref, den_ref, mb_ref) = refs
    i = pl.program_id(1)
    nreal = past + t_cur
    G = ATT_GROUP
    gt = G * tq

    @pl.when(i == 0)
    def _():
        if past:
            kbf[0:past, :] = _bf(kp_ref[0, 0])
            vbf[0:past, :] = _bf(vp_ref[0, 0])
        kbf[past:nreal, :] = _bf(k_ref[...])
        vbf[past:nreal, :] = _bf(v_ref[...])
        if lp > nreal:
            kbf[nreal:lp, :] = jnp.zeros((lp - nreal, KV_HEADS * HEAD_DIM), jnp.bfloat16)
            vbf[nreal:lp, :] = jnp.zeros((lp - nreal, KV_HEADS * HEAD_DIM), jnp.bfloat16)

    q0 = past + i * tq

    far_end = jnp.maximum(q0 - LANES, 0)
    nfull = far_end // kc
    nleft = (far_end - nfull * kc) // LANES
    tail0 = pl.multiple_of(far_end, LANES)
    tw = 2 * LANES
    first = q0 == 0

    def keys_at(off, w):
        return pl.ds(pl.multiple_of(off, LANES), w)

    scale = HEAD_DIM ** -0.5
    for hd in range(ATT_HEADS):
        qs_ref[hd * tq:(hd + 1) * tq, :] = _bf(qb_ref[:, hd * HEAD_DIM:(hd + 1) * HEAD_DIM] * scale)

    for n in range(KV_HEADS):
        ncol = slice(n * HEAD_DIM, (n + 1) * HEAD_DIM)
        grows = slice(n * gt, (n + 1) * gt)

        def logits(off, w, bias=None):
            sc = _dot_nt(qs_ref[grows, :], kbf[keys_at(off, w), ncol])
            ma = madd_ref[:, keys_at(off, w)].astype(jnp.float32)
            for g in range(G):
                r = slice(g * tq, (g + 1) * tq)
                s = sc[r] + ma
                if bias is not None:
                    s = s + bias(n * G + g)
                lg_ref[r, keys_at(off, w)] = s
                mt = mb_ref[r, :]
                for j in range(w // LANES):
                    mt = jnp.maximum(mt, s[:, j * LANES:(j + 1) * LANES])
                mb_ref[r, :] = mt

        def tail_bias(hd):
            zero = jnp.zeros((tq, LANES), jnp.float32)
            return jnp.concatenate([jnp.where(first, nb_ref[hd], na_ref[hd]),
                                    jnp.where(first, zero, nb_ref[hd])], axis=1)

        def weighted_values(off, w):
            p = jnp.exp(lg_ref[:, keys_at(off, w)] - jnp.concatenate([mb_ref[...]] * (w // LANES), axis=1))
            den = den_ref[...]
            for j in range(w // LANES):
                den = den + p[:, j * LANES:(j + 1) * LANES]
            den_ref[...] = den
            acc_ref[...] = acc_ref[...] + _dot(_bf(p), vbf[keys_at(off, w), ncol])

        def walk(fn, tail_kwargs):
            lax.fori_loop(0, nfull, lambda c, carry: (fn(c * kc, kc), carry)[1], 0)
            lax.fori_loop(0, nleft, lambda b, carry: (fn(nfull * kc + b * LANES, LANES), carry)[1], 0)
            fn(tail0, tw, **tail_kwargs)

        mb_ref[...] = jnp.full((gt, LANES), -jnp.inf, jnp.float32)
        walk(logits, dict(bias=tail_bias))
        for g in range(G):
            r = slice(g * tq, (g + 1) * tq)
            mb_ref[r, :] = jnp.broadcast_to(jnp.max(mb_ref[r, :], axis=1, keepdims=True), (tq, LANES))

        acc_ref[...] = jnp.zeros((gt, HEAD_DIM), jnp.float32)
        den_ref[...] = jnp.zeros((gt, LANES), jnp.float32)
        walk(weighted_values, {})

        for g in range(G):
            r = slice(g * tq, (g + 1) * tq)
            hcol = slice((n * G + g) * HEAD_DIM, (n * G + g + 1) * HEAD_DIM)
            den = jnp.sum(den_ref[r, :], axis=1, keepdims=True)
            z = zb_ref[:, hcol]
            o_ref[:, hcol] = _bf((acc_ref[r, :] / den) * _silu(z))


def _attn_call(proj, madd, near_a, near_b, *, nbatch, t, past_kv=None):
    past = 0 if past_kv is None else past_kv[0].shape[2]
    tq, kc, lp, _ = _attn_geometry(t, past)
    nq = t // tq
    kvw = KV_HEADS * HEAD_DIM
    gt = ATT_GROUP * tq

    in_specs = [pl.BlockSpec((tq, D_MODEL), lambda b, i: (b * nq + i, OFF_QB // D_MODEL)),
                pl.BlockSpec((tq, D_MODEL), lambda b, i: (b * nq + i, OFF_ZB // D_MODEL)),
                pl.BlockSpec((tq, lp), lambda b, i: (b * nq + i, 0)),
                pl.BlockSpec((t, kvw), lambda b, i: (b, OFF_KB // kvw)),
                pl.BlockSpec((t, kvw), lambda b, i: (b, OFF_VB // kvw))]
    args = [proj, proj, madd, proj, proj]
    if past:
        k_all, v_all, layer = past_kv
        in_specs += [pl.BlockSpec((1, 1, past, kvw), lambda b, i: (layer, b, 0, 0)),
                     pl.BlockSpec((1, 1, past, kvw), lambda b, i: (layer, b, 0, 0))]
        args += [k_all, v_all]
    in_specs += [pl.BlockSpec((ATT_HEADS, tq, LANES), lambda b, i: (0, 0, 0)),
                 pl.BlockSpec((ATT_HEADS, tq, LANES), lambda b, i: (0, 0, 0))]
    args += [near_a, near_b]

    return pl.pallas_call(
        functools.partial(_attn_kernel, tq=tq, t_cur=t, past=past, lp=lp, kc=kc),
        grid=(nbatch, nq),
        in_specs=in_specs,
        out_specs=pl.BlockSpec((tq, D_MODEL), lambda b, i: (b * nq + i, 0)),
        out_shape=jax.ShapeDtypeStruct((nbatch * t, D_MODEL), jnp.bfloat16),
        scratch_shapes=[pltpu.VMEM((lp, kvw), jnp.bfloat16),
                        pltpu.VMEM((lp, kvw), jnp.bfloat16),
                        pltpu.VMEM((gt, lp), jnp.float32),
                        pltpu.VMEM((ATT_HEADS * tq, HEAD_DIM), jnp.bfloat16),
                        pltpu.VMEM((gt, HEAD_DIM), jnp.float32),
                        pltpu.VMEM((gt, LANES), jnp.float32),
                        pltpu.VMEM((gt, LANES), jnp.float32)],
        compiler_params=_cparams(("arbitrary", "arbitrary")),
        name="attn_past" if past else "attn",
    )(*args)


def _merge_kernel(oa_ref, ob_ref, wa_ref, wb_ref, ga_ref, gb_ref, o_ref):
    ya = _dot(oa_ref[...], wa_ref[...])
    yb = _dot(ob_ref[...], wb_ref[...])
    o_ref[...] = _bf(jax.nn.sigmoid(ga_ref[...]) * ya + jax.nn.sigmoid(gb_ref[...]) * yb)


def _merge_call(oa, ob, wa_bf, wb_bf, proj):
    m, d = oa.shape
    tm = min(1024, m)
    tn = 512
    return pl.pallas_call(
        _merge_kernel,
        grid=(m // tm, d // tn),
        in_specs=[pl.BlockSpec((tm, d), lambda i, j: (i, 0)),
                  pl.BlockSpec((tm, d), lambda i, j: (i, 0)),
                  pl.BlockSpec((d, tn), lambda i, j: (0, j)),
                  pl.BlockSpec((d, tn), lambda i, j: (0, j)),
                  pl.BlockSpec((tm, tn), lambda i, j: (i, OFF_GLA // tn + j)),
                  pl.BlockSpec((tm, tn), lambda i, j: (i, OFF_GLB // tn + j))],
        out_specs=pl.BlockSpec((tm, tn), lambda i, j: (i, j)),
        out_shape=jax.ShapeDtypeStruct((m, d), jnp.bfloat16),
        compiler_params=_cparams(("arbitrary", "arbitrary")),
        name="merge",
    )(oa, ob, wa_bf, wb_bf, proj, proj)


def _outproj_kernel(mg_ref, w_ref, x_ref, gate_ref, o_ref, *, tm):
    y = _dot(mg_ref[...], w_ref[...])
    for g in range(tm // GROUP):
        r = slice(g * GROUP, (g + 1) * GROUP)
        o_ref[r, :] = x_ref[r, :] + gate_ref[g:g + 1, :] * y[r]


def _outproj_call(merged, wo_bf, x, gate_g):
    m, d = x.shape
    tm = min(1024, m)
    tn = 512
    gpt = tm // GROUP
    return pl.pallas_call(
        functools.partial(_outproj_kernel, tm=tm),
        grid=(m // tm, d // tn),
        in_specs=[pl.BlockSpec((tm, d), lambda i, j: (i, 0)),
                  pl.BlockSpec((d, tn), lambda i, j: (0, j)),
                  pl.BlockSpec((tm, tn), lambda i, j: (i, j)),
                  pl.BlockSpec((gpt, tn), lambda i, j: (i, j))],
        out_specs=pl.BlockSpec((tm, tn), lambda i, j: (i, j)),
        out_shape=jax.ShapeDtypeStruct((m, d), jnp.float32),
        compiler_params=_cparams(("arbitrary", "arbitrary")),
        name="outproj",
    )(merged, wo_bf, x, gate_g)


def _rmsnorm_kernel(x_ref, w_ref, o_ref):
    x = x_ref[...]
    o_ref[...] = x * lax.rsqrt(jnp.mean(x * x, axis=-1, keepdims=True) + EPS) * w_ref[...]


def _rmsnorm_call(x, w):
    m, d = x.shape
    tm = min(512, m)
    return pl.pallas_call(
        _rmsnorm_kernel,
        grid=(m // tm,),
        in_specs=[pl.BlockSpec((tm, d), lambda i: (i, 0)),
                  pl.BlockSpec((1, d), lambda i: (0, 0))],
        out_specs=pl.BlockSpec((tm, d), lambda i: (i, 0)),
        out_shape=jax.ShapeDtypeStruct((m, d), jnp.float32),
        compiler_params=_cparams(("arbitrary",)),
        name="final_norm",
    )(x, w.reshape(1, d))


def _relayout_w_in(w):
    offs = [0]
    for s in _IN_SIZES:
        offs.append(offs[-1] + s)
    (qkv, za, ba, aa, qb, kb, vb, zb, qi, ki, wi, gla, glb) = [w[:, offs[n]:offs[n + 1]] for n in range(13)]
    d = w.shape[0]
    pad_sm = jnp.zeros((d, LANES - (IDX_DIM + 3 * GDN_HEADS)), w.dtype)
    cols = [qkv, za, qb, zb, gla, glb, qi, kb, vb, ki, ba, aa, wi, pad_sm]
    out = jnp.concatenate(cols, axis=1)
    pad = jnp.zeros((d, N_PROJ - out.shape[1]), w.dtype)
    return jnp.concatenate([out, pad], axis=1).astype(jnp.bfloat16)


def _rel_bucket(rel):
    nb = REL_BUCKETS // 2
    max_exact = nb // 2
    n = jnp.abs(rel)
    nf = jnp.maximum(n, 1).astype(jnp.float32)
    large = max_exact + (jnp.log(nf / max_exact) / math.log(REL_MAX_DIST / max_exact)
                         * (nb - max_exact)).astype(jnp.int32)
    large = jnp.minimum(large, nb - 1)
    return jnp.where(rel > 0, nb, 0) + jnp.where(n < max_exact, n, large)


def _bias_tables(rel_bias):
    tq = LANES
    trow = jnp.arange(tq)[:, None]
    col = jnp.arange(2 * LANES)[None, :]
    rel = (col - LANES) - trow
    tab = rel_bias[_rel_bucket(rel)]
    far = rel_bias[_rel_bucket(jnp.array(-REL_MAX_DIST))]
    tab = jnp.moveaxis(tab, 2, 0) - far[:, None, None]
    return tab[:, :, :LANES], tab[:, :, LANES:]


def _pad_rows8(a):
    z = jnp.zeros(a.shape[:-2] + (8 - a.shape[-2], a.shape[-1]), a.dtype)
    return jnp.concatenate([z, a], axis=-2)


def _lane_row(vals, off):
    r = jnp.zeros((1, LANES), jnp.float32)
    return r.at[0, off:off + vals.shape[0]].set(vals)


def _layer(x, mod, lw, tables, *, nbatch, t, caches=None):
    (norm_w, w_in_bf, wconv8, lrow, drow, gw, wa_bf, wb_bf, wo_bf) = lw
    near_a, near_b = tables
    d = D_MODEL
    gpb = t // GROUP

    def per_group(a):
        return jnp.broadcast_to(a[:, None, :], (nbatch, gpb, d)).reshape(nbatch * gpb, d)

    shift_g, scale_g, gate_g = [per_group(mod[:, n * d:(n + 1) * d]) for n in range(3)]
    if caches is None:
        state = past_ki = past_kv = None
    else:
        layer, k_all, v_all, ki_all, s_all, conv_prev = caches
        state = (_pad_rows8(conv_prev), s_all, layer)
        past_ki = (ki_all, layer)
        past_kv = (k_all, v_all, layer)

    proj = _inproj_call(x, norm_w, scale_g, shift_g, w_in_bf)
    oa, s_new, tails = _gdn_call(proj, wconv8, lrow, drow, gw, nbatch=nbatch, t=t, state=state)
    tq = min(LANES, t)
    madd = _sel_call(proj, nbatch=nbatch, t=t, past_ki=past_ki)
    ob = _attn_call(proj, madd, near_a[:, :tq], near_b[:, :tq], nbatch=nbatch, t=t, past_kv=past_kv)
    merged = _merge_call(oa, ob, wa_bf, wb_bf, proj)
    x_new = _outproj_call(merged, wo_bf, x, gate_g)

    kvw = KV_HEADS * HEAD_DIM
    k_new = proj[:, OFF_KB:OFF_KB + kvw].reshape(nbatch, t, KV_HEADS, HEAD_DIM)
    v_new = proj[:, OFF_VB:OFF_VB + kvw].reshape(nbatch, t, KV_HEADS, HEAD_DIM)
    ki_new = proj[:, OFF_SM + SM_KI:OFF_SM + SM_KI + IDX_DIM].reshape(nbatch, t, IDX_DIM)
    conv_new = tails[:, 8 - (CONV_W - 1):, :]
    return x_new, (k_new, v_new, ki_new, s_new, conv_new)


def kernel(x_prompt, x_sample, c_prompt, c_sample, cache_k, cache_v, cache_idx_k, state_gdn, state_conv,
           norm_w, w_ada, b_ada, w_in, w_conv, a_log, dt_bias, gdn_norm_w, w_branch_a, w_branch_b,
           w_out, rel_bias, final_norm_w):
    depth = w_in.shape[0]
    bp, tp, d = x_prompt.shape
    bs, ts, _ = x_sample.shape
    past = cache_k.shape[2]
    kvw = KV_HEADS * HEAD_DIM

    mod = _ada_call(jnp.concatenate([c_prompt, c_sample], axis=0), w_ada, b_ada)
    tables = _bias_tables(rel_bias)

    xp = x_prompt.reshape(bp * tp, d)
    xs = x_sample.reshape(bs * ts, d)
    new_p, new_s = [], []
    for l in range(depth):
        wconv8 = jnp.concatenate([w_conv[l], jnp.zeros((8 - CONV_W, w_conv.shape[2]), w_conv.dtype)], axis=0)
        lw = (norm_w[l], _relayout_w_in(w_in[l]), wconv8,
              _lane_row(a_log[l], SM_AA), _lane_row(dt_bias[l], SM_AA), gdn_norm_w[l].reshape(1, GDN_DV),
              w_branch_a[l].astype(jnp.bfloat16), w_branch_b[l].astype(jnp.bfloat16),
              w_out[l].astype(jnp.bfloat16))
        xp, sp = _layer(xp, mod[l, :bp], lw, tables, nbatch=bp, t=tp)
        caches = (l, cache_k.reshape(depth, bs, past, kvw), cache_v.reshape(depth, bs, past, kvw),
                  cache_idx_k, state_gdn, state_conv[l])
        xs, ss = _layer(xs, mod[l, bp:], lw, tables, nbatch=bs, t=ts, caches=caches)
        new_p.append(sp)
        new_s.append(ss)

    y_prompt = _rmsnorm_call(xp, final_norm_w).reshape(bp, tp, d)
    y_sample = _rmsnorm_call(xs, final_norm_w).reshape(bs, ts, d)
    outs_p = [jnp.stack([s[n] for s in new_p]) for n in range(5)]
    outs_s = [jnp.stack([s[n] for s in new_s]) for n in range(5)]
    return (y_prompt, y_sample, *outs_p, *outs_s)
```

```python
import functools
import math

import jax
import jax.numpy as jnp
from jax import lax
from jax.experimental import pallas as pl
from jax.experimental.pallas import tpu as pltpu

D_MODEL = 2048
CHUNK = 64
GDN_HEADS = 16
GDN_DK = 128
GDN_DV = 128
CONV_W = 4
ATT_HEADS = 16
KV_HEADS = 2
HEAD_DIM = 128
IDX_HEADS = 16
IDX_DIM = 64
TOPK_MAX = 256
REL_BUCKETS = 32
REL_MAX_DIST = 128
EPS = 1e-6

LANES = 128
VMEM_LIMIT = 56 * 1024 * 1024

W_QKV = 3 * GDN_HEADS * GDN_DK
OFF_QKV = 0
OFF_ZA = OFF_QKV + W_QKV
OFF_QB = OFF_ZA + D_MODEL
OFF_ZB = OFF_QB + D_MODEL
OFF_GLA = OFF_ZB + D_MODEL
OFF_GLB = OFF_GLA + D_MODEL
OFF_QI = OFF_GLB + D_MODEL
OFF_KB = OFF_QI + IDX_HEADS * IDX_DIM
OFF_VB = OFF_KB + KV_HEADS * HEAD_DIM
OFF_SM = OFF_VB + KV_HEADS * HEAD_DIM
SM_KI, SM_BA, SM_AA, SM_WI = 0, 64, 80, 96
N_PROJ = 18432
GROUP = 64

_IN_SIZES = (W_QKV, D_MODEL, GDN_HEADS, GDN_HEADS, D_MODEL, KV_HEADS * HEAD_DIM, KV_HEADS * HEAD_DIM,
             D_MODEL, IDX_HEADS * IDX_DIM, IDX_DIM, IDX_HEADS, D_MODEL, D_MODEL)


def _cparams(sem):
    return pltpu.CompilerParams(dimension_semantics=sem, vmem_limit_bytes=VMEM_LIMIT)


def _bf(x):
    return x.astype(jnp.bfloat16)


def _dot(a, b):
    return jnp.dot(a, b, preferred_element_type=jnp.float32)


def _dot_nt(a, b):
    return lax.dot_general(a, b, (((1,), (1,)), ((), ())), preferred_element_type=jnp.float32)


def _ada_kernel(c_ref, w_ref, b_ref, o_ref):
    c = c_ref[...]
    a = _bf(c * jax.nn.sigmoid(c))
    o_ref[0] = _dot(a, _bf(w_ref[0])) + b_ref[0]


def _ada_call(c_all, w_ada, b_ada):
    depth, d, n = w_ada.shape
    nb = c_all.shape[0]
    tn = 1024
    return pl.pallas_call(
        _ada_kernel,
        grid=(depth, n // tn),
        in_specs=[pl.BlockSpec((nb, d), lambda l, j: (0, 0)),
                  pl.BlockSpec((1, d, tn), lambda l, j: (l, 0, j)),
                  pl.BlockSpec((1, 1, tn), lambda l, j: (l, 0, j))],
        out_specs=pl.BlockSpec((1, nb, tn), lambda l, j: (l, 0, j)),
        out_shape=jax.ShapeDtypeStruct((depth, nb, n), jnp.float32),
        compiler_params=_cparams(("arbitrary", "arbitrary")),
        name="ada_mod",
    )(c_all, w_ada, b_ada.reshape(depth, 1, n))


def _inproj_kernel(x_ref, nw_ref, sc_ref, sh_ref, w_ref, o_ref, h_ref, *, tm):
    @pl.when(pl.program_id(1) == 0)
    def _():
        nw = nw_ref[...]

        def body(g, carry):
            rows = pl.ds(pl.multiple_of(g * GROUP, GROUP), GROUP)
            x = x_ref[rows, :]
            y = x * lax.rsqrt(jnp.mean(x * x, axis=-1, keepdims=True) + EPS) * nw
            hh = y * (1.0 + sc_ref[pl.ds(g, 1), :]) + sh_ref[pl.ds(g, 1), :]
            h_ref[rows, :] = _bf(hh)
            return carry

        lax.fori_loop(0, tm // GROUP, body, 0)

    o_ref[...] = _dot(h_ref[...], w_ref[...])


def _inproj_call(x, norm_w, scale_g, shift_g, w_bf):
    m, d = x.shape
    n = w_bf.shape[1]
    tm = min(1024, m)
    tn = 1024
    gpt = tm // GROUP
    return pl.pallas_call(
        functools.partial(_inproj_kernel, tm=tm),
        grid=(m // tm, n // tn),
        in_specs=[pl.BlockSpec((tm, d), lambda i, j: (i, 0)),
                  pl.BlockSpec((1, d), lambda i, j: (0, 0)),
                  pl.BlockSpec((gpt, d), lambda i, j: (i, 0)),
                  pl.BlockSpec((gpt, d), lambda i, j: (i, 0)),
                  pl.BlockSpec((d, tn), lambda i, j: (0, j))],
        out_specs=pl.BlockSpec((tm, tn), lambda i, j: (i, j)),
        out_shape=jax.ShapeDtypeStruct((m, n), jnp.float32),
        scratch_shapes=[pltpu.VMEM((tm, d), jnp.bfloat16)],
        compiler_params=_cparams(("arbitrary", "arbitrary")),
        name="inproj",
    )(x, norm_w.reshape(1, d), scale_g, shift_g, w_bf)


def _silu(x):
    hx = 0.5 * x
    return hx * jnp.tanh(hx) + hx


def _l2norm(x):
    return x * lax.rsqrt(jnp.sum(x * x, axis=-1, keepdims=True) + EPS)


def _softplus(x):
    return jnp.maximum(x, 0.0) + jnp.log1p(jnp.exp(-jnp.abs(x)))


INV_BASE = 8
HEADS_PER_PASS = GDN_HEADS


def _unit_lower_inverses(As, ii, jj, eye):
    C = As[0].shape[0]
    sh = INV_BASE.bit_length() - 1
    Ns = [jnp.where((ii >> sh) == (jj >> sh), -A, 0.0) for A in As]
    Ps = [eye + N for N in Ns]
    m = 2
    while m < INV_BASE:
        Nbs = [_bf(N) for N in Ns]
        Ns = [_dot(Nb, Nb) for Nb in Nbs]
        Ps = [P + _dot(_bf(P), _bf(N)) for P, N in zip(Ps, Ns)]
        m *= 2
    s = INV_BASE
    while s < C:
        sh = s.bit_length() - 1
        off = ((ii >> (sh + 1)) == (jj >> (sh + 1))) & (((ii >> sh) & 1) == 1) & (((jj >> sh) & 1) == 0)
        Pbs = [_bf(P) for P in Ps]
        Xs = [_dot(Pb, _bf(jnp.where(off, A, 0.0))) for Pb, A in zip(Pbs, As)]
        Ps = [P - _dot(_bf(X), Pb) for P, X, Pb in zip(Ps, Xs, Pbs)]
        s *= 2
    return Ps


def _gdn_kernel(*refs, has_state):
    if has_state:
        (qkv_ref, z_ref, sm_ref, wc_ref, lrow_ref, drow_ref, gw_ref, cp_ref, s0_ref,
         o_ref, sout_ref, tout_ref, S_ref, xe_ref) = refs
    else:
        (qkv_ref, z_ref, sm_ref, wc_ref, lrow_ref, drow_ref, gw_ref,
         o_ref, sout_ref, tout_ref, S_ref, xe_ref) = refs
    i = pl.program_id(1)
    C = CHUNK
    hw = GDN_HEADS * GDN_DK

    @pl.when(i == 0)
    def _():
        if has_state:
            S_ref[...] = s0_ref[0, 0]
            xe_ref[0:8, :] = cp_ref[0]
        else:
            S_ref[...] = jnp.zeros(S_ref.shape, jnp.float32)
            xe_ref[0:8, :] = jnp.zeros((8, xe_ref.shape[1]), jnp.float32)

    @pl.when(i > 0)
    def _():
        xe_ref[0:8, :] = xe_ref[C:C + 8, :]

    xe_ref[8:8 + C, :] = qkv_ref[...]

    sm = sm_ref[...]
    beta_all = jax.nn.sigmoid(sm)
    g_all = -jnp.exp(lrow_ref[...]) * _softplus(sm + drow_ref[...])
    rowc = lax.broadcasted_iota(jnp.int32, (C, LANES), 0)
    gc = g_all
    s = 1
    while s < C:
        gc = gc + jnp.where(rowc >= s, pltpu.roll(gc, s, 0), 0.0)
        s *= 2
    glast = gc[C - 1:C, :]
    egc_all = jnp.exp(gc)
    ekd_all = jnp.exp(glast - gc)
    egl_all = jnp.exp(glast)
    gc_t = gc.T

    ii = lax.broadcasted_iota(jnp.int32, (C, C), 0)
    jj = lax.broadcasted_iota(jnp.int32, (C, C), 1)
    eye = jnp.where(ii == jj, 1.0, 0.0)
    gw = gw_ref[...]


    def col(p, h):
        return slice(p * hw + h * GDN_DK, p * hw + (h + 1) * GDN_DK)

    def conv(p, h):
        w = wc_ref[:, col(p, h)]
        y = xe_ref[8:8 + C, col(p, h)] * w[CONV_W - 1:CONV_W]
        for s in range(1, CONV_W):
            y = y + xe_ref[8 - s:8 - s + C, col(p, h)] * w[CONV_W - 1 - s:CONV_W - s]
        return _silu(y)

    def lane(a, l):
        return a[:, l:l + 1]

    def run(heads):
        ks = [_l2norm(conv(1, h)) for h in heads]
        qs = [_l2norm(conv(0, h)) * (GDN_DK ** -0.5) for h in heads]
        kbs = [k * lane(beta_all, SM_BA + h) for h, k in zip(heads, ks)]
        kqs = [_dot_nt(_bf(jnp.concatenate([kb, q], axis=0)), _bf(k)) for kb, q, k in zip(kbs, qs, ks)]
        decays = [jnp.where(ii >= jj,
                            jnp.exp(jnp.minimum(lane(gc, SM_AA + h) - gc_t[SM_AA + h:SM_AA + h + 1, :], 0.0)),
                            0.0)
                  for h in heads]
        As = [jnp.where(ii > jj, kq[:C] * d, 0.0) for kq, d in zip(kqs, decays)]
        qks = [_bf(kq[C:] * d) for kq, d in zip(kqs, decays)]
        Ps = _unit_lower_inverses(As, ii, jj, eye)
        vs = [conv(2, h) for h in heads]
        rhs = [_bf(jnp.concatenate([v * lane(beta_all, SM_BA + h), kb * lane(egc_all, SM_AA + h)], axis=1))
               for h, v, kb in zip(heads, vs, kbs)]
        uws = [_dot(_bf(P), r) for P, r in zip(Ps, rhs)]
        Ss = [S_ref[h] for h in heads]
        wqs = [_dot(_bf(jnp.concatenate([uw[:, GDN_DV:], q * lane(egc_all, SM_AA + h)], axis=0)), _bf(S))
               for h, uw, q, S in zip(heads, uws, qs, Ss)]
        vnbs = [_bf(uw[:, :GDN_DV] - wq[:C]) for uw, wq in zip(uws, wqs)]
        kdts = [_bf((k * lane(ekd_all, SM_AA + h)).T) for h, k in zip(heads, ks)]
        for h, S, kdt, vnb in zip(heads, Ss, kdts, vnbs):
            S_ref[h] = S * lane(egl_all, SM_AA + h) + _dot(kdt, vnb)
        os_ = [wq[C:] + _dot(qk, vnb) for wq, qk, vnb in zip(wqs, qks, vnbs)]
        for h, o in zip(heads, os_):
            o = o * lax.rsqrt(jnp.mean(o * o, axis=-1, keepdims=True) + EPS) * gw
            z = z_ref[:, col(0, h)]
            o_ref[:, col(0, h)] = _bf(o * _silu(z))

    for h0 in range(0, GDN_HEADS, HEADS_PER_PASS):
        run(range(h0, h0 + HEADS_PER_PASS))

    @pl.when(i == pl.num_programs(1) - 1)
    def _():
        sout_ref[0] = S_ref[...]
        tout_ref[0] = xe_ref[C:C + 8, :]


def _gdn_call(proj, wconv8, lrow, drow, gw, *, nbatch, t, state=None):
    nt = t // CHUNK
    hh = GDN_HEADS
    const = lambda b, i: (0, 0)
    state_spec = pl.BlockSpec((1, hh, GDN_DK, GDN_DV), lambda b, i: (b, 0, 0, 0))
    tail_spec = pl.BlockSpec((1, 8, W_QKV), lambda b, i: (b, 0, 0))
    in_specs = [pl.BlockSpec((CHUNK, W_QKV), lambda b, i: (b * nt + i, OFF_QKV // W_QKV)),
                pl.BlockSpec((CHUNK, D_MODEL), lambda b, i: (b * nt + i, OFF_ZA // D_MODEL)),
                pl.BlockSpec((CHUNK, LANES), lambda b, i: (b * nt + i, OFF_SM // LANES)),
                pl.BlockSpec((8, W_QKV), const),
                pl.BlockSpec((1, LANES), const),
                pl.BlockSpec((1, LANES), const),
                pl.BlockSpec((1, GDN_DV), const)]
    args = [proj, proj, proj, wconv8, lrow, drow, gw]
    if state is not None:
        conv_prev8, s_all, layer = state
        in_specs += [tail_spec,
                     pl.BlockSpec((1, 1, hh, GDN_DK, GDN_DV), lambda b, i: (layer, b, 0, 0, 0))]
        args += [conv_prev8, s_all]
    return pl.pallas_call(
        functools.partial(_gdn_kernel, has_state=state is not None),
        grid=(nbatch, nt),
        in_specs=in_specs,
        out_specs=[pl.BlockSpec((CHUNK, hh * GDN_DV), lambda b, i: (b * nt + i, 0)),
                   state_spec, tail_spec],
        out_shape=[jax.ShapeDtypeStruct((nbatch * t, hh * GDN_DV), jnp.bfloat16),
                   jax.ShapeDtypeStruct((nbatch, hh, GDN_DK, GDN_DV), jnp.float32),
                   jax.ShapeDtypeStruct((nbatch, 8, W_QKV), jnp.float32)],
        scratch_shapes=[pltpu.VMEM((hh, GDN_DK, GDN_DV), jnp.float32),
                        pltpu.VMEM((8 + CHUNK, W_QKV), jnp.float32)],
        compiler_params=_cparams(("arbitrary", "arbitrary")),
        name="gdn",
    )(*args)


_INT_MIN = -2147483648
_KEY_NEG_INF = -2139095041
ATT_GROUP = ATT_HEADS // KV_HEADS
CNT_VREGS = 8


def _visible_chunks(q0, rows, nreal, kc, nkc):
    lim_max = jnp.minimum(((q0 + rows - 1) // CHUNK + 1) * CHUNK, nreal)
    return jnp.minimum((lim_max + kc - 1) // kc, nkc)


def _sel_kernel(*refs, tq, rg, npart, t_cur, past, lp, kc, topk):
    if past:
        qi_ref, smq_ref, smk_ref, kip_ref, m_ref, kibf, key_ref, qis_ref = refs
    else:
        qi_ref, smq_ref, smk_ref, m_ref, kibf, key_ref, qis_ref = refs
    i = pl.program_id(1)
    nreal = past + t_cur
    nkc = lp // kc
    nl = kc // LANES
    groups = range(rg)
    pw = tq // npart

    @pl.when(i == 0)
    def _():
        for p in range(npart):
            if past:
                kibf[p, 0:past, :] = _bf(kip_ref[0, p])
            kibf[p, past:nreal, :] = _bf(smk_ref[p * t_cur:(p + 1) * t_cur, SM_KI:SM_KI + IDX_DIM])
            if lp > nreal:
                kibf[p, nreal:lp, :] = jnp.zeros((lp - nreal, IDX_DIM), jnp.bfloat16)

    q0 = past + i * (rg * pw)
    nvis = _visible_chunks(q0, rg * pw, nreal, kc, nkc)
    tlane = lax.broadcasted_iota(jnp.int32, (1, tq), 1)
    if npart > 1:
        tlane = tlane % pw
    lims = [jnp.minimum(((q0 + g * tq + tlane) // CHUNK + 1) * CHUNK, nreal) for g in groups]

    def rows_of(g):
        return slice(g * tq, (g + 1) * tq)

    def keys_of(c, j=None):
        if j is None:
            return pl.ds(pl.multiple_of(c * kc, kc), kc)
        return pl.ds(pl.multiple_of(c * kc + j * LANES, LANES), LANES)

    def key_pos(c, n, j=0, width=tq):
        return c * kc + j * LANES + lax.broadcasted_iota(jnp.int32, (n, width), 0)

    for g in groups:
        for p in range(npart):
            part = slice(g * tq + p * pw, g * tq + (p + 1) * pw)
            w_t = (smq_ref[part, :] * ((IDX_HEADS ** -0.5) * (IDX_DIM ** -0.5))).T
            for hh in range(IDX_HEADS):
                qis_ref[hh * pw:(hh + 1) * pw, :] = _bf(qi_ref[part, hh * IDX_DIM:(hh + 1) * IDX_DIM])
            lim = lims[g][:, p * pw:(p + 1) * pw]

            def score_chunk(c, carry, p=p, part=part, w_t=w_t, lim=lim):
                d = _dot_nt(kibf[p, keys_of(c), :], qis_ref[0:IDX_HEADS * pw, :])
                acc = jnp.zeros((kc, pw), jnp.float32)
                for hh in range(IDX_HEADS):
                    acc = acc + (w_t[SM_WI + hh:SM_WI + hh + 1, :]
                                 * jnp.maximum(d[:, hh * pw:(hh + 1) * pw], 0.0))
                acc = jnp.where(key_pos(c, kc, width=pw) < lim, acc, -jnp.inf)
                bits = pltpu.bitcast(acc, jnp.int32)
                key_ref[keys_of(c), part] = jnp.where(bits < 0, bits ^ 0x7FFFFFFF, bits)
                return carry

            lax.fori_loop(0, nvis, score_chunk, 0)

    all_visible = past >= (nkc - 1) * kc

    def count_ge(cands):
        def count(keys, w, accs):
            accs = list(accs)
            for g in groups:
                hit = jnp.where(key_ref[keys, rows_of(g)] >= cands[g], 1.0, 0.0)
                accs[g] = accs[g] + jnp.sum(hit.reshape(w // cnt_rows, cnt_rows, tq), axis=0)
            return tuple(accs)

        cnt_rows = 8 * max(1, CNT_VREGS // rg)
        accs = tuple(jnp.zeros((cnt_rows, tq), jnp.float32) for _ in groups)
        if all_visible:
            accs = count(slice(0, lp), lp, accs)
        else:
            accs = lax.fori_loop(0, nvis, lambda c, a: count(keys_of(c), kc, a), accs)
        return [jnp.sum(a, axis=0, keepdims=True) for a in accs]

    def bit_step(it, taus_u):
        cands_u = [t | lax.shift_left(jnp.int32(1), 31 - it) for t in taus_u]
        cnts = count_ge([c ^ _INT_MIN for c in cands_u])
        return tuple(jnp.where(n >= float(topk), c, t) for n, c, t in zip(cnts, cands_u, taus_u))

    taus_u = lax.fori_loop(0, 32, bit_step, tuple(jnp.zeros((1, tq), jnp.int32) for _ in groups))
    taus = [t ^ _INT_MIN for t in taus_u]
    cnts_ge = count_ge(taus)
    cnts_gt = count_ge([t + 1 for t in taus])
    needs = [float(topk) - n for n in cnts_gt]
    any_excess = jnp.int32(0)
    for g in groups:
        excess = (cnts_ge[g] > float(topk)) & (taus[g] > _KEY_NEG_INF)
        any_excess = jnp.maximum(any_excess, jnp.max(jnp.where(excess, 1, 0)))

    ea = lax.broadcasted_iota(jnp.int32, (tq, tq), 0)
    eb = lax.broadcasted_iota(jnp.int32, (tq, tq), 1)
    eye = jnp.where(ea == eb, 1.0, 0.0).astype(jnp.bfloat16)

    def store_mask(g, c, j, sel_t):
        sel = _dot_nt(eye, jnp.where(sel_t, 1.0, 0.0).astype(jnp.bfloat16))
        m_ref[rows_of(g), keys_of(c, j)] = _bf(jnp.where(sel > 0.5, 0.0, -jnp.inf))

    @pl.when(any_excess == 0)
    def _():
        def body(c, carry):
            for g in groups:
                sel_t = (key_ref[keys_of(c), rows_of(g)] >= taus[g]) & (key_pos(c, kc) < lims[g])
                store_mask(g, c, None, sel_t)
            return carry

        lax.fori_loop(0, nvis, body, 0)

    @pl.when(any_excess != 0)
    def _():
        la = lax.broadcasted_iota(jnp.int32, (LANES, LANES), 0)
        lb = lax.broadcasted_iota(jnp.int32, (LANES, LANES), 1)
        lower = jnp.where(la >= lb, 1.0, 0.0).astype(jnp.bfloat16)

        def body(c, carries):
            carries = list(carries)
            for j in range(nl):
                for g in groups:
                    key = key_ref[keys_of(c, j), rows_of(g)]
                    eq = key == taus[g]
                    pref = _dot(lower, jnp.where(eq, 1.0, 0.0).astype(jnp.bfloat16)) + carries[g]
                    sel_t = ((key > taus[g]) | (eq & (pref <= needs[g]))) & (key_pos(c, LANES, j) < lims[g])
                    store_mask(g, c, j, sel_t)
                    carries[g] = pref[LANES - 1:LANES, :]
            return tuple(carries)

        lax.fori_loop(0, nvis, body, tuple(jnp.zeros((1, tq), jnp.float32) for _ in groups))

    def fill(c, carry):
        m_ref[:, keys_of(c)] = jnp.full((rg * tq, kc), -jnp.inf, jnp.bfloat16)
        return carry

    lax.fori_loop(nvis, nkc, fill, 0)


def _sel_call(proj, *, nbatch, t, past_ki=None):
    past = 0 if past_ki is None else past_ki[0].shape[2]
    _, kc, lp, topk = _attn_geometry(t, past)
    npart = LANES // t if (t < LANES and nbatch % (LANES // t) == 0) else 1
    tq = min(LANES, t * npart)
    rg = min(4, (t * npart) // tq)
    rows = rg * tq
    nr = (t * npart) // rows
    qiw = IDX_HEADS * IDX_DIM
    in_specs = [pl.BlockSpec((rows, qiw), lambda b, i: (b * nr + i, OFF_QI // qiw)),
                pl.BlockSpec((rows, LANES), lambda b, i: (b * nr + i, OFF_SM // LANES)),
                pl.BlockSpec((npart * t, LANES), lambda b, i: (b, OFF_SM // LANES))]
    args = [proj, proj, proj]
    if past:
        ki_all, layer = past_ki
        in_specs.append(pl.BlockSpec((1, npart, past, IDX_DIM), lambda b, i: (layer, b, 0, 0)))
        args.append(ki_all)
    return pl.pallas_call(
        functools.partial(_sel_kernel, tq=tq, rg=rg, npart=npart, t_cur=t, past=past, lp=lp, kc=kc, topk=topk),
        grid=(nbatch // npart, nr),
        in_specs=in_specs,
        out_specs=pl.BlockSpec((rows, lp), lambda b, i: (b * nr + i, 0)),
        out_shape=jax.ShapeDtypeStruct((nbatch * t, lp), jnp.bfloat16),
        scratch_shapes=[pltpu.VMEM((npart, lp, IDX_DIM), jnp.bfloat16),
                        pltpu.VMEM((lp, rows), jnp.int32),
                        pltpu.VMEM((IDX_HEADS * tq, IDX_DIM), jnp.bfloat16)],
        compiler_params=_cparams(("arbitrary", "arbitrary")),
        name="sel_past" if past else "sel",
    )(*args)


def _attn_geometry(t, past):
    tq = min(128, t)
    nreal = past + t
    kc = 512 if nreal % 512 == 0 else 384
    if nreal < kc:
        kc = LANES * (-(-nreal // LANES))
    lp = kc * (-(-nreal // kc))
    assert past % LANES == 0 and past + LANES * (-(-t // LANES)) <= lp
    return tq, kc, lp, min(TOPK_MAX, nreal // 4)


def _attn_kernel(*refs, tq, t_cur, past, lp, kc):
    if past:
        (qb_ref, zb_ref, madd_ref, k_ref, v_ref, kp_ref, vp_ref, na_ref, nb_ref,
         o_ref, kbf, vbf, lg_ref, qs_ref, acc_ref, den_ref, mb_ref) = refs
    else:
        (qb_ref, zb_ref, madd_ref, k_ref, v_ref, na_ref, nb_ref,
         o_ref, kbf, vbf, lg_ref, qs_ref, acc_ref, den_ref, mb_ref) = refs
    i = pl.program_id(1)
    nreal = past + t_cur
    G = ATT_GROUP
    gt = G * tq

    @pl.when(i == 0)
    def _():
        if past:
            kbf[0:past, :] = _bf(kp_ref[0, 0])
            vbf[0:past, :] = _bf(vp_ref[0, 0])
        kbf[past:nreal, :] = _bf(k_ref[...])
        vbf[past:nreal, :] = _bf(v_ref[...])
        if lp > nreal:
            kbf[nreal:lp, :] = jnp.zeros((lp - nreal, KV_HEADS * HEAD_DIM), jnp.bfloat16)
            vbf[nreal:lp, :] = jnp.zeros((lp - nreal, KV_HEADS * HEAD_DIM), jnp.bfloat16)

    q0 = past + i * tq

    far_end = jnp.maximum(q0 - LANES, 0)
    nfull = far_end // kc
    nleft = (far_end - nfull * kc) // LANES
    tail0 = pl.multiple_of(far_end, LANES)
    tw = 2 * LANES
    first = q0 == 0

    def keys_at(off, w):
        return pl.ds(pl.multiple_of(off, LANES), w)

    scale = HEAD_DIM ** -0.5
    for hd in range(ATT_HEADS):
        qs_ref[hd * tq:(hd + 1) * tq, :] = _bf(qb_ref[:, hd * HEAD_DIM:(hd + 1) * HEAD_DIM] * scale)

    for n in range(KV_HEADS):
        ncol = slice(n * HEAD_DIM, (n + 1) * HEAD_DIM)
        grows = slice(n * gt, (n + 1) * gt)

        def logits(off, w, bias=None):
            sc = _dot_nt(qs_ref[grows, :], kbf[keys_at(off, w), ncol])
            ma = madd_ref[:, keys_at(off, w)].astype(jnp.float32)
            for g in range(G):
                r = slice(g * tq, (g + 1) * tq)
                s = sc[r] + ma
                if bias is not None:
                    s = s + bias(n * G + g)
                lg_ref[r, keys_at(off, w)] = s
                mt = mb_ref[r, :]
                for j in range(w // LANES):
                    mt = jnp.maximum(mt, s[:, j * LANES:(j + 1) * LANES])
                mb_ref[r, :] = mt

        def tail_bias(hd):
            zero = jnp.zeros((tq, LANES), jnp.float32)
            return jnp.concatenate([jnp.where(first, nb_ref[hd], na_ref[hd]),
                                    jnp.where(first, zero, nb_ref[hd])], axis=1)

        def weighted_values(off, w):
            p = jnp.exp(lg_ref[:, keys_at(off, w)] - jnp.concatenate([mb_ref[...]] * (w // LANES), axis=1))
            den = den_ref[...]
            for j in range(w // LANES):
                den = den + p[:, j * LANES:(j + 1) * LANES]
            den_ref[...] = den
            acc_ref[...] = acc_ref[...] + _dot(_bf(p), vbf[keys_at(off, w), ncol])

        def walk(fn, tail_kwargs):
            lax.fori_loop(0, nfull, lambda c, carry: (fn(c * kc, kc), carry)[1], 0)
            lax.fori_loop(0, nleft, lambda b, carry: (fn(nfull * kc + b * LANES, LANES), carry)[1], 0)
            fn(tail0, tw, **tail_kwargs)

        mb_ref[...] = jnp.full((gt, LANES), -jnp.inf, jnp.float32)
        walk(logits, dict(bias=tail_bias))
        for g in range(G):
            r = slice(g * tq, (g + 1) * tq)
            mb_ref[r, :] = jnp.broadcast_to(jnp.max(mb_ref[r, :], axis=1, keepdims=True), (tq, LANES))

        acc_ref[...] = jnp.zeros((gt, HEAD_DIM), jnp.float32)
        den_ref[...] = jnp.zeros((gt, LANES), jnp.float32)
        walk(weighted_values, {})

        for g in range(G):
            r = slice(g * tq, (g + 1) * tq)
            hcol = slice((n * G + g) * HEAD_DIM, (n * G + g + 1) * HEAD_DIM)
            den = jnp.sum(den_ref[r, :], axis=1, keepdims=True)
            z = zb_ref[:, hcol]
            o_ref[:, hcol] = _bf((acc_ref[r, :] / den) * _silu(z))


def _attn_call(proj, madd, near_a, near_b, *, nbatch, t, past_kv=None):
    past = 0 if past_kv is None else past_kv[0].shape[2]
    tq, kc, lp, _ = _attn_geometry(t, past)
    nq = t // tq
    kvw = KV_HEADS * HEAD_DIM
    gt = ATT_GROUP * tq

    in_specs = [pl.BlockSpec((tq, D_MODEL), lambda b, i: (b * nq + i, OFF_QB // D_MODEL)),
                pl.BlockSpec((tq, D_MODEL), lambda b, i: (b * nq + i, OFF_ZB // D_MODEL)),
                pl.BlockSpec((tq, lp), lambda b, i: (b * nq + i, 0)),
                pl.BlockSpec((t, kvw), lambda b, i: (b, OFF_KB // kvw)),
                pl.BlockSpec((t, kvw), lambda b, i: (b, OFF_VB // kvw))]
    args = [proj, proj, madd, proj, proj]
    if past:
        k_all, v_all, layer = past_kv
        in_specs += [pl.BlockSpec((1, 1, past, kvw), lambda b, i: (layer, b, 0, 0)),
                     pl.BlockSpec((1, 1, past, kvw), lambda b, i: (layer, b, 0, 0))]
        args += [k_all, v_all]
    in_specs += [pl.BlockSpec((ATT_HEADS, tq, LANES), lambda b, i: (0, 0, 0)),
                 pl.BlockSpec((ATT_HEADS, tq, LANES), lambda b, i: (0, 0, 0))]
    args += [near_a, near_b]

    return pl.pallas_call(
        functools.partial(_attn_kernel, tq=tq, t_cur=t, past=past, lp=lp, kc=kc),
        grid=(nbatch, nq),
        in_specs=in_specs,
        out_specs=pl.BlockSpec((tq, D_MODEL), lambda b, i: (b * nq + i, 0)),
        out_shape=jax.ShapeDtypeStruct((nbatch * t, D_MODEL), jnp.bfloat16),
        scratch_shapes=[pltpu.VMEM((lp, kvw), jnp.bfloat16),
                        pltpu.VMEM((lp, kvw), jnp.bfloat16),
                        pltpu.VMEM((gt, lp), jnp.float32),
                        pltpu.VMEM((ATT_HEADS * tq, HEAD_DIM), jnp.bfloat16),
                        pltpu.VMEM((gt, HEAD_DIM), jnp.float32),
                        pltpu.VMEM((gt, LANES), jnp.float32),
                        pltpu.VMEM((gt, LANES), jnp.float32)],
        compiler_params=_cparams(("arbitrary", "arbitrary")),
        name="attn_past" if past else "attn",
    )(*args)


def _merge_kernel(oa_ref, ob_ref, wa_ref, wb_ref, ga_ref, gb_ref, o_ref):
    ya = _dot(oa_ref[...], wa_ref[...])
    yb = _dot(ob_ref[...], wb_ref[...])
    o_ref[...] = _bf(jax.nn.sigmoid(ga_ref[...]) * ya + jax.nn.sigmoid(gb_ref[...]) * yb)


def _merge_call(oa, ob, wa_bf, wb_bf, proj):
    m, d = oa.shape
    tm = min(1024, m)
    tn = 512
    return pl.pallas_call(
        _merge_kernel,
        grid=(m // tm, d // tn),
        in_specs=[pl.BlockSpec((tm, d), lambda i, j: (i, 0)),
                  pl.BlockSpec((tm, d), lambda i, j: (i, 0)),
                  pl.BlockSpec((d, tn), lambda i, j: (0, j)),
                  pl.BlockSpec((d, tn), lambda i, j: (0, j)),
                  pl.BlockSpec((tm, tn), lambda i, j: (i, OFF_GLA // tn + j)),
                  pl.BlockSpec((tm, tn), lambda i, j: (i, OFF_GLB // tn + j))],
        out_specs=pl.BlockSpec((tm, tn), lambda i, j: (i, j)),
        out_shape=jax.ShapeDtypeStruct((m, d), jnp.bfloat16),
        compiler_params=_cparams(("arbitrary", "arbitrary")),
        name="merge",
    )(oa, ob, wa_bf, wb_bf, proj, proj)


def _outproj_kernel(*refs, tm, final_norm):
    if final_norm:
        mg_ref, w_ref, x_ref, gate_ref, nw_ref, o_ref = refs
    else:
        mg_ref, w_ref, x_ref, gate_ref, o_ref = refs
    y = _dot(mg_ref[...], w_ref[...])
    for g in range(tm // GROUP):
        r = slice(g * GROUP, (g + 1) * GROUP)
        xn = x_ref[r, :] + gate_ref[g:g + 1, :] * y[r]
        if final_norm:
            xn = xn * lax.rsqrt(jnp.mean(xn * xn, axis=-1, keepdims=True) + EPS) * nw_ref[...]
        o_ref[r, :] = xn


def _outproj_call(merged, wo_bf, x, gate_g, final_norm_w=None):
    m, d = x.shape
    tm = min(512, m)
    gpt = tm // GROUP
    in_specs = [pl.BlockSpec((tm, d), lambda i: (i, 0)),
                pl.BlockSpec((d, d), lambda i: (0, 0)),
                pl.BlockSpec((tm, d), lambda i: (i, 0)),
                pl.BlockSpec((gpt, d), lambda i: (i, 0))]
    args = [merged, wo_bf, x, gate_g]
    if final_norm_w is not None:
        in_specs.append(pl.BlockSpec((1, d), lambda i: (0, 0)))
        args.append(final_norm_w.reshape(1, d))
    return pl.pallas_call(
        functools.partial(_outproj_kernel, tm=tm, final_norm=final_norm_w is not None),
        grid=(m // tm,),
        in_specs=in_specs,
        out_specs=pl.BlockSpec((tm, d), lambda i: (i, 0)),
        out_shape=jax.ShapeDtypeStruct((m, d), jnp.float32),
        compiler_params=_cparams(("arbitrary",)),
        name="outproj",
    )(*args)


def _relayout_w_in(w):
    offs = [0]
    for s in _IN_SIZES:
        offs.append(offs[-1] + s)
    (qkv, za, ba, aa, qb, kb, vb, zb, qi, ki, wi, gla, glb) = [w[:, offs[n]:offs[n + 1]] for n in range(13)]
    d = w.shape[0]
    pad_sm = jnp.zeros((d, LANES - (IDX_DIM + 3 * GDN_HEADS)), w.dtype)
    cols = [qkv, za, qb, zb, gla, glb, qi, kb, vb, ki, ba, aa, wi, pad_sm]
    out = jnp.concatenate(cols, axis=1)
    pad = jnp.zeros((d, N_PROJ - out.shape[1]), w.dtype)
    return jnp.concatenate([out, pad], axis=1).astype(jnp.bfloat16)


def _rel_bucket(rel):
    nb = REL_BUCKETS // 2
    max_exact = nb // 2
    n = jnp.abs(rel)
    nf = jnp.maximum(n, 1).astype(jnp.float32)
    large = max_exact + (jnp.log(nf / max_exact) / math.log(REL_MAX_DIST / max_exact)
                         * (nb - max_exact)).astype(jnp.int32)
    large = jnp.minimum(large, nb - 1)
    return jnp.where(rel > 0, nb, 0) + jnp.where(n < max_exact, n, large)


def _bias_tables(rel_bias):
    tq = LANES
    trow = jnp.arange(tq)[:, None]
    col = jnp.arange(2 * LANES)[None, :]
    rel = (col - LANES) - trow
    bucket = _rel_bucket(rel)
    tab = sum(jnp.where(bucket == b, rel_bias[b][:, None, None], 0.0) for b in range(REL_BUCKETS))
    far = rel_bias[REL_BUCKETS // 2 - 1]
    tab = tab - far[:, None, None]
    return tab[:, :, :LANES], tab[:, :, LANES:]


def _pad_rows8(a):
    z = jnp.zeros(a.shape[:-2] + (8 - a.shape[-2], a.shape[-1]), a.dtype)
    return jnp.concatenate([z, a], axis=-2)


def _lane_row(vals, off):
    r = jnp.zeros((1, LANES), jnp.float32)
    return r.at[0, off:off + vals.shape[0]].set(vals)


def _layer(x, mod, lw, tables, *, nbatch, t, caches=None, final_norm_w=None):
    (norm_w, w_in_bf, wconv8, lrow, drow, gw, wa_bf, wb_bf, wo_bf) = lw
    near_a, near_b = tables
    d = D_MODEL
    gpb = t // GROUP

    def per_group(a):
        return jnp.broadcast_to(a[:, None, :], (nbatch, gpb, d)).reshape(nbatch * gpb, d)

    shift_g, scale_g, gate_g = [per_group(mod[:, n * d:(n + 1) * d]) for n in range(3)]
    if caches is None:
        state = past_ki = past_kv = None
    else:
        layer, k_all, v_all, ki_all, s_all, conv_prev = caches
        state = (_pad_rows8(conv_prev), s_all, layer)
        past_ki = (ki_all, layer)
        past_kv = (k_all, v_all, layer)

    proj = _inproj_call(x, norm_w, scale_g, shift_g, w_in_bf)
    oa, s_new, tails = _gdn_call(proj, wconv8, lrow, drow, gw, nbatch=nbatch, t=t, state=state)
    tq = min(LANES, t)
    madd = _sel_call(proj, nbatch=nbatch, t=t, past_ki=past_ki)
    ob = _attn_call(proj, madd, near_a[:, :tq], near_b[:, :tq], nbatch=nbatch, t=t, past_kv=past_kv)
    merged = _merge_call(oa, ob, wa_bf, wb_bf, proj)
    x_new = _outproj_call(merged, wo_bf, x, gate_g, final_norm_w)

    kvw = KV_HEADS * HEAD_DIM
    k_new = proj[:, OFF_KB:OFF_KB + kvw].reshape(nbatch, t, KV_HEADS, HEAD_DIM)
    v_new = proj[:, OFF_VB:OFF_VB + kvw].reshape(nbatch, t, KV_HEADS, HEAD_DIM)
    ki_new = proj[:, OFF_SM + SM_KI:OFF_SM + SM_KI + IDX_DIM].reshape(nbatch, t, IDX_DIM)
    conv_new = tails[:, 8 - (CONV_W - 1):, :]
    return x_new, (k_new, v_new, ki_new, s_new, conv_new)


def kernel(x_prompt, x_sample, c_prompt, c_sample, cache_k, cache_v, cache_idx_k, state_gdn, state_conv,
           norm_w, w_ada, b_ada, w_in, w_conv, a_log, dt_bias, gdn_norm_w, w_branch_a, w_branch_b,
           w_out, rel_bias, final_norm_w):
    depth = w_in.shape[0]
    bp, tp, d = x_prompt.shape
    bs, ts, _ = x_sample.shape
    past = cache_k.shape[2]
    kvw = KV_HEADS * HEAD_DIM

    mod = _ada_call(jnp.concatenate([c_prompt, c_sample], axis=0), w_ada, b_ada)
    tables = _bias_tables(rel_bias)

    xp = x_prompt.reshape(bp * tp, d)
    xs = x_sample.reshape(bs * ts, d)
    new_p, new_s = [], []
    for l in range(depth):
        wconv8 = jnp.concatenate([w_conv[l], jnp.zeros((8 - CONV_W, w_conv.shape[2]), w_conv.dtype)], axis=0)
        lw = (norm_w[l], _relayout_w_in(w_in[l]), wconv8,
              _lane_row(a_log[l], SM_AA), _lane_row(dt_bias[l], SM_AA), gdn_norm_w[l].reshape(1, GDN_DV),
              w_branch_a[l].astype(jnp.bfloat16), w_branch_b[l].astype(jnp.bfloat16),
              w_out[l].astype(jnp.bfloat16))
        fnw = final_norm_w if l == depth - 1 else None
        xp, sp = _layer(xp, mod[l, :bp], lw, tables, nbatch=bp, t=tp, final_norm_w=fnw)
        caches = (l, cache_k.reshape(depth, bs, past, kvw), cache_v.reshape(depth, bs, past, kvw),
                  cache_idx_k, state_gdn, state_conv[l])
        xs, ss = _layer(xs, mod[l, bp:], lw, tables, nbatch=bs, t=ts, caches=caches, final_norm_w=fnw)
        new_p.append(sp)
        new_s.append(ss)

    y_prompt = xp.reshape(bp, tp, d)
    y_sample = xs.reshape(bs, ts, d)
    outs_p = [jnp.stack([s[n] for s in new_p]) for n in range(5)]
    outs_s = [jnp.stack([s[n] for s in new_s]) for n in range(5)]
    return (y_prompt, y_sample, *outs_p, *outs_s)
```

```python
import functools
import math

import jax
import jax.numpy as jnp
from jax import lax
from jax.experimental import pallas as pl
from jax.experimental.pallas import tpu as pltpu

D_MODEL = 2048
CHUNK = 64
GDN_HEADS = 16
GDN_DK = 128
GDN_DV = 128
CONV_W = 4
ATT_HEADS = 16
KV_HEADS = 2
HEAD_DIM = 128
IDX_HEADS = 16
IDX_DIM = 64
TOPK_MAX = 256
REL_BUCKETS = 32
REL_MAX_DIST = 128
EPS = 1e-6

LANES = 128
VMEM_LIMIT = 56 * 1024 * 1024

W_QKV = 3 * GDN_HEADS * GDN_DK
OFF_QKV = 0
OFF_ZA = OFF_QKV + W_QKV
OFF_QB = OFF_ZA + D_MODEL
OFF_ZB = OFF_QB + D_MODEL
OFF_GLA = OFF_ZB + D_MODEL
OFF_GLB = OFF_GLA + D_MODEL
OFF_QI = OFF_GLB + D_MODEL
OFF_KB = OFF_QI + IDX_HEADS * IDX_DIM
OFF_VB = OFF_KB + KV_HEADS * HEAD_DIM
OFF_SM = OFF_VB + KV_HEADS * HEAD_DIM
SM_KI, SM_BA, SM_AA, SM_WI = 0, 64, 80, 96
N_PROJ = 18432
GROUP = 64

_IN_SIZES = (W_QKV, D_MODEL, GDN_HEADS, GDN_HEADS, D_MODEL, KV_HEADS * HEAD_DIM, KV_HEADS * HEAD_DIM,
             D_MODEL, IDX_HEADS * IDX_DIM, IDX_DIM, IDX_HEADS, D_MODEL, D_MODEL)


def _cparams(sem):
    return pltpu.CompilerParams(dimension_semantics=sem, vmem_limit_bytes=VMEM_LIMIT)


def _bf(x):
    return x.astype(jnp.bfloat16)


def _dot(a, b):
    return jnp.dot(a, b, preferred_element_type=jnp.float32)


def _dot_nt(a, b):
    return lax.dot_general(a, b, (((1,), (1,)), ((), ())), preferred_element_type=jnp.float32)


def _ada_kernel(c_ref, w_ref, b_ref, o_ref):
    c = c_ref[...]
    a = _bf(c * jax.nn.sigmoid(c))
    o_ref[0] = _dot(a, _bf(w_ref[0])) + b_ref[0]


def _ada_call(c_all, w_ada, b_ada):
    depth, d, n = w_ada.shape
    nb = c_all.shape[0]
    tn = 1024
    return pl.pallas_call(
        _ada_kernel,
        grid=(depth, n // tn),
        in_specs=[pl.BlockSpec((nb, d), lambda l, j: (0, 0)),
                  pl.BlockSpec((1, d, tn), lambda l, j: (l, 0, j)),
                  pl.BlockSpec((1, 1, tn), lambda l, j: (l, 0, j))],
        out_specs=pl.BlockSpec((1, nb, tn), lambda l, j: (l, 0, j)),
        out_shape=jax.ShapeDtypeStruct((depth, nb, n), jnp.float32),
        compiler_params=_cparams(("arbitrary", "arbitrary")),
        name="ada_mod",
    )(c_all, w_ada, b_ada.reshape(depth, 1, n))


def _inproj_kernel(x_ref, nw_ref, sc_ref, sh_ref, w_ref, o_ref, h_ref, *, tm):
    @pl.when(pl.program_id(1) == 0)
    def _():
        nw = nw_ref[...]

        def body(g, carry):
            rows = pl.ds(pl.multiple_of(g * GROUP, GROUP), GROUP)
            x = x_ref[rows, :]
            y = x * lax.rsqrt(jnp.mean(x * x, axis=-1, keepdims=True) + EPS) * nw
            hh = y * (1.0 + sc_ref[pl.ds(g, 1), :]) + sh_ref[pl.ds(g, 1), :]
            h_ref[rows, :] = _bf(hh)
            return carry

        lax.fori_loop(0, tm // GROUP, body, 0)

    o_ref[...] = _dot(h_ref[...], w_ref[...])


def _inproj_call(x, norm_w, scale_g, shift_g, w_bf):
    m, d = x.shape
    n = w_bf.shape[1]
    tm = min(1024, m)
    tn = 1024
    gpt = tm // GROUP
    return pl.pallas_call(
        functools.partial(_inproj_kernel, tm=tm),
        grid=(m // tm, n // tn),
        in_specs=[pl.BlockSpec((tm, d), lambda i, j: (i, 0)),
                  pl.BlockSpec((1, d), lambda i, j: (0, 0)),
                  pl.BlockSpec((gpt, d), lambda i, j: (i, 0)),
                  pl.BlockSpec((gpt, d), lambda i, j: (i, 0)),
                  pl.BlockSpec((d, tn), lambda i, j: (0, j))],
        out_specs=pl.BlockSpec((tm, tn), lambda i, j: (i, j)),
        out_shape=jax.ShapeDtypeStruct((m, n), jnp.float32),
        scratch_shapes=[pltpu.VMEM((tm, d), jnp.bfloat16)],
        compiler_params=_cparams(("arbitrary", "arbitrary")),
        name="inproj",
    )(x, norm_w.reshape(1, d), scale_g, shift_g, w_bf)


def _silu(x):
    hx = 0.5 * x
    return hx * jnp.tanh(hx) + hx


def _l2norm(x):
    return x * lax.rsqrt(jnp.sum(x * x, axis=-1, keepdims=True) + EPS)


def _softplus(x):
    return jnp.maximum(x, 0.0) + jnp.log1p(jnp.exp(-jnp.abs(x)))


INV_BASE = 8
HEADS_PER_PASS = GDN_HEADS


def _unit_lower_inverses(As, ii, jj, eye):
    C = As[0].shape[0]
    sh = INV_BASE.bit_length() - 1
    Ns = [jnp.where((ii >> sh) == (jj >> sh), -A, 0.0) for A in As]
    Ps = [eye + N for N in Ns]
    m = 2
    while m < INV_BASE:
        Nbs = [_bf(N) for N in Ns]
        Ns = [_dot(Nb, Nb) for Nb in Nbs]
        Ps = [P + _dot(_bf(P), _bf(N)) for P, N in zip(Ps, Ns)]
        m *= 2
    s = INV_BASE
    while s < C:
        sh = s.bit_length() - 1
        off = ((ii >> (sh + 1)) == (jj >> (sh + 1))) & (((ii >> sh) & 1) == 1) & (((jj >> sh) & 1) == 0)
        Pbs = [_bf(P) for P in Ps]
        Xs = [_dot(Pb, _bf(jnp.where(off, A, 0.0))) for Pb, A in zip(Pbs, As)]
        Ps = [P - _dot(_bf(X), Pb) for P, X, Pb in zip(Ps, Xs, Pbs)]
        s *= 2
    return Ps


def _gdn_kernel(*refs, has_state):
    if has_state:
        (qkv_ref, z_ref, sm_ref, wc_ref, lrow_ref, drow_ref, gw_ref, cp_ref, s0_ref,
         o_ref, sout_ref, tout_ref, S_ref, xe_ref) = refs
    else:
        (qkv_ref, z_ref, sm_ref, wc_ref, lrow_ref, drow_ref, gw_ref,
         o_ref, sout_ref, tout_ref, S_ref, xe_ref) = refs
    i = pl.program_id(1)
    C = CHUNK
    hw = GDN_HEADS * GDN_DK

    @pl.when(i == 0)
    def _():
        if has_state:
            S_ref[...] = s0_ref[0, 0]
            xe_ref[0:8, :] = cp_ref[0]
        else:
            S_ref[...] = jnp.zeros(S_ref.shape, jnp.float32)
            xe_ref[0:8, :] = jnp.zeros((8, xe_ref.shape[1]), jnp.float32)

    @pl.when(i > 0)
    def _():
        xe_ref[0:8, :] = xe_ref[C:C + 8, :]

    xe_ref[8:8 + C, :] = qkv_ref[...]

    sm = sm_ref[...]
    beta_all = jax.nn.sigmoid(sm)
    g_all = -jnp.exp(lrow_ref[...]) * _softplus(sm + drow_ref[...])
    rowc = lax.broadcasted_iota(jnp.int32, (C, LANES), 0)
    gc = g_all
    s = 1
    while s < C:
        gc = gc + jnp.where(rowc >= s, pltpu.roll(gc, s, 0), 0.0)
        s *= 2
    glast = gc[C - 1:C, :]
    egc_all = jnp.exp(gc)
    ekd_all = jnp.exp(glast - gc)
    egl_all = jnp.exp(glast)
    gc_t = gc.T

    ii = lax.broadcasted_iota(jnp.int32, (C, C), 0)
    jj = lax.broadcasted_iota(jnp.int32, (C, C), 1)
    eye = jnp.where(ii == jj, 1.0, 0.0)
    gw = gw_ref[...]


    def col(p, h):
        return slice(p * hw + h * GDN_DK, p * hw + (h + 1) * GDN_DK)

    def conv(p, h):
        w = wc_ref[:, col(p, h)]
        y = xe_ref[8:8 + C, col(p, h)] * w[CONV_W - 1:CONV_W]
        for s in range(1, CONV_W):
            y = y + xe_ref[8 - s:8 - s + C, col(p, h)] * w[CONV_W - 1 - s:CONV_W - s]
        return _silu(y)

    def lane(a, l):
        return a[:, l:l + 1]

    def run(heads):
        ks = [_l2norm(conv(1, h)) for h in heads]
        qs = [_l2norm(conv(0, h)) * (GDN_DK ** -0.5) for h in heads]
        kbs = [k * lane(beta_all, SM_BA + h) for h, k in zip(heads, ks)]
        kqs = [_dot_nt(_bf(jnp.concatenate([kb, q], axis=0)), _bf(k)) for kb, q, k in zip(kbs, qs, ks)]
        decays = [jnp.where(ii >= jj,
                            jnp.exp(jnp.minimum(lane(gc, SM_AA + h) - gc_t[SM_AA + h:SM_AA + h + 1, :], 0.0)),
                            0.0)
                  for h in heads]
        As = [jnp.where(ii > jj, kq[:C] * d, 0.0) for kq, d in zip(kqs, decays)]
        qks = [_bf(kq[C:] * d) for kq, d in zip(kqs, decays)]
        Ps = _unit_lower_inverses(As, ii, jj, eye)
        vs = [conv(2, h) for h in heads]
        rhs = [_bf(jnp.concatenate([v * lane(beta_all, SM_BA + h), kb * lane(egc_all, SM_AA + h)], axis=1))
               for h, v, kb in zip(heads, vs, kbs)]
        uws = [_dot(_bf(P), r) for P, r in zip(Ps, rhs)]
        Ss = [S_ref[h] for h in heads]
        wqs = [_dot(_bf(jnp.concatenate([uw[:, GDN_DV:], q * lane(egc_all, SM_AA + h)], axis=0)), _bf(S))
               for h, uw, q, S in zip(heads, uws, qs, Ss)]
        vnbs = [_bf(uw[:, :GDN_DV] - wq[:C]) for uw, wq in zip(uws, wqs)]
        kdts = [_bf((k * lane(ekd_all, SM_AA + h)).T) for h, k in zip(heads, ks)]
        for h, S, kdt, vnb in zip(heads, Ss, kdts, vnbs):
            S_ref[h] = S * lane(egl_all, SM_AA + h) + _dot(kdt, vnb)
        os_ = [wq[C:] + _dot(qk, vnb) for wq, qk, vnb in zip(wqs, qks, vnbs)]
        for h, o in zip(heads, os_):
            o = o * lax.rsqrt(jnp.mean(o * o, axis=-1, keepdims=True) + EPS) * gw
            z = z_ref[:, col(0, h)]
            o_ref[:, col(0, h)] = _bf(o * _silu(z))

    for h0 in range(0, GDN_HEADS, HEADS_PER_PASS):
        run(range(h0, h0 + HEADS_PER_PASS))

    @pl.when(i == pl.num_programs(1) - 1)
    def _():
        sout_ref[0] = S_ref[...]
        tout_ref[0] = xe_ref[C:C + 8, :]


def _gdn_call(proj, wconv8, lrow, drow, gw, *, nbatch, t, state=None):
    nt = t // CHUNK
    hh = GDN_HEADS
    const = lambda b, i: (0, 0)
    state_spec = pl.BlockSpec((1, hh, GDN_DK, GDN_DV), lambda b, i: (b, 0, 0, 0))
    tail_spec = pl.BlockSpec((1, 8, W_QKV), lambda b, i: (b, 0, 0))
    in_specs = [pl.BlockSpec((CHUNK, W_QKV), lambda b, i: (b * nt + i, OFF_QKV // W_QKV)),
                pl.BlockSpec((CHUNK, D_MODEL), lambda b, i: (b * nt + i, OFF_ZA // D_MODEL)),
                pl.BlockSpec((CHUNK, LANES), lambda b, i: (b * nt + i, OFF_SM // LANES)),
                pl.BlockSpec((8, W_QKV), const),
                pl.BlockSpec((1, LANES), const),
                pl.BlockSpec((1, LANES), const),
                pl.BlockSpec((1, GDN_DV), const)]
    args = [proj, proj, proj, wconv8, lrow, drow, gw]
    if state is not None:
        conv_prev8, s_all, layer = state
        in_specs += [tail_spec,
                     pl.BlockSpec((1, 1, hh, GDN_DK, GDN_DV), lambda b, i: (layer, b, 0, 0, 0))]
        args += [conv_prev8, s_all]
    return pl.pallas_call(
        functools.partial(_gdn_kernel, has_state=state is not None),
        grid=(nbatch, nt),
        in_specs=in_specs,
        out_specs=[pl.BlockSpec((CHUNK, hh * GDN_DV), lambda b, i: (b * nt + i, 0)),
                   state_spec, tail_spec],
        out_shape=[jax.ShapeDtypeStruct((nbatch * t, hh * GDN_DV), jnp.bfloat16),
                   jax.ShapeDtypeStruct((nbatch, hh, GDN_DK, GDN_DV), jnp.float32),
                   jax.ShapeDtypeStruct((nbatch, 8, W_QKV), jnp.float32)],
        scratch_shapes=[pltpu.VMEM((hh, GDN_DK, GDN_DV), jnp.float32),
                        pltpu.VMEM((8 + CHUNK, W_QKV), jnp.float32)],
        compiler_params=_cparams(("arbitrary", "arbitrary")),
        name="gdn",
    )(*args)


_INT_MIN = -2147483648
_KEY_NEG_INF = -2139095041
ATT_GROUP = ATT_HEADS // KV_HEADS
CNT_VREGS = 8


def _visible_chunks(q0, rows, nreal, kc, nkc):
    lim_max = jnp.minimum(((q0 + rows - 1) // CHUNK + 1) * CHUNK, nreal)
    return jnp.minimum((lim_max + kc - 1) // kc, nkc)


def _sel_kernel(*refs, tq, rg, npart, t_cur, past, lp, kc, topk):
    if past:
        qi_ref, smq_ref, smk_ref, kip_ref, m_ref, kibf, key_ref, qis_ref = refs
    else:
        qi_ref, smq_ref, smk_ref, m_ref, kibf, key_ref, qis_ref = refs
    i = pl.program_id(1)
    nreal = past + t_cur
    nkc = lp // kc
    nl = kc // LANES
    groups = range(rg)
    pw = tq // npart

    @pl.when(i == 0)
    def _():
        for p in range(npart):
            if past:
                kibf[p, 0:past, :] = _bf(kip_ref[0, p])
            kibf[p, past:nreal, :] = _bf(smk_ref[p * t_cur:(p + 1) * t_cur, SM_KI:SM_KI + IDX_DIM])
            if lp > nreal:
                kibf[p, nreal:lp, :] = jnp.zeros((lp - nreal, IDX_DIM), jnp.bfloat16)

    q0 = past + i * (rg * pw)
    nvis = _visible_chunks(q0, rg * pw, nreal, kc, nkc)
    tlane = lax.broadcasted_iota(jnp.int32, (1, tq), 1)
    if npart > 1:
        tlane = tlane % pw
    lims = [jnp.minimum(((q0 + g * tq + tlane) // CHUNK + 1) * CHUNK, nreal) for g in groups]

    def rows_of(g):
        return slice(g * tq, (g + 1) * tq)

    def keys_of(c, j=None):
        if j is None:
            return pl.ds(pl.multiple_of(c * kc, kc), kc)
        return pl.ds(pl.multiple_of(c * kc + j * LANES, LANES), LANES)

    def key_pos(c, n, j=0, width=tq):
        return c * kc + j * LANES + lax.broadcasted_iota(jnp.int32, (n, width), 0)

    for g in groups:
        for p in range(npart):
            part = slice(g * tq + p * pw, g * tq + (p + 1) * pw)
            w_t = (smq_ref[part, :] * ((IDX_HEADS ** -0.5) * (IDX_DIM ** -0.5))).T
            for hh in range(IDX_HEADS):
                qis_ref[hh * pw:(hh + 1) * pw, :] = _bf(qi_ref[part, hh * IDX_DIM:(hh + 1) * IDX_DIM])
            lim = lims[g][:, p * pw:(p + 1) * pw]

            def score_chunk(c, carry, p=p, part=part, w_t=w_t, lim=lim):
                d = _dot_nt(kibf[p, keys_of(c), :], qis_ref[0:IDX_HEADS * pw, :])
                acc = jnp.zeros((kc, pw), jnp.float32)
                for hh in range(IDX_HEADS):
                    acc = acc + (w_t[SM_WI + hh:SM_WI + hh + 1, :]
                                 * jnp.maximum(d[:, hh * pw:(hh + 1) * pw], 0.0))
                acc = jnp.where(key_pos(c, kc, width=pw) < lim, acc, -jnp.inf)
                bits = pltpu.bitcast(acc, jnp.int32)
                key_ref[keys_of(c), part] = jnp.where(bits < 0, bits ^ 0x7FFFFFFF, bits)
                return carry

            lax.fori_loop(0, nvis, score_chunk, 0)

    all_visible = past >= (nkc - 1) * kc

    def count_ge(cands):
        def count(keys, w, accs):
            accs = list(accs)
            for g in groups:
                hit = jnp.where(key_ref[keys, rows_of(g)] >= cands[g], 1.0, 0.0)
                accs[g] = accs[g] + jnp.sum(hit.reshape(w // cnt_rows, cnt_rows, tq), axis=0)
            return tuple(accs)

        cnt_rows = 8 * max(1, CNT_VREGS // rg)
        accs = tuple(jnp.zeros((cnt_rows, tq), jnp.float32) for _ in groups)
        if all_visible:
            accs = count(slice(0, lp), lp, accs)
        else:
            accs = lax.fori_loop(0, nvis, lambda c, a: count(keys_of(c), kc, a), accs)
        return [jnp.sum(a, axis=0, keepdims=True) for a in accs]

    def bit_step(it, taus_u):
        cands_u = [t | lax.shift_left(jnp.int32(1), 31 - it) for t in taus_u]
        cnts = count_ge([c ^ _INT_MIN for c in cands_u])
        return tuple(jnp.where(n >= float(topk), c, t) for n, c, t in zip(cnts, cands_u, taus_u))

    taus_u = lax.fori_loop(0, 32, bit_step, tuple(jnp.zeros((1, tq), jnp.int32) for _ in groups))
    taus = [t ^ _INT_MIN for t in taus_u]
    cnts_ge = count_ge(taus)
    cnts_gt = count_ge([t + 1 for t in taus])
    needs = [float(topk) - n for n in cnts_gt]
    any_excess = jnp.int32(0)
    for g in groups:
        excess = (cnts_ge[g] > float(topk)) & (taus[g] > _KEY_NEG_INF)
        any_excess = jnp.maximum(any_excess, jnp.max(jnp.where(excess, 1, 0)))

    ea = lax.broadcasted_iota(jnp.int32, (tq, tq), 0)
    eb = lax.broadcasted_iota(jnp.int32, (tq, tq), 1)
    eye = jnp.where(ea == eb, 1.0, 0.0).astype(jnp.bfloat16)

    def store_mask(g, c, j, sel_t):
        sel = _dot_nt(eye, jnp.where(sel_t, 1.0, 0.0).astype(jnp.bfloat16))
        m_ref[rows_of(g), keys_of(c, j)] = _bf(jnp.where(sel > 0.5, 0.0, -jnp.inf))

    @pl.when(any_excess == 0)
    def _():
        def body(c, carry):
            for g in groups:
                sel_t = (key_ref[keys_of(c), rows_of(g)] >= taus[g]) & (key_pos(c, kc) < lims[g])
                store_mask(g, c, None, sel_t)
            return carry

        lax.fori_loop(0, nvis, body, 0)

    @pl.when(any_excess != 0)
    def _():
        la = lax.broadcasted_iota(jnp.int32, (LANES, LANES), 0)
        lb = lax.broadcasted_iota(jnp.int32, (LANES, LANES), 1)
        lower = jnp.where(la >= lb, 1.0, 0.0).astype(jnp.bfloat16)

        def body(c, carries):
            carries = list(carries)
            for j in range(nl):
                for g in groups:
                    key = key_ref[keys_of(c, j), rows_of(g)]
                    eq = key == taus[g]
                    pref = _dot(lower, jnp.where(eq, 1.0, 0.0).astype(jnp.bfloat16)) + carries[g]
                    sel_t = ((key > taus[g]) | (eq & (pref <= needs[g]))) & (key_pos(c, LANES, j) < lims[g])
                    store_mask(g, c, j, sel_t)
                    carries[g] = pref[LANES - 1:LANES, :]
            return tuple(carries)

        lax.fori_loop(0, nvis, body, tuple(jnp.zeros((1, tq), jnp.float32) for _ in groups))

    def fill(c, carry):
        m_ref[:, keys_of(c)] = jnp.full((rg * tq, kc), -jnp.inf, jnp.bfloat16)
        return carry

    lax.fori_loop(nvis, nkc, fill, 0)


def _sel_call(proj, *, nbatch, t, past_ki=None):
    past = 0 if past_ki is None else past_ki[0].shape[2]
    _, kc, lp, topk = _attn_geometry(t, past)
    npart = LANES // t if (t < LANES and nbatch % (LANES // t) == 0) else 1
    tq = min(LANES, t * npart)
    rg = min(4, (t * npart) // tq)
    rows = rg * tq
    nr = (t * npart) // rows
    qiw = IDX_HEADS * IDX_DIM
    in_specs = [pl.BlockSpec((rows, qiw), lambda b, i: (b * nr + i, OFF_QI // qiw)),
                pl.BlockSpec((rows, LANES), lambda b, i: (b * nr + i, OFF_SM // LANES)),
                pl.BlockSpec((npart * t, LANES), lambda b, i: (b, OFF_SM // LANES))]
    args = [proj, proj, proj]
    if past:
        ki_all, layer = past_ki
        in_specs.append(pl.BlockSpec((1, npart, past, IDX_DIM), lambda b, i: (layer, b, 0, 0)))
        args.append(ki_all)
    return pl.pallas_call(
        functools.partial(_sel_kernel, tq=tq, rg=rg, npart=npart, t_cur=t, past=past, lp=lp, kc=kc, topk=topk),
        grid=(nbatch // npart, nr),
        in_specs=in_specs,
        out_specs=pl.BlockSpec((rows, lp), lambda b, i: (b * nr + i, 0)),
        out_shape=jax.ShapeDtypeStruct((nbatch * t, lp), jnp.bfloat16),
        scratch_shapes=[pltpu.VMEM((npart, lp, IDX_DIM), jnp.bfloat16),
                        pltpu.VMEM((lp, rows), jnp.int32),
                        pltpu.VMEM((IDX_HEADS * tq, IDX_DIM), jnp.bfloat16)],
        compiler_params=_cparams(("arbitrary", "arbitrary")),
        name="sel_past" if past else "sel",
    )(*args)


def _attn_geometry(t, past):
    tq = min(128, t)
    nreal = past + t
    kc = 512 if nreal % 512 == 0 else 384
    if nreal < kc:
        kc = LANES * (-(-nreal // LANES))
    lp = kc * (-(-nreal // kc))
    assert past % LANES == 0 and past + LANES * (-(-t // LANES)) <= lp
    return tq, kc, lp, min(TOPK_MAX, nreal // 4)


def _attn_kernel(*refs, tq, t_cur, past, lp, kc):
    if past:
        (qb_ref, zb_ref, madd_ref, k_ref, v_ref, kp_ref, vp_ref, na_ref, nb_ref,
         o_ref, ko_ref, vo_ref, kbf, vbf, lg_ref, qs_ref, acc_ref, den_ref, mb_ref) = refs
    else:
        (qb_ref, zb_ref, madd_ref, k_ref, v_ref, na_ref, nb_ref,
         o_ref, ko_ref, vo_ref, kbf, vbf, lg_ref, qs_ref, acc_ref, den_ref, mb_ref) = refs
    i = pl.program_id(1)
    nreal = past + t_cur
    G = ATT_GROUP
    gt = G * tq

    @pl.when(i == 0)
    def _():
        for n in range(KV_HEADS):
            ncol = slice(n * HEAD_DIM, (n + 1) * HEAD_DIM)
            if past:
                kbf[0:past, ncol] = _bf(kp_ref[pl.ds(n, past, stride=KV_HEADS), :])
                vbf[0:past, ncol] = _bf(vp_ref[pl.ds(n, past, stride=KV_HEADS), :])
            ko_ref[pl.ds(n, t_cur, stride=KV_HEADS), :] = k_ref[:, ncol]
            vo_ref[pl.ds(n, t_cur, stride=KV_HEADS), :] = v_ref[:, ncol]
        kbf[past:nreal, :] = _bf(k_ref[...])
        vbf[past:nreal, :] = _bf(v_ref[...])
        if lp > nreal:
            kbf[nreal:lp, :] = jnp.zeros((lp - nreal, KV_HEADS * HEAD_DIM), jnp.bfloat16)
            vbf[nreal:lp, :] = jnp.zeros((lp - nreal, KV_HEADS * HEAD_DIM), jnp.bfloat16)

    q0 = past + i * tq

    far_end = jnp.maximum(q0 - LANES, 0)
    nfull = far_end // kc
    nleft = (far_end - nfull * kc) // LANES
    tail0 = pl.multiple_of(far_end, LANES)
    tw = 2 * LANES
    first = q0 == 0

    def keys_at(off, w):
        return pl.ds(pl.multiple_of(off, LANES), w)

    scale = HEAD_DIM ** -0.5
    for hd in range(ATT_HEADS):
        qs_ref[hd * tq:(hd + 1) * tq, :] = _bf(qb_ref[:, hd * HEAD_DIM:(hd + 1) * HEAD_DIM] * scale)

    for n in range(KV_HEADS):
        ncol = slice(n * HEAD_DIM, (n + 1) * HEAD_DIM)
        grows = slice(n * gt, (n + 1) * gt)

        def logits(off, w, bias=None):
            sc = _dot_nt(qs_ref[grows, :], kbf[keys_at(off, w), ncol])
            ma = madd_ref[:, keys_at(off, w)].astype(jnp.float32)
            for g in range(G):
                r = slice(g * tq, (g + 1) * tq)
                s = sc[r] + ma
                if bias is not None:
                    s = s + bias(n * G + g)
                lg_ref[r, keys_at(off, w)] = s
                mt = mb_ref[r, :]
                for j in range(w // LANES):
                    mt = jnp.maximum(mt, s[:, j * LANES:(j + 1) * LANES])
                mb_ref[r, :] = mt

        def tail_bias(hd):
            zero = jnp.zeros((tq, LANES), jnp.float32)
            return jnp.concatenate([jnp.where(first, nb_ref[hd], na_ref[hd]),
                                    jnp.where(first, zero, nb_ref[hd])], axis=1)

        def weighted_values(off, w):
            p = jnp.exp(lg_ref[:, keys_at(off, w)] - jnp.concatenate([mb_ref[...]] * (w // LANES), axis=1))
            den = den_ref[...]
            for j in range(w // LANES):
                den = den + p[:, j * LANES:(j + 1) * LANES]
            den_ref[...] = den
            acc_ref[...] = acc_ref[...] + _dot(_bf(p), vbf[keys_at(off, w), ncol])

        def walk(fn, tail_kwargs):
            lax.fori_loop(0, nfull, lambda c, carry: (fn(c * kc, kc), carry)[1], 0)
            lax.fori_loop(0, nleft, lambda b, carry: (fn(nfull * kc + b * LANES, LANES), carry)[1], 0)
            fn(tail0, tw, **tail_kwargs)

        mb_ref[...] = jnp.full((gt, LANES), -jnp.inf, jnp.float32)
        walk(logits, dict(bias=tail_bias))
        for g in range(G):
            r = slice(g * tq, (g + 1) * tq)
            mb_ref[r, :] = jnp.broadcast_to(jnp.max(mb_ref[r, :], axis=1, keepdims=True), (tq, LANES))

        acc_ref[...] = jnp.zeros((gt, HEAD_DIM), jnp.float32)
        den_ref[...] = jnp.zeros((gt, LANES), jnp.float32)
        walk(weighted_values, {})

        for g in range(G):
            r = slice(g * tq, (g + 1) * tq)
            hcol = slice((n * G + g) * HEAD_DIM, (n * G + g + 1) * HEAD_DIM)
            den = jnp.sum(den_ref[r, :], axis=1, keepdims=True)
            z = zb_ref[:, hcol]
            o_ref[:, hcol] = _bf((acc_ref[r, :] / den) * _silu(z))


def _attn_call(proj, madd, near_a, near_b, *, nbatch, t, past_kv=None):
    past = 0 if past_kv is None else past_kv[0].shape[2] // KV_HEADS
    tq, kc, lp, _ = _attn_geometry(t, past)
    nq = t // tq
    kvw = KV_HEADS * HEAD_DIM
    gt = ATT_GROUP * tq

    in_specs = [pl.BlockSpec((tq, D_MODEL), lambda b, i: (b * nq + i, OFF_QB // D_MODEL)),
                pl.BlockSpec((tq, D_MODEL), lambda b, i: (b * nq + i, OFF_ZB // D_MODEL)),
                pl.BlockSpec((tq, lp), lambda b, i: (b * nq + i, 0)),
                pl.BlockSpec((t, kvw), lambda b, i: (b, OFF_KB // kvw)),
                pl.BlockSpec((t, kvw), lambda b, i: (b, OFF_VB // kvw))]
    args = [proj, proj, madd, proj, proj]
    if past:
        k_all, v_all, layer = past_kv
        in_specs += [pl.BlockSpec((None, None, past * KV_HEADS, HEAD_DIM), lambda b, i: (layer, b, 0, 0)),
                     pl.BlockSpec((None, None, past * KV_HEADS, HEAD_DIM), lambda b, i: (layer, b, 0, 0))]
        args += [k_all, v_all]
    in_specs += [pl.BlockSpec((ATT_HEADS, tq, LANES), lambda b, i: (0, 0, 0)),
                 pl.BlockSpec((ATT_HEADS, tq, LANES), lambda b, i: (0, 0, 0))]
    args += [near_a, near_b]

    return pl.pallas_call(
        functools.partial(_attn_kernel, tq=tq, t_cur=t, past=past, lp=lp, kc=kc),
        grid=(nbatch, nq),
        in_specs=in_specs,
        out_specs=[pl.BlockSpec((tq, D_MODEL), lambda b, i: (b * nq + i, 0)),
                   pl.BlockSpec((None, t * KV_HEADS, HEAD_DIM), lambda b, i: (b, 0, 0)),
                   pl.BlockSpec((None, t * KV_HEADS, HEAD_DIM), lambda b, i: (b, 0, 0))],
        out_shape=[jax.ShapeDtypeStruct((nbatch * t, D_MODEL), jnp.bfloat16),
                   jax.ShapeDtypeStruct((nbatch, t * KV_HEADS, HEAD_DIM), jnp.float32),
                   jax.ShapeDtypeStruct((nbatch, t * KV_HEADS, HEAD_DIM), jnp.float32)],
        scratch_shapes=[pltpu.VMEM((lp, kvw), jnp.bfloat16),
                        pltpu.VMEM((lp, kvw), jnp.bfloat16),
                        pltpu.VMEM((gt, lp), jnp.float32),
                        pltpu.VMEM((ATT_HEADS * tq, HEAD_DIM), jnp.bfloat16),
                        pltpu.VMEM((gt, HEAD_DIM), jnp.float32),
                        pltpu.VMEM((gt, LANES), jnp.float32),
                        pltpu.VMEM((gt, LANES), jnp.float32)],
        compiler_params=_cparams(("arbitrary", "arbitrary")),
        name="attn_past" if past else "attn",
    )(*args)


def _merge_kernel(oa_ref, ob_ref, wa_ref, wb_ref, ga_ref, gb_ref, o_ref):
    ya = _dot(oa_ref[...], wa_ref[...])
    yb = _dot(ob_ref[...], wb_ref[...])
    o_ref[...] = _bf(jax.nn.sigmoid(ga_ref[...]) * ya + jax.nn.sigmoid(gb_ref[...]) * yb)


def _merge_call(oa, ob, wa_bf, wb_bf, proj):
    m, d = oa.shape
    tm = min(1024, m)
    tn = 512
    return pl.pallas_call(
        _merge_kernel,
        grid=(m // tm, d // tn),
        in_specs=[pl.BlockSpec((tm, d), lambda i, j: (i, 0)),
                  pl.BlockSpec((tm, d), lambda i, j: (i, 0)),
                  pl.BlockSpec((d, tn), lambda i, j: (0, j)),
                  pl.BlockSpec((d, tn), lambda i, j: (0, j)),
                  pl.BlockSpec((tm, tn), lambda i, j: (i, OFF_GLA // tn + j)),
                  pl.BlockSpec((tm, tn), lambda i, j: (i, OFF_GLB // tn + j))],
        out_specs=pl.BlockSpec((tm, tn), lambda i, j: (i, j)),
        out_shape=jax.ShapeDtypeStruct((m, d), jnp.bfloat16),
        compiler_params=_cparams(("arbitrary", "arbitrary")),
        name="merge",
    )(oa, ob, wa_bf, wb_bf, proj, proj)


def _outproj_kernel(*refs, tm, final_norm):
    if final_norm:
        mg_ref, w_ref, x_ref, gate_ref, nw_ref, o_ref = refs
    else:
        mg_ref, w_ref, x_ref, gate_ref, o_ref = refs
    y = _dot(mg_ref[...], w_ref[...])
    for g in range(tm // GROUP):
        r = slice(g * GROUP, (g + 1) * GROUP)
        xn = x_ref[r, :] + gate_ref[g:g + 1, :] * y[r]
        if final_norm:
            xn = xn * lax.rsqrt(jnp.mean(xn * xn, axis=-1, keepdims=True) + EPS) * nw_ref[...]
        o_ref[r, :] = xn


def _outproj_call(merged, wo_bf, x, gate_g, final_norm_w=None):
    m, d = x.shape
    tm = min(512, m)
    gpt = tm // GROUP
    in_specs = [pl.BlockSpec((tm, d), lambda i: (i, 0)),
                pl.BlockSpec((d, d), lambda i: (0, 0)),
                pl.BlockSpec((tm, d), lambda i: (i, 0)),
                pl.BlockSpec((gpt, d), lambda i: (i, 0))]
    args = [merged, wo_bf, x, gate_g]
    if final_norm_w is not None:
        in_specs.append(pl.BlockSpec((1, d), lambda i: (0, 0)))
        args.append(final_norm_w.reshape(1, d))
    return pl.pallas_call(
        functools.partial(_outproj_kernel, tm=tm, final_norm=final_norm_w is not None),
        grid=(m // tm,),
        in_specs=in_specs,
        out_specs=pl.BlockSpec((tm, d), lambda i: (i, 0)),
        out_shape=jax.ShapeDtypeStruct((m, d), jnp.float32),
        compiler_params=_cparams(("arbitrary",)),
        name="outproj",
    )(*args)


def _relayout_w_in(w):
    offs = [0]
    for s in _IN_SIZES:
        offs.append(offs[-1] + s)
    (qkv, za, ba, aa, qb, kb, vb, zb, qi, ki, wi, gla, glb) = [w[:, offs[n]:offs[n + 1]] for n in range(13)]
    d = w.shape[0]
    pad_sm = jnp.zeros((d, LANES - (IDX_DIM + 3 * GDN_HEADS)), w.dtype)
    cols = [qkv, za, qb, zb, gla, glb, qi, kb, vb, ki, ba, aa, wi, pad_sm]
    out = jnp.concatenate(cols, axis=1)
    pad = jnp.zeros((d, N_PROJ - out.shape[1]), w.dtype)
    return jnp.concatenate([out, pad], axis=1).astype(jnp.bfloat16)


def _rel_bucket(rel):
    nb = REL_BUCKETS // 2
    max_exact = nb // 2
    n = jnp.abs(rel)
    nf = jnp.maximum(n, 1).astype(jnp.float32)
    large = max_exact + (jnp.log(nf / max_exact) / math.log(REL_MAX_DIST / max_exact)
                         * (nb - max_exact)).astype(jnp.int32)
    large = jnp.minimum(large, nb - 1)
    return jnp.where(rel > 0, nb, 0) + jnp.where(n < max_exact, n, large)


def _bias_tables(rel_bias):
    tq = LANES
    trow = jnp.arange(tq)[:, None]
    col = jnp.arange(2 * LANES)[None, :]
    rel = (col - LANES) - trow
    bucket = _rel_bucket(rel)
    tab = sum(jnp.where(bucket == b, rel_bias[b][:, None, None], 0.0) for b in range(REL_BUCKETS))
    far = rel_bias[REL_BUCKETS // 2 - 1]
    tab = tab - far[:, None, None]
    return tab[:, :, :LANES], tab[:, :, LANES:]


def _pad_rows8(a):
    z = jnp.zeros(a.shape[:-2] + (8 - a.shape[-2], a.shape[-1]), a.dtype)
    return jnp.concatenate([z, a], axis=-2)


def _lane_row(vals, off):
    r = jnp.zeros((1, LANES), jnp.float32)
    return r.at[0, off:off + vals.shape[0]].set(vals)


def _layer(x, mod, lw, tables, *, nbatch, t, caches=None, final_norm_w=None):
    (norm_w, w_in_bf, wconv8, lrow, drow, gw, wa_bf, wb_bf, wo_bf) = lw
    near_a, near_b = tables
    d = D_MODEL
    gpb = t // GROUP

    def per_group(a):
        return jnp.broadcast_to(a[:, None, :], (nbatch, gpb, d)).reshape(nbatch * gpb, d)

    shift_g, scale_g, gate_g = [per_group(mod[:, n * d:(n + 1) * d]) for n in range(3)]
    if caches is None:
        state = past_ki = past_kv = None
    else:
        layer, k_all, v_all, ki_all, s_all, conv_prev = caches
        state = (_pad_rows8(conv_prev), s_all, layer)
        past_ki = (ki_all, layer)
        past_kv = (k_all, v_all, layer)

    proj = _inproj_call(x, norm_w, scale_g, shift_g, w_in_bf)
    oa, s_new, tails = _gdn_call(proj, wconv8, lrow, drow, gw, nbatch=nbatch, t=t, state=state)
    tq = min(LANES, t)
    madd = _sel_call(proj, nbatch=nbatch, t=t, past_ki=past_ki)
    ob, k_rows, v_rows = _attn_call(proj, madd, near_a[:, :tq], near_b[:, :tq], nbatch=nbatch, t=t,
                                    past_kv=past_kv)
    merged = _merge_call(oa, ob, wa_bf, wb_bf, proj)
    x_new = _outproj_call(merged, wo_bf, x, gate_g, final_norm_w)

    k_new = k_rows.reshape(nbatch, t, KV_HEADS, HEAD_DIM)
    v_new = v_rows.reshape(nbatch, t, KV_HEADS, HEAD_DIM)
    ki_new = proj[:, OFF_SM + SM_KI:OFF_SM + SM_KI + IDX_DIM].reshape(nbatch, t, IDX_DIM)
    conv_new = tails[:, 8 - (CONV_W - 1):, :]
    return x_new, (k_new, v_new, ki_new, s_new, conv_new)


def kernel(x_prompt, x_sample, c_prompt, c_sample, cache_k, cache_v, cache_idx_k, state_gdn, state_conv,
           norm_w, w_ada, b_ada, w_in, w_conv, a_log, dt_bias, gdn_norm_w, w_branch_a, w_branch_b,
           w_out, rel_bias, final_norm_w):
    depth = w_in.shape[0]
    bp, tp, d = x_prompt.shape
    bs, ts, _ = x_sample.shape
    past = cache_k.shape[2]
    kvw = KV_HEADS * HEAD_DIM

    mod = _ada_call(jnp.concatenate([c_prompt, c_sample], axis=0), w_ada, b_ada)
    tables = _bias_tables(rel_bias)

    xp = x_prompt.reshape(bp * tp, d)
    xs = x_sample.reshape(bs * ts, d)
    new_p, new_s = [], []
    for l in range(depth):
        wconv8 = jnp.concatenate([w_conv[l], jnp.zeros((8 - CONV_W, w_conv.shape[2]), w_conv.dtype)], axis=0)
        lw = (norm_w[l], _relayout_w_in(w_in[l]), wconv8,
              _lane_row(a_log[l], SM_AA), _lane_row(dt_bias[l], SM_AA), gdn_norm_w[l].reshape(1, GDN_DV),
              w_branch_a[l].astype(jnp.bfloat16), w_branch_b[l].astype(jnp.bfloat16),
              w_out[l].astype(jnp.bfloat16))
        fnw = final_norm_w if l == depth - 1 else None
        xp, sp = _layer(xp, mod[l, :bp], lw, tables, nbatch=bp, t=tp, final_norm_w=fnw)
        caches = (l, cache_k.reshape(depth, bs, past * KV_HEADS, HEAD_DIM),
                  cache_v.reshape(depth, bs, past * KV_HEADS, HEAD_DIM), cache_idx_k, state_gdn, state_conv[l])
        xs, ss = _layer(xs, mod[l, bp:], lw, tables, nbatch=bs, t=ts, caches=caches, final_norm_w=fnw)
        new_p.append(sp)
        new_s.append(ss)

    y_prompt = xp.reshape(bp, tp, d)
    y_sample = xs.reshape(bs, ts, d)
    outs_p = [jnp.stack([s[n] for s in new_p]) for n in range(5)]
    outs_s = [jnp.stack([s[n] for s in new_s]) for n in range(5)]
    return (y_prompt, y_sample, *outs_p, *outs_s)
```

```python
import functools
import math

import jax
import jax.numpy as jnp
from jax import lax
from jax.experimental import pallas as pl
from jax.experimental.pallas import tpu as pltpu

D_MODEL = 2048
CHUNK = 64
GDN_HEADS = 16
GDN_DK = 128
GDN_DV = 128
CONV_W = 4
ATT_HEADS = 16
KV_HEADS = 2
HEAD_DIM = 128
IDX_HEADS = 16
IDX_DIM = 64
TOPK_MAX = 256
REL_BUCKETS = 32
REL_MAX_DIST = 128
EPS = 1e-6

LANES = 128
VMEM_LIMIT = 56 * 1024 * 1024

W_QKV = 3 * GDN_HEADS * GDN_DK
OFF_QKV = 0
OFF_ZA = OFF_QKV + W_QKV
OFF_QB = OFF_ZA + D_MODEL
OFF_ZB = OFF_QB + D_MODEL
OFF_GLA = OFF_ZB + D_MODEL
OFF_GLB = OFF_GLA + D_MODEL
OFF_QI = OFF_GLB + D_MODEL
OFF_KB = OFF_QI + IDX_HEADS * IDX_DIM
OFF_VB = OFF_KB + KV_HEADS * HEAD_DIM
OFF_SM = OFF_VB + KV_HEADS * HEAD_DIM
SM_KI, SM_BA, SM_AA, SM_WI = 0, 64, 80, 96
N_PROJ = 18432
GROUP = 64

_IN_SIZES = (W_QKV, D_MODEL, GDN_HEADS, GDN_HEADS, D_MODEL, KV_HEADS * HEAD_DIM, KV_HEADS * HEAD_DIM,
             D_MODEL, IDX_HEADS * IDX_DIM, IDX_DIM, IDX_HEADS, D_MODEL, D_MODEL)


def _cparams(sem):
    return pltpu.CompilerParams(dimension_semantics=sem, vmem_limit_bytes=VMEM_LIMIT)


def _bf(x):
    return x.astype(jnp.bfloat16)


def _dot(a, b):
    return jnp.dot(a, b, preferred_element_type=jnp.float32)


def _dot_nt(a, b):
    return lax.dot_general(a, b, (((1,), (1,)), ((), ())), preferred_element_type=jnp.float32)


def _ada_kernel(c_ref, w_ref, b_ref, o_ref):
    c = c_ref[...]
    a = _bf(c * jax.nn.sigmoid(c))
    o_ref[0] = _dot(a, _bf(w_ref[0])) + b_ref[0]


def _ada_call(c_all, w_ada, b_ada):
    depth, d, n = w_ada.shape
    nb = c_all.shape[0]
    tn = 1024
    return pl.pallas_call(
        _ada_kernel,
        grid=(depth, n // tn),
        in_specs=[pl.BlockSpec((nb, d), lambda l, j: (0, 0)),
                  pl.BlockSpec((1, d, tn), lambda l, j: (l, 0, j)),
                  pl.BlockSpec((1, 1, tn), lambda l, j: (l, 0, j))],
        out_specs=pl.BlockSpec((1, nb, tn), lambda l, j: (l, 0, j)),
        out_shape=jax.ShapeDtypeStruct((depth, nb, n), jnp.float32),
        compiler_params=_cparams(("arbitrary", "arbitrary")),
        name="ada_mod",
    )(c_all, w_ada, b_ada.reshape(depth, 1, n))


def _inproj_kernel(x_ref, nw_ref, sc_ref, sh_ref, w_ref, o_ref, h_ref, *, tm):
    @pl.when(pl.program_id(1) == 0)
    def _():
        nw = nw_ref[...]

        def body(g, carry):
            rows = pl.ds(pl.multiple_of(g * GROUP, GROUP), GROUP)
            x = x_ref[rows, :]
            y = x * lax.rsqrt(jnp.mean(x * x, axis=-1, keepdims=True) + EPS) * nw
            hh = y * (1.0 + sc_ref[pl.ds(g, 1), :]) + sh_ref[pl.ds(g, 1), :]
            h_ref[rows, :] = _bf(hh)
            return carry

        lax.fori_loop(0, tm // GROUP, body, 0)

    o_ref[...] = _dot(h_ref[...], w_ref[...])


def _inproj_call(x, norm_w, scale_g, shift_g, w_bf):
    m, d = x.shape
    n = w_bf.shape[1]
    tm = min(1024, m)
    tn = 1024
    gpt = tm // GROUP
    return pl.pallas_call(
        functools.partial(_inproj_kernel, tm=tm),
        grid=(m // tm, n // tn),
        in_specs=[pl.BlockSpec((tm, d), lambda i, j: (i, 0)),
                  pl.BlockSpec((1, d), lambda i, j: (0, 0)),
                  pl.BlockSpec((gpt, d), lambda i, j: (i, 0)),
                  pl.BlockSpec((gpt, d), lambda i, j: (i, 0)),
                  pl.BlockSpec((d, tn), lambda i, j: (0, j))],
        out_specs=pl.BlockSpec((tm, tn), lambda i, j: (i, j)),
        out_shape=jax.ShapeDtypeStruct((m, n), jnp.float32),
        scratch_shapes=[pltpu.VMEM((tm, d), jnp.bfloat16)],
        compiler_params=_cparams(("arbitrary", "arbitrary")),
        name="inproj",
    )(x, norm_w.reshape(1, d), scale_g, shift_g, w_bf)


def _silu(x):
    hx = 0.5 * x
    return hx * jnp.tanh(hx) + hx


def _l2norm(x):
    return x * lax.rsqrt(jnp.sum(x * x, axis=-1, keepdims=True) + EPS)


def _softplus(x):
    return jnp.maximum(x, 0.0) + jnp.log1p(jnp.exp(-jnp.abs(x)))


INV_BASE = 8
HEADS_PER_PASS = GDN_HEADS


def _unit_lower_inverses(As, ii, jj, eye):
    C = As[0].shape[0]
    sh = INV_BASE.bit_length() - 1
    Ns = [jnp.where((ii >> sh) == (jj >> sh), -A, 0.0) for A in As]
    Ps = [eye + N for N in Ns]
    m = 2
    while m < INV_BASE:
        Nbs = [_bf(N) for N in Ns]
        Ns = [_dot(Nb, Nb) for Nb in Nbs]
        Ps = [P + _dot(_bf(P), _bf(N)) for P, N in zip(Ps, Ns)]
        m *= 2
    s = INV_BASE
    while s < C:
        sh = s.bit_length() - 1
        off = ((ii >> (sh + 1)) == (jj >> (sh + 1))) & (((ii >> sh) & 1) == 1) & (((jj >> sh) & 1) == 0)
        Pbs = [_bf(P) for P in Ps]
        Xs = [_dot(Pb, _bf(jnp.where(off, A, 0.0))) for Pb, A in zip(Pbs, As)]
        Ps = [P - _dot(_bf(X), Pb) for P, X, Pb in zip(Ps, Xs, Pbs)]
        s *= 2
    return Ps


def _gdn_kernel(*refs, has_state):
    if has_state:
        (qkv_ref, z_ref, sm_ref, wc_ref, lrow_ref, drow_ref, gw_ref, cp_ref, s0_ref,
         o_ref, sout_ref, tout_ref, S_ref, xe_ref) = refs
    else:
        (qkv_ref, z_ref, sm_ref, wc_ref, lrow_ref, drow_ref, gw_ref,
         o_ref, sout_ref, tout_ref, S_ref, xe_ref) = refs
    i = pl.program_id(1)
    C = CHUNK
    hw = GDN_HEADS * GDN_DK

    @pl.when(i == 0)
    def _():
        if has_state:
            S_ref[...] = s0_ref[0, 0]
            xe_ref[0:8, :] = cp_ref[0]
        else:
            S_ref[...] = jnp.zeros(S_ref.shape, jnp.float32)
            xe_ref[0:8, :] = jnp.zeros((8, xe_ref.shape[1]), jnp.float32)

    @pl.when(i > 0)
    def _():
        xe_ref[0:8, :] = xe_ref[C:C + 8, :]

    xe_ref[8:8 + C, :] = qkv_ref[...]

    sm = sm_ref[...]
    beta_all = jax.nn.sigmoid(sm)
    g_all = -jnp.exp(lrow_ref[...]) * _softplus(sm + drow_ref[...])
    rowc = lax.broadcasted_iota(jnp.int32, (C, LANES), 0)
    gc = g_all
    s = 1
    while s < C:
        gc = gc + jnp.where(rowc >= s, pltpu.roll(gc, s, 0), 0.0)
        s *= 2
    glast = gc[C - 1:C, :]
    egc_all = jnp.exp(gc)
    ekd_all = jnp.exp(glast - gc)
    egl_all = jnp.exp(glast)
    gc_t = gc.T

    ii = lax.broadcasted_iota(jnp.int32, (C, C), 0)
    jj = lax.broadcasted_iota(jnp.int32, (C, C), 1)
    eye = jnp.where(ii == jj, 1.0, 0.0)
    gw = gw_ref[...]


    def col(p, h):
        return slice(p * hw + h * GDN_DK, p * hw + (h + 1) * GDN_DK)

    def conv(p, h):
        w = wc_ref[:, col(p, h)]
        y = xe_ref[8:8 + C, col(p, h)] * w[CONV_W - 1:CONV_W]
        for s in range(1, CONV_W):
            y = y + xe_ref[8 - s:8 - s + C, col(p, h)] * w[CONV_W - 1 - s:CONV_W - s]
        return _silu(y)

    def lane(a, l):
        return a[:, l:l + 1]

    def run(heads):
        ks = [_l2norm(conv(1, h)) for h in heads]
        qs = [_l2norm(conv(0, h)) * (GDN_DK ** -0.5) for h in heads]
        kbs = [k * lane(beta_all, SM_BA + h) for h, k in zip(heads, ks)]
        kqs = [_dot_nt(_bf(jnp.concatenate([kb, q], axis=0)), _bf(k)) for kb, q, k in zip(kbs, qs, ks)]
        decays = [jnp.where(ii >= jj,
                            jnp.exp(jnp.minimum(lane(gc, SM_AA + h) - gc_t[SM_AA + h:SM_AA + h + 1, :], 0.0)),
                            0.0)
                  for h in heads]
        As = [jnp.where(ii > jj, kq[:C] * d, 0.0) for kq, d in zip(kqs, decays)]
        qks = [_bf(kq[C:] * d) for kq, d in zip(kqs, decays)]
        Ps = _unit_lower_inverses(As, ii, jj, eye)
        vs = [conv(2, h) for h in heads]
        rhs = [_bf(jnp.concatenate([v * lane(beta_all, SM_BA + h), kb * lane(egc_all, SM_AA + h)], axis=1))
               for h, v, kb in zip(heads, vs, kbs)]
        uws = [_dot(_bf(P), r) for P, r in zip(Ps, rhs)]
        Ss = [S_ref[h] for h in heads]
        wqs = [_dot(_bf(jnp.concatenate([uw[:, GDN_DV:], q * lane(egc_all, SM_AA + h)], axis=0)), _bf(S))
               for h, uw, q, S in zip(heads, uws, qs, Ss)]
        vnbs = [_bf(uw[:, :GDN_DV] - wq[:C]) for uw, wq in zip(uws, wqs)]
        kdts = [_bf((k * lane(ekd_all, SM_AA + h)).T) for h, k in zip(heads, ks)]
        for h, S, kdt, vnb in zip(heads, Ss, kdts, vnbs):
            S_ref[h] = S * lane(egl_all, SM_AA + h) + _dot(kdt, vnb)
        os_ = [wq[C:] + _dot(qk, vnb) for wq, qk, vnb in zip(wqs, qks, vnbs)]
        for h, o in zip(heads, os_):
            o = o * lax.rsqrt(jnp.mean(o * o, axis=-1, keepdims=True) + EPS) * gw
            z = z_ref[:, col(0, h)]
            o_ref[:, col(0, h)] = _bf(o * _silu(z))

    for h0 in range(0, GDN_HEADS, HEADS_PER_PASS):
        run(range(h0, h0 + HEADS_PER_PASS))

    @pl.when(i == pl.num_programs(1) - 1)
    def _():
        sout_ref[0] = S_ref[...]
        tout_ref[0] = xe_ref[C:C + 8, :]


def _gdn_call(proj, wconv8, lrow, drow, gw, *, nbatch, t, state=None):
    nt = t // CHUNK
    hh = GDN_HEADS
    const = lambda b, i: (0, 0)
    state_spec = pl.BlockSpec((1, hh, GDN_DK, GDN_DV), lambda b, i: (b, 0, 0, 0))
    tail_spec = pl.BlockSpec((1, 8, W_QKV), lambda b, i: (b, 0, 0))
    in_specs = [pl.BlockSpec((CHUNK, W_QKV), lambda b, i: (b * nt + i, OFF_QKV // W_QKV)),
                pl.BlockSpec((CHUNK, D_MODEL), lambda b, i: (b * nt + i, OFF_ZA // D_MODEL)),
                pl.BlockSpec((CHUNK, LANES), lambda b, i: (b * nt + i, OFF_SM // LANES)),
                pl.BlockSpec((8, W_QKV), const),
                pl.BlockSpec((1, LANES), const),
                pl.BlockSpec((1, LANES), const),
                pl.BlockSpec((1, GDN_DV), const)]
    args = [proj, proj, proj, wconv8, lrow, drow, gw]
    if state is not None:
        conv_prev8, s_all, layer = state
        in_specs += [tail_spec,
                     pl.BlockSpec((1, 1, hh, GDN_DK, GDN_DV), lambda b, i: (layer, b, 0, 0, 0))]
        args += [conv_prev8, s_all]
    return pl.pallas_call(
        functools.partial(_gdn_kernel, has_state=state is not None),
        grid=(nbatch, nt),
        in_specs=in_specs,
        out_specs=[pl.BlockSpec((CHUNK, hh * GDN_DV), lambda b, i: (b * nt + i, 0)),
                   state_spec, tail_spec],
        out_shape=[jax.ShapeDtypeStruct((nbatch * t, hh * GDN_DV), jnp.bfloat16),
                   jax.ShapeDtypeStruct((nbatch, hh, GDN_DK, GDN_DV), jnp.float32),
                   jax.ShapeDtypeStruct((nbatch, 8, W_QKV), jnp.float32)],
        scratch_shapes=[pltpu.VMEM((hh, GDN_DK, GDN_DV), jnp.float32),
                        pltpu.VMEM((8 + CHUNK, W_QKV), jnp.float32)],
        compiler_params=_cparams(("arbitrary", "arbitrary")),
        name="gdn",
    )(*args)


_INT_MIN = -2147483648
_KEY_NEG_INF = -2139095041
ATT_GROUP = ATT_HEADS // KV_HEADS
CNT_VREGS = 8


def _visible_chunks(q0, rows, nreal, kc, nkc):
    lim_max = jnp.minimum(((q0 + rows - 1) // CHUNK + 1) * CHUNK, nreal)
    return jnp.minimum((lim_max + kc - 1) // kc, nkc)


def _sel_kernel(*refs, tq, rg, npart, t_cur, past, lp, kc, topk):
    if past:
        qi_ref, smq_ref, smk_ref, kip_ref, m_ref, kibf, key_ref, qis_ref = refs
    else:
        qi_ref, smq_ref, smk_ref, m_ref, kibf, key_ref, qis_ref = refs
    i = pl.program_id(1)
    nreal = past + t_cur
    nkc = lp // kc
    nl = kc // LANES
    groups = range(rg)
    pw = tq // npart

    @pl.when(i == 0)
    def _():
        for p in range(npart):
            if past:
                kibf[p, 0:past, :] = _bf(kip_ref[0, p])
            kibf[p, past:nreal, :] = _bf(smk_ref[p * t_cur:(p + 1) * t_cur, SM_KI:SM_KI + IDX_DIM])
            if lp > nreal:
                kibf[p, nreal:lp, :] = jnp.zeros((lp - nreal, IDX_DIM), jnp.bfloat16)

    q0 = past + i * (rg * pw)
    nvis = _visible_chunks(q0, rg * pw, nreal, kc, nkc)
    tlane = lax.broadcasted_iota(jnp.int32, (1, tq), 1)
    if npart > 1:
        tlane = tlane % pw
    lims = [jnp.minimum(((q0 + g * tq + tlane) // CHUNK + 1) * CHUNK, nreal) for g in groups]

    def rows_of(g):
        return slice(g * tq, (g + 1) * tq)

    def keys_at(start, w):
        return pl.ds(pl.multiple_of(start, LANES), w)

    def pos_at(start, n, width=tq):
        return start + lax.broadcasted_iota(jnp.int32, (n, width), 0)

    trim = npart == 1 and past == 0 and rg * tq == kc
    if trim:
        nfull = i
        diag = [(g + 1) * tq for g in groups]
    else:
        nfull = nvis
        diag = None

    for g in groups:
        for p in range(npart):
            part = slice(g * tq + p * pw, g * tq + (p + 1) * pw)
            w_t = (smq_ref[part, :] * ((IDX_HEADS ** -0.5) * (IDX_DIM ** -0.5))).T
            for hh in range(IDX_HEADS):
                qis_ref[hh * pw:(hh + 1) * pw, :] = _bf(qi_ref[part, hh * IDX_DIM:(hh + 1) * IDX_DIM])
            lim = lims[g][:, p * pw:(p + 1) * pw]

            def score(start, w, masked, p=p, part=part, w_t=w_t, lim=lim):
                d = _dot_nt(kibf[p, keys_at(start, w), :], qis_ref[0:IDX_HEADS * pw, :])
                acc = jnp.zeros((w, pw), jnp.float32)
                for hh in range(IDX_HEADS):
                    acc = acc + (w_t[SM_WI + hh:SM_WI + hh + 1, :]
                                 * jnp.maximum(d[:, hh * pw:(hh + 1) * pw], 0.0))
                if masked:
                    acc = jnp.where(pos_at(start, w, pw) < lim, acc, -jnp.inf)
                bits = pltpu.bitcast(acc, jnp.int32)
                key_ref[keys_at(start, w), part] = jnp.where(bits < 0, bits ^ 0x7FFFFFFF, bits)

            lax.fori_loop(0, nfull, lambda c, carry, score=score: (score(c * kc, kc, not trim), carry)[1], 0)
            if trim:
                score(q0, diag[g], True)

    all_visible = (not trim) and past >= (nkc - 1) * kc

    def count_ge(cands):
        def count(keys, w, accs, which=groups):
            accs = list(accs)
            for g in which:
                hit = jnp.where(key_ref[keys, rows_of(g)] >= cands[g], 1.0, 0.0)
                accs[g] = accs[g] + jnp.sum(hit.reshape(w // cnt_rows, cnt_rows, tq), axis=0)
            return tuple(accs)

        cnt_rows = 8 * max(1, CNT_VREGS // rg)
        accs = tuple(jnp.zeros((cnt_rows, tq), jnp.float32) for _ in groups)
        if all_visible:
            accs = count(slice(0, lp), lp, accs)
        else:
            accs = lax.fori_loop(0, nfull, lambda c, a: count(keys_at(c * kc, kc), kc, a), accs)
            if trim:
                for g in groups:
                    accs = count(keys_at(q0, diag[g]), diag[g], accs, which=[g])
        return [jnp.sum(a, axis=0, keepdims=True) for a in accs]

    def bit_step(it, taus_u):
        cands_u = [t | lax.shift_left(jnp.int32(1), 31 - it) for t in taus_u]
        cnts = count_ge([c ^ _INT_MIN for c in cands_u])
        return tuple(jnp.where(n >= float(topk), c, t) for n, c, t in zip(cnts, cands_u, taus_u))

    taus_u = lax.fori_loop(0, 32, bit_step, tuple(jnp.zeros((1, tq), jnp.int32) for _ in groups))
    taus = [t ^ _INT_MIN for t in taus_u]
    cnts_ge = count_ge(taus)
    cnts_gt = count_ge([t + 1 for t in taus])
    needs = [float(topk) - n for n in cnts_gt]
    any_excess = jnp.int32(0)
    for g in groups:
        excess = (cnts_ge[g] > float(topk)) & (taus[g] > _KEY_NEG_INF)
        any_excess = jnp.maximum(any_excess, jnp.max(jnp.where(excess, 1, 0)))

    ea = lax.broadcasted_iota(jnp.int32, (tq, tq), 0)
    eb = lax.broadcasted_iota(jnp.int32, (tq, tq), 1)
    eye = jnp.where(ea == eb, 1.0, 0.0).astype(jnp.bfloat16)

    def store_mask(g, start, w, sel_t):
        sel = _dot_nt(eye, jnp.where(sel_t, 1.0, 0.0).astype(jnp.bfloat16))
        m_ref[rows_of(g), keys_at(start, w)] = _bf(jnp.where(sel > 0.5, 0.0, -jnp.inf))

    @pl.when(any_excess == 0)
    def _():
        def span(g, start, w, masked):
            sel_t = key_ref[keys_at(start, w), rows_of(g)] >= taus[g]
            if masked:
                sel_t = sel_t & (pos_at(start, w) < lims[g])
            store_mask(g, start, w, sel_t)

        def body(c, carry):
            for g in groups:
                span(g, c * kc, kc, not trim)
            return carry

        lax.fori_loop(0, nfull, body, 0)
        if trim:
            for g in groups:
                span(g, q0, diag[g], True)

    @pl.when(any_excess != 0)
    def _():
        la = lax.broadcasted_iota(jnp.int32, (LANES, LANES), 0)
        lb = lax.broadcasted_iota(jnp.int32, (LANES, LANES), 1)
        lower = jnp.where(la >= lb, 1.0, 0.0).astype(jnp.bfloat16)

        def block(g, start, carry):
            key = key_ref[keys_at(start, LANES), rows_of(g)]
            eq = key == taus[g]
            pref = _dot(lower, jnp.where(eq, 1.0, 0.0).astype(jnp.bfloat16)) + carry
            sel_t = ((key > taus[g]) | (eq & (pref <= needs[g]))) & (pos_at(start, LANES) < lims[g])
            store_mask(g, start, LANES, sel_t)
            return pref[LANES - 1:LANES, :]

        def body(c, carries):
            carries = list(carries)
            for j in range(nl):
                for g in groups:
                    carries[g] = block(g, c * kc + j * LANES, carries[g])
            return tuple(carries)

        carries = lax.fori_loop(0, nfull, body, tuple(jnp.zeros((1, tq), jnp.float32) for _ in groups))
        if trim:
            for g in groups:
                carry = carries[g]
                for j in range(diag[g] // LANES):
                    carry = block(g, q0 + j * LANES, carry)

    if trim:
        for g in groups:
            if diag[g] < kc:
                m_ref[rows_of(g), keys_at(q0 + diag[g], kc - diag[g])] = jnp.full(
                    (tq, kc - diag[g]), -jnp.inf, jnp.bfloat16)

    def fill(c, carry):
        m_ref[:, keys_at(c * kc, kc)] = jnp.full((rg * tq, kc), -jnp.inf, jnp.bfloat16)
        return carry

    lax.fori_loop(nfull + 1 if trim else nvis, nkc, fill, 0)


def _sel_call(proj, *, nbatch, t, past_ki=None):
    past = 0 if past_ki is None else past_ki[0].shape[2]
    _, kc, lp, topk = _attn_geometry(t, past)
    npart = LANES // t if (t < LANES and nbatch % (LANES // t) == 0) else 1
    tq = min(LANES, t * npart)
    rg = min(4, (t * npart) // tq)
    rows = rg * tq
    nr = (t * npart) // rows
    qiw = IDX_HEADS * IDX_DIM
    in_specs = [pl.BlockSpec((rows, qiw), lambda b, i: (b * nr + i, OFF_QI // qiw)),
                pl.BlockSpec((rows, LANES), lambda b, i: (b * nr + i, OFF_SM // LANES)),
                pl.BlockSpec((npart * t, LANES), lambda b, i: (b, OFF_SM // LANES))]
    args = [proj, proj, proj]
    if past:
        ki_all, layer = past_ki
        in_specs.append(pl.BlockSpec((1, npart, past, IDX_DIM), lambda b, i: (layer, b, 0, 0)))
        args.append(ki_all)
    return pl.pallas_call(
        functools.partial(_sel_kernel, tq=tq, rg=rg, npart=npart, t_cur=t, past=past, lp=lp, kc=kc, topk=topk),
        grid=(nbatch // npart, nr),
        in_specs=in_specs,
        out_specs=pl.BlockSpec((rows, lp), lambda b, i: (b * nr + i, 0)),
        out_shape=jax.ShapeDtypeStruct((nbatch * t, lp), jnp.bfloat16),
        scratch_shapes=[pltpu.VMEM((npart, lp, IDX_DIM), jnp.bfloat16),
                        pltpu.VMEM((lp, rows), jnp.int32),
                        pltpu.VMEM((IDX_HEADS * tq, IDX_DIM), jnp.bfloat16)],
        compiler_params=_cparams(("arbitrary", "arbitrary")),
        name="sel_past" if past else "sel",
    )(*args)


def _attn_geometry(t, past):
    tq = min(128, t)
    nreal = past + t
    kc = 512 if nreal % 512 == 0 else 384
    if nreal < kc:
        kc = LANES * (-(-nreal // LANES))
    lp = kc * (-(-nreal // kc))
    assert past % LANES == 0 and past + LANES * (-(-t // LANES)) <= lp
    return tq, kc, lp, min(TOPK_MAX, nreal // 4)


def _attn_kernel(*refs, tq, t_cur, past, lp, kc):
    if past:
        (qb_ref, zb_ref, madd_ref, k_ref, v_ref, kp_ref, vp_ref, na_ref, nb_ref,
         o_ref, ko_ref, vo_ref, kbf, vbf, lg_ref, qs_ref, acc_ref, den_ref, mb_ref) = refs
    else:
        (qb_ref, zb_ref, madd_ref, k_ref, v_ref, na_ref, nb_ref,
         o_ref, ko_ref, vo_ref, kbf, vbf, lg_ref, qs_ref, acc_ref, den_ref, mb_ref) = refs
    i = pl.program_id(1)
    nreal = past + t_cur
    G = ATT_GROUP
    gt = G * tq

    @pl.when(i == 0)
    def _():
        for n in range(KV_HEADS):
            ncol = slice(n * HEAD_DIM, (n + 1) * HEAD_DIM)
            if past:
                kbf[0:past, ncol] = _bf(kp_ref[pl.ds(n, past, stride=KV_HEADS), :])
                vbf[0:past, ncol] = _bf(vp_ref[pl.ds(n, past, stride=KV_HEADS), :])
            ko_ref[pl.ds(n, t_cur, stride=KV_HEADS), :] = k_ref[:, ncol]
            vo_ref[pl.ds(n, t_cur, stride=KV_HEADS), :] = v_ref[:, ncol]
        kbf[past:nreal, :] = _bf(k_ref[...])
        vbf[past:nreal, :] = _bf(v_ref[...])
        if lp > nreal:
            kbf[nreal:lp, :] = jnp.zeros((lp - nreal, KV_HEADS * HEAD_DIM), jnp.bfloat16)
            vbf[nreal:lp, :] = jnp.zeros((lp - nreal, KV_HEADS * HEAD_DIM), jnp.bfloat16)

    q0 = past + i * tq

    far_end = jnp.maximum(q0 - LANES, 0)
    nfull = far_end // kc
    nleft = (far_end - nfull * kc) // LANES
    tail0 = pl.multiple_of(far_end, LANES)
    tw = 2 * LANES
    first = q0 == 0

    def keys_at(off, w):
        return pl.ds(pl.multiple_of(off, LANES), w)

    scale = HEAD_DIM ** -0.5
    for hd in range(ATT_HEADS):
        qs_ref[hd * tq:(hd + 1) * tq, :] = _bf(qb_ref[:, hd * HEAD_DIM:(hd + 1) * HEAD_DIM] * scale)

    for n in range(KV_HEADS):
        ncol = slice(n * HEAD_DIM, (n + 1) * HEAD_DIM)
        grows = slice(n * gt, (n + 1) * gt)

        def logits(off, w, bias=None):
            sc = _dot_nt(qs_ref[grows, :], kbf[keys_at(off, w), ncol])
            ma = madd_ref[:, keys_at(off, w)].astype(jnp.float32)
            for g in range(G):
                r = slice(g * tq, (g + 1) * tq)
                s = sc[r] + ma
                if bias is not None:
                    s = s + bias(n * G + g)
                lg_ref[r, keys_at(off, w)] = s
                mt = mb_ref[r, :]
                for j in range(w // LANES):
                    mt = jnp.maximum(mt, s[:, j * LANES:(j + 1) * LANES])
                mb_ref[r, :] = mt

        def tail_bias(hd):
            zero = jnp.zeros((tq, LANES), jnp.float32)
            return jnp.concatenate([jnp.where(first, nb_ref[hd], na_ref[hd]),
                                    jnp.where(first, zero, nb_ref[hd])], axis=1)

        def weighted_values(off, w):
            p = jnp.exp(lg_ref[:, keys_at(off, w)] - jnp.concatenate([mb_ref[...]] * (w // LANES), axis=1))
            den = den_ref[...]
            for j in range(w // LANES):
                den = den + p[:, j * LANES:(j + 1) * LANES]
            den_ref[...] = den
            acc_ref[...] = acc_ref[...] + _dot(_bf(p), vbf[keys_at(off, w), ncol])

        def walk(fn, tail_kwargs):
            lax.fori_loop(0, nfull, lambda c, carry: (fn(c * kc, kc), carry)[1], 0)
            lax.fori_loop(0, nleft, lambda b, carry: (fn(nfull * kc + b * LANES, LANES), carry)[1], 0)
            fn(tail0, tw, **tail_kwargs)

        mb_ref[...] = jnp.full((gt, LANES), -jnp.inf, jnp.float32)
        walk(logits, dict(bias=tail_bias))
        for g in range(G):
            r = slice(g * tq, (g + 1) * tq)
            mb_ref[r, :] = jnp.broadcast_to(jnp.max(mb_ref[r, :], axis=1, keepdims=True), (tq, LANES))

        acc_ref[...] = jnp.zeros((gt, HEAD_DIM), jnp.float32)
        den_ref[...] = jnp.zeros((gt, LANES), jnp.float32)
        walk(weighted_values, {})

        for g in range(G):
            r = slice(g * tq, (g + 1) * tq)
            hcol = slice((n * G + g) * HEAD_DIM, (n * G + g + 1) * HEAD_DIM)
            den = jnp.sum(den_ref[r, :], axis=1, keepdims=True)
            z = zb_ref[:, hcol]
            o_ref[:, hcol] = _bf((acc_ref[r, :] / den) * _silu(z))


def _attn_call(proj, madd, near_a, near_b, *, nbatch, t, past_kv=None):
    past = 0 if past_kv is None else past_kv[0].shape[2] // KV_HEADS
    tq, kc, lp, _ = _attn_geometry(t, past)
    nq = t // tq
    kvw = KV_HEADS * HEAD_DIM
    gt = ATT_GROUP * tq

    in_specs = [pl.BlockSpec((tq, D_MODEL), lambda b, i: (b * nq + i, OFF_QB // D_MODEL)),
                pl.BlockSpec((tq, D_MODEL), lambda b, i: (b * nq + i, OFF_ZB // D_MODEL)),
                pl.BlockSpec((tq, lp), lambda b, i: (b * nq + i, 0)),
                pl.BlockSpec((t, kvw), lambda b, i: (b, OFF_KB // kvw)),
                pl.BlockSpec((t, kvw), lambda b, i: (b, OFF_VB // kvw))]
    args = [proj, proj, madd, proj, proj]
    if past:
        k_all, v_all, layer = past_kv
        in_specs += [pl.BlockSpec((None, None, past * KV_HEADS, HEAD_DIM), lambda b, i: (layer, b, 0, 0)),
                     pl.BlockSpec((None, None, past * KV_HEADS, HEAD_DIM), lambda b, i: (layer, b, 0, 0))]
        args += [k_all, v_all]
    in_specs += [pl.BlockSpec((ATT_HEADS, tq, LANES), lambda b, i: (0, 0, 0)),
                 pl.BlockSpec((ATT_HEADS, tq, LANES), lambda b, i: (0, 0, 0))]
    args += [near_a, near_b]

    return pl.pallas_call(
        functools.partial(_attn_kernel, tq=tq, t_cur=t, past=past, lp=lp, kc=kc),
        grid=(nbatch, nq),
        in_specs=in_specs,
        out_specs=[pl.BlockSpec((tq, D_MODEL), lambda b, i: (b * nq + i, 0)),
                   pl.BlockSpec((None, t * KV_HEADS, HEAD_DIM), lambda b, i: (b, 0, 0)),
                   pl.BlockSpec((None, t * KV_HEADS, HEAD_DIM), lambda b, i: (b, 0, 0))],
        out_shape=[jax.ShapeDtypeStruct((nbatch * t, D_MODEL), jnp.bfloat16),
                   jax.ShapeDtypeStruct((nbatch, t * KV_HEADS, HEAD_DIM), jnp.float32),
                   jax.ShapeDtypeStruct((nbatch, t * KV_HEADS, HEAD_DIM), jnp.float32)],
        scratch_shapes=[pltpu.VMEM((lp, kvw), jnp.bfloat16),
                        pltpu.VMEM((lp, kvw), jnp.bfloat16),
                        pltpu.VMEM((gt, lp), jnp.float32),
                        pltpu.VMEM((ATT_HEADS * tq, HEAD_DIM), jnp.bfloat16),
                        pltpu.VMEM((gt, HEAD_DIM), jnp.float32),
                        pltpu.VMEM((gt, LANES), jnp.float32),
                        pltpu.VMEM((gt, LANES), jnp.float32)],
        compiler_params=_cparams(("arbitrary", "arbitrary")),
        name="attn_past" if past else "attn",
    )(*args)


def _merge_kernel(oa_ref, ob_ref, wa_ref, wb_ref, ga_ref, gb_ref, o_ref):
    ya = _dot(oa_ref[...], wa_ref[...])
    yb = _dot(ob_ref[...], wb_ref[...])
    o_ref[...] = _bf(jax.nn.sigmoid(ga_ref[...]) * ya + jax.nn.sigmoid(gb_ref[...]) * yb)


def _merge_call(oa, ob, wa_bf, wb_bf, proj):
    m, d = oa.shape
    tm = min(1024, m)
    tn = 512
    return pl.pallas_call(
        _merge_kernel,
        grid=(m // tm, d // tn),
        in_specs=[pl.BlockSpec((tm, d), lambda i, j: (i, 0)),
                  pl.BlockSpec((tm, d), lambda i, j: (i, 0)),
                  pl.BlockSpec((d, tn), lambda i, j: (0, j)),
                  pl.BlockSpec((d, tn), lambda i, j: (0, j)),
                  pl.BlockSpec((tm, tn), lambda i, j: (i, OFF_GLA // tn + j)),
                  pl.BlockSpec((tm, tn), lambda i, j: (i, OFF_GLB // tn + j))],
        out_specs=pl.BlockSpec((tm, tn), lambda i, j: (i, j)),
        out_shape=jax.ShapeDtypeStruct((m, d), jnp.bfloat16),
        compiler_params=_cparams(("arbitrary", "arbitrary")),
        name="merge",
    )(oa, ob, wa_bf, wb_bf, proj, proj)


def _outproj_kernel(*refs, tm, final_norm):
    if final_norm:
        mg_ref, w_ref, x_ref, gate_ref, nw_ref, o_ref = refs
    else:
        mg_ref, w_ref, x_ref, gate_ref, o_ref = refs
    y = _dot(mg_ref[...], w_ref[...])
    for g in range(tm // GROUP):
        r = slice(g * GROUP, (g + 1) * GROUP)
        xn = x_ref[r, :] + gate_ref[g:g + 1, :] * y[r]
        if final_norm:
            xn = xn * lax.rsqrt(jnp.mean(xn * xn, axis=-1, keepdims=True) + EPS) * nw_ref[...]
        o_ref[r, :] = xn


def _outproj_call(merged, wo_bf, x, gate_g, final_norm_w=None):
    m, d = x.shape
    tm = min(512, m)
    gpt = tm // GROUP
    in_specs = [pl.BlockSpec((tm, d), lambda i: (i, 0)),
                pl.BlockSpec((d, d), lambda i: (0, 0)),
                pl.BlockSpec((tm, d), lambda i: (i, 0)),
                pl.BlockSpec((gpt, d), lambda i: (i, 0))]
    args = [merged, wo_bf, x, gate_g]
    if final_norm_w is not None:
        in_specs.append(pl.BlockSpec((1, d), lambda i: (0, 0)))
        args.append(final_norm_w.reshape(1, d))
    return pl.pallas_call(
        functools.partial(_outproj_kernel, tm=tm, final_norm=final_norm_w is not None),
        grid=(m // tm,),
        in_specs=in_specs,
        out_specs=pl.BlockSpec((tm, d), lambda i: (i, 0)),
        out_shape=jax.ShapeDtypeStruct((m, d), jnp.float32),
        compiler_params=_cparams(("arbitrary",)),
        name="outproj",
    )(*args)


def _relayout_w_in(w):
    offs = [0]
    for s in _IN_SIZES:
        offs.append(offs[-1] + s)
    (qkv, za, ba, aa, qb, kb, vb, zb, qi, ki, wi, gla, glb) = [w[:, offs[n]:offs[n + 1]] for n in range(13)]
    d = w.shape[0]
    pad_sm = jnp.zeros((d, LANES - (IDX_DIM + 3 * GDN_HEADS)), w.dtype)
    cols = [qkv, za, qb, zb, gla, glb, qi, kb, vb, ki, ba, aa, wi, pad_sm]
    out = jnp.concatenate(cols, axis=1)
    pad = jnp.zeros((d, N_PROJ - out.shape[1]), w.dtype)
    return jnp.concatenate([out, pad], axis=1).astype(jnp.bfloat16)


def _rel_bucket(rel):
    nb = REL_BUCKETS // 2
    max_exact = nb // 2
    n = jnp.abs(rel)
    nf = jnp.maximum(n, 1).astype(jnp.float32)
    large = max_exact + (jnp.log(nf / max_exact) / math.log(REL_MAX_DIST / max_exact)
                         * (nb - max_exact)).astype(jnp.int32)
    large = jnp.minimum(large, nb - 1)
    return jnp.where(rel > 0, nb, 0) + jnp.where(n < max_exact, n, large)


def _bias_tables(rel_bias):
    tq = LANES
    trow = jnp.arange(tq)[:, None]
    col = jnp.arange(2 * LANES)[None, :]
    rel = (col - LANES) - trow
    bucket = _rel_bucket(rel)
    tab = sum(jnp.where(bucket == b, rel_bias[b][:, None, None], 0.0) for b in range(REL_BUCKETS))
    far = rel_bias[REL_BUCKETS // 2 - 1]
    tab = tab - far[:, None, None]
    return tab[:, :, :LANES], tab[:, :, LANES:]


def _pad_rows8(a):
    z = jnp.zeros(a.shape[:-2] + (8 - a.shape[-2], a.shape[-1]), a.dtype)
    return jnp.concatenate([z, a], axis=-2)


def _lane_row(vals, off):
    r = jnp.zeros((1, LANES), jnp.float32)
    return r.at[0, off:off + vals.shape[0]].set(vals)


def _layer(x, mod, lw, tables, *, nbatch, t, caches=None, final_norm_w=None):
    (norm_w, w_in_bf, wconv8, lrow, drow, gw, wa_bf, wb_bf, wo_bf) = lw
    near_a, near_b = tables
    d = D_MODEL
    gpb = t // GROUP

    def per_group(a):
        return jnp.broadcast_to(a[:, None, :], (nbatch, gpb, d)).reshape(nbatch * gpb, d)

    shift_g, scale_g, gate_g = [per_group(mod[:, n * d:(n + 1) * d]) for n in range(3)]
    if caches is None:
        state = past_ki = past_kv = None
    else:
        layer, k_all, v_all, ki_all, s_all, conv_prev = caches
        state = (_pad_rows8(conv_prev), s_all, layer)
        past_ki = (ki_all, layer)
        past_kv = (k_all, v_all, layer)

    proj = _inproj_call(x, norm_w, scale_g, shift_g, w_in_bf)
    oa, s_new, tails = _gdn_call(proj, wconv8, lrow, drow, gw, nbatch=nbatch, t=t, state=state)
    tq = min(LANES, t)
    madd = _sel_call(proj, nbatch=nbatch, t=t, past_ki=past_ki)
    ob, k_rows, v_rows = _attn_call(proj, madd, near_a[:, :tq], near_b[:, :tq], nbatch=nbatch, t=t,
                                    past_kv=past_kv)
    merged = _merge_call(oa, ob, wa_bf, wb_bf, proj)
    x_new = _outproj_call(merged, wo_bf, x, gate_g, final_norm_w)

    k_new = k_rows.reshape(nbatch, t, KV_HEADS, HEAD_DIM)
    v_new = v_rows.reshape(nbatch, t, KV_HEADS, HEAD_DIM)
    ki_new = proj[:, OFF_SM + SM_KI:OFF_SM + SM_KI + IDX_DIM].reshape(nbatch, t, IDX_DIM)
    conv_new = tails[:, 8 - (CONV_W - 1):, :]
    return x_new, (k_new, v_new, ki_new, s_new, conv_new)


def kernel(x_prompt, x_sample, c_prompt, c_sample, cache_k, cache_v, cache_idx_k, state_gdn, state_conv,
           norm_w, w_ada, b_ada, w_in, w_conv, a_log, dt_bias, gdn_norm_w, w_branch_a, w_branch_b,
           w_out, rel_bias, final_norm_w):
    depth = w_in.shape[0]
    bp, tp, d = x_prompt.shape
    bs, ts, _ = x_sample.shape
    past = cache_k.shape[2]
    kvw = KV_HEADS * HEAD_DIM

    mod = _ada_call(jnp.concatenate([c_prompt, c_sample], axis=0), w_ada, b_ada)
    tables = _bias_tables(rel_bias)

    xp = x_prompt.reshape(bp * tp, d)
    xs = x_sample.reshape(bs * ts, d)
    new_p, new_s = [], []
    for l in range(depth):
        wconv8 = jnp.concatenate([w_conv[l], jnp.zeros((8 - CONV_W, w_conv.shape[2]), w_conv.dtype)], axis=0)
        lw = (norm_w[l], _relayout_w_in(w_in[l]), wconv8,
              _lane_row(a_log[l], SM_AA), _lane_row(dt_bias[l], SM_AA), gdn_norm_w[l].reshape(1, GDN_DV),
              w_branch_a[l].astype(jnp.bfloat16), w_branch_b[l].astype(jnp.bfloat16),
              w_out[l].astype(jnp.bfloat16))
        fnw = final_norm_w if l == depth - 1 else None
        xp, sp = _layer(xp, mod[l, :bp], lw, tables, nbatch=bp, t=tp, final_norm_w=fnw)
        caches = (l, cache_k.reshape(depth, bs, past * KV_HEADS, HEAD_DIM),
                  cache_v.reshape(depth, bs, past * KV_HEADS, HEAD_DIM), cache_idx_k, state_gdn, state_conv[l])
        xs, ss = _layer(xs, mod[l, bp:], lw, tables, nbatch=bs, t=ts, caches=caches, final_norm_w=fnw)
        new_p.append(sp)
        new_s.append(ss)

    y_prompt = xp.reshape(bp, tp, d)
    y_sample = xs.reshape(bs, ts, d)
    outs_p = [jnp.stack([s[n] for s in new_p]) for n in range(5)]
    outs_s = [jnp.stack([s[n] for s in new_s]) for n in range(5)]
    return (y_prompt, y_sample, *outs_p, *outs_s)
```

```python
import functools
import math

import jax
import jax.numpy as jnp
from jax import lax
from jax.experimental import pallas as pl
from jax.experimental.pallas import tpu as pltpu

D_MODEL = 2048
CHUNK = 64
GDN_HEADS = 16
GDN_DK = 128
GDN_DV = 128
CONV_W = 4
ATT_HEADS = 16
KV_HEADS = 2
HEAD_DIM = 128
IDX_HEADS = 16
IDX_DIM = 64
TOPK_MAX = 256
REL_BUCKETS = 32
REL_MAX_DIST = 128
EPS = 1e-6

LANES = 128
VMEM_LIMIT = 56 * 1024 * 1024

W_QKV = 3 * GDN_HEADS * GDN_DK
OFF_QKV = 0
OFF_ZA = OFF_QKV + W_QKV
OFF_QB = OFF_ZA + D_MODEL
OFF_ZB = OFF_QB + D_MODEL
OFF_GLA = OFF_ZB + D_MODEL
OFF_GLB = OFF_GLA + D_MODEL
OFF_QI = OFF_GLB + D_MODEL
OFF_KB = OFF_QI + IDX_HEADS * IDX_DIM
OFF_VB = OFF_KB + KV_HEADS * HEAD_DIM
OFF_SM = OFF_VB + KV_HEADS * HEAD_DIM
SM_KI, SM_BA, SM_AA, SM_WI = 0, 64, 80, 96
N_PROJ = 18432
GROUP = 64

_IN_SIZES = (W_QKV, D_MODEL, GDN_HEADS, GDN_HEADS, D_MODEL, KV_HEADS * HEAD_DIM, KV_HEADS * HEAD_DIM,
             D_MODEL, IDX_HEADS * IDX_DIM, IDX_DIM, IDX_HEADS, D_MODEL, D_MODEL)


def _cparams(sem):
    return pltpu.CompilerParams(dimension_semantics=sem, vmem_limit_bytes=VMEM_LIMIT)


def _bf(x):
    return x.astype(jnp.bfloat16)


def _dot(a, b):
    return jnp.dot(a, b, preferred_element_type=jnp.float32)


def _dot_nt(a, b):
    return lax.dot_general(a, b, (((1,), (1,)), ((), ())), preferred_element_type=jnp.float32)


def _ada_kernel(c_ref, w_ref, b_ref, o_ref):
    c = c_ref[...]
    a = _bf(c * jax.nn.sigmoid(c))
    o_ref[0] = _dot(a, _bf(w_ref[0])) + b_ref[0]


def _ada_call(c_all, w_ada, b_ada):
    depth, d, n = w_ada.shape
    nb = c_all.shape[0]
    tn = 1024
    return pl.pallas_call(
        _ada_kernel,
        grid=(depth, n // tn),
        in_specs=[pl.BlockSpec((nb, d), lambda l, j: (0, 0)),
                  pl.BlockSpec((1, d, tn), lambda l, j: (l, 0, j)),
                  pl.BlockSpec((1, 1, tn), lambda l, j: (l, 0, j))],
        out_specs=pl.BlockSpec((1, nb, tn), lambda l, j: (l, 0, j)),
        out_shape=jax.ShapeDtypeStruct((depth, nb, n), jnp.float32),
        compiler_params=_cparams(("arbitrary", "arbitrary")),
        name="ada_mod",
    )(c_all, w_ada, b_ada.reshape(depth, 1, n))


def _inproj_kernel(x_ref, nw_ref, sc_ref, sh_ref, w_ref, o_ref, h_ref, *, tm):
    @pl.when(pl.program_id(1) == 0)
    def _():
        nw = nw_ref[...]

        def body(g, carry):
            rows = pl.ds(pl.multiple_of(g * GROUP, GROUP), GROUP)
            x = x_ref[rows, :]
            y = x * lax.rsqrt(jnp.mean(x * x, axis=-1, keepdims=True) + EPS) * nw
            hh = y * (1.0 + sc_ref[pl.ds(g, 1), :]) + sh_ref[pl.ds(g, 1), :]
            h_ref[rows, :] = _bf(hh)
            return carry

        lax.fori_loop(0, tm // GROUP, body, 0)

    o_ref[...] = _dot(h_ref[...], w_ref[...])


def _inproj_call(x, norm_w, scale_g, shift_g, w_bf):
    m, d = x.shape
    n = w_bf.shape[1]
    tm = min(1024, m)
    tn = 1024
    gpt = tm // GROUP
    return pl.pallas_call(
        functools.partial(_inproj_kernel, tm=tm),
        grid=(m // tm, n // tn),
        in_specs=[pl.BlockSpec((tm, d), lambda i, j: (i, 0)),
                  pl.BlockSpec((1, d), lambda i, j: (0, 0)),
                  pl.BlockSpec((gpt, d), lambda i, j: (i, 0)),
                  pl.BlockSpec((gpt, d), lambda i, j: (i, 0)),
                  pl.BlockSpec((d, tn), lambda i, j: (0, j))],
        out_specs=pl.BlockSpec((tm, tn), lambda i, j: (i, j)),
        out_shape=jax.ShapeDtypeStruct((m, n), jnp.float32),
        scratch_shapes=[pltpu.VMEM((tm, d), jnp.bfloat16)],
        compiler_params=_cparams(("arbitrary", "arbitrary")),
        name="inproj",
    )(x, norm_w.reshape(1, d), scale_g, shift_g, w_bf)


def _silu(x):
    hx = 0.5 * x
    return hx * jnp.tanh(hx) + hx


def _l2norm(x):
    return x * lax.rsqrt(jnp.sum(x * x, axis=-1, keepdims=True) + EPS)


def _softplus(x):
    return jnp.maximum(x, 0.0) + jnp.log1p(jnp.exp(-jnp.abs(x)))


INV_BASE = 8
HEADS_PER_PASS = GDN_HEADS


def _unit_lower_inverses(As, ii, jj, eye):
    C = As[0].shape[0]
    sh = INV_BASE.bit_length() - 1
    Ns = [jnp.where((ii >> sh) == (jj >> sh), -A, 0.0) for A in As]
    Ps = [eye + N for N in Ns]
    m = 2
    while m < INV_BASE:
        Nbs = [_bf(N) for N in Ns]
        Ns = [_dot(Nb, Nb) for Nb in Nbs]
        Ps = [P + _dot(_bf(P), _bf(N)) for P, N in zip(Ps, Ns)]
        m *= 2
    s = INV_BASE
    while s < C:
        sh = s.bit_length() - 1
        off = ((ii >> (sh + 1)) == (jj >> (sh + 1))) & (((ii >> sh) & 1) == 1) & (((jj >> sh) & 1) == 0)
        Pbs = [_bf(P) for P in Ps]
        Xs = [_dot(Pb, _bf(jnp.where(off, A, 0.0))) for Pb, A in zip(Pbs, As)]
        Ps = [P - _dot(_bf(X), Pb) for P, X, Pb in zip(Ps, Xs, Pbs)]
        s *= 2
    return Ps


MERGE_CHUNKS = 8
MERGE_COLS = 256


def _gdn_kernel(*refs, has_state, merge=None):
    if merge is not None:
        (qkv_ref, z_ref, sm_ref, wc_ref, lrow_ref, drow_ref, gw_ref, ob_ref, wa_ref, wb_ref, ga_ref, gb_ref,
         mg_ref, sout_ref, tout_ref, S_ref, xe_ref, oa_ref) = refs
        nt, nsteps = merge
        step = pl.program_id(0)
        i = step % nt
        last = (i == nt - 1) & (step < nsteps)
    elif has_state:
        (qkv_ref, z_ref, sm_ref, wc_ref, lrow_ref, drow_ref, gw_ref, cp_ref, s0_ref,
         o_ref, sout_ref, tout_ref, S_ref, xe_ref) = refs
    else:
        (qkv_ref, z_ref, sm_ref, wc_ref, lrow_ref, drow_ref, gw_ref,
         o_ref, sout_ref, tout_ref, S_ref, xe_ref) = refs
    if merge is None:
        i = pl.program_id(1)
        last = i == pl.num_programs(1) - 1
    C = CHUNK
    hw = GDN_HEADS * GDN_DK

    if merge is not None:
        @pl.when(step == 0)
        def _():
            oa_ref[...] = jnp.zeros(oa_ref.shape, jnp.bfloat16)

        group = step // MERGE_CHUNKS
        prev = (group + 1) % 2
        ysum = [None, None]

        def merge_piece(kt):
            ks = slice(kt * MERGE_COLS, (kt + 1) * MERGE_COLS)
            for n, (lhs, w_ref) in enumerate(((oa_ref[prev, :, ks], wa_ref), (ob_ref[:, ks], wb_ref))):
                part = _dot(lhs, w_ref[ks, :])
                ysum[n] = part if ysum[n] is None else ysum[n] + part

        pieces = [functools.partial(merge_piece, kt) for kt in range(D_MODEL // MERGE_COLS)]
        o_rows = pl.ds(pl.multiple_of((step % MERGE_CHUNKS) * C, C), C)

        def store_o(cols, val):
            oa_ref[group % 2, o_rows, cols] = val
    else:
        pieces = []

        def store_o(cols, val):
            o_ref[:, cols] = val

    def filler():
        if pieces:
            pieces.pop(0)()

    @pl.when(i == 0)
    def _():
        if has_state:
            S_ref[...] = s0_ref[0, 0]
            xe_ref[0:8, :] = cp_ref[0]
        else:
            S_ref[...] = jnp.zeros(S_ref.shape, jnp.float32)
            xe_ref[0:8, :] = jnp.zeros((8, xe_ref.shape[1]), jnp.float32)

    @pl.when(i > 0)
    def _():
        xe_ref[0:8, :] = xe_ref[C:C + 8, :]

    xe_ref[8:8 + C, :] = qkv_ref[...]

    sm = sm_ref[...]
    beta_all = jax.nn.sigmoid(sm)
    g_all = -jnp.exp(lrow_ref[...]) * _softplus(sm + drow_ref[...])
    rowc = lax.broadcasted_iota(jnp.int32, (C, LANES), 0)
    gc = g_all
    s = 1
    while s < C:
        gc = gc + jnp.where(rowc >= s, pltpu.roll(gc, s, 0), 0.0)
        s *= 2
    glast = gc[C - 1:C, :]
    egc_all = jnp.exp(gc)
    ekd_all = jnp.exp(glast - gc)
    egl_all = jnp.exp(glast)
    gc_t = gc.T

    ii = lax.broadcasted_iota(jnp.int32, (C, C), 0)
    jj = lax.broadcasted_iota(jnp.int32, (C, C), 1)
    eye = jnp.where(ii == jj, 1.0, 0.0)
    gw = gw_ref[...]


    def col(p, h):
        return slice(p * hw + h * GDN_DK, p * hw + (h + 1) * GDN_DK)

    def conv(p, h):
        w = wc_ref[:, col(p, h)]
        y = xe_ref[8:8 + C, col(p, h)] * w[CONV_W - 1:CONV_W]
        for s in range(1, CONV_W):
            y = y + xe_ref[8 - s:8 - s + C, col(p, h)] * w[CONV_W - 1 - s:CONV_W - s]
        return _silu(y)

    def lane(a, l):
        return a[:, l:l + 1]

    def run(heads):
        ks, qs = [], []
        for n, h in enumerate(heads):
            ks.append(_l2norm(conv(1, h)))
            if n % 4 == 3:
                filler()
        for n, h in enumerate(heads):
            qs.append(_l2norm(conv(0, h)) * (GDN_DK ** -0.5))
            if n % 4 == 3:
                filler()
        kbs = [k * lane(beta_all, SM_BA + h) for h, k in zip(heads, ks)]
        kqs = [_dot_nt(_bf(jnp.concatenate([kb, q], axis=0)), _bf(k)) for kb, q, k in zip(kbs, qs, ks)]
        decays = [jnp.where(ii >= jj,
                            jnp.exp(jnp.minimum(lane(gc, SM_AA + h) - gc_t[SM_AA + h:SM_AA + h + 1, :], 0.0)),
                            0.0)
                  for h in heads]
        As = [jnp.where(ii > jj, kq[:C] * d, 0.0) for kq, d in zip(kqs, decays)]
        qks = [_bf(kq[C:] * d) for kq, d in zip(kqs, decays)]
        Ps = _unit_lower_inverses(As, ii, jj, eye)
        vs = [conv(2, h) for h in heads]
        rhs = [_bf(jnp.concatenate([v * lane(beta_all, SM_BA + h), kb * lane(egc_all, SM_AA + h)], axis=1))
               for h, v, kb in zip(heads, vs, kbs)]
        uws = [_dot(_bf(P), r) for P, r in zip(Ps, rhs)]
        Ss = [S_ref[h] for h in heads]
        wqs = [_dot(_bf(jnp.concatenate([uw[:, GDN_DV:], q * lane(egc_all, SM_AA + h)], axis=0)), _bf(S))
               for h, uw, q, S in zip(heads, uws, qs, Ss)]
        vnbs = [_bf(uw[:, :GDN_DV] - wq[:C]) for uw, wq in zip(uws, wqs)]
        kdts = [_bf((k * lane(ekd_all, SM_AA + h)).T) for h, k in zip(heads, ks)]
        for h, S, kdt, vnb in zip(heads, Ss, kdts, vnbs):
            S_ref[h] = S * lane(egl_all, SM_AA + h) + _dot(kdt, vnb)
        os_ = [wq[C:] + _dot(qk, vnb) for wq, qk, vnb in zip(wqs, qks, vnbs)]
        for h, o in zip(heads, os_):
            o = o * lax.rsqrt(jnp.mean(o * o, axis=-1, keepdims=True) + EPS) * gw
            z = z_ref[:, col(0, h)]
            store_o(col(0, h), _bf(o * _silu(z)))

    for h0 in range(0, GDN_HEADS, HEADS_PER_PASS):
        run(range(h0, h0 + HEADS_PER_PASS))

    if merge is not None:
        while pieces:
            filler()
        mg_ref[...] = _bf(jax.nn.sigmoid(ga_ref[...]) * ysum[0] + jax.nn.sigmoid(gb_ref[...]) * ysum[1])

    @pl.when(last)
    def _():
        sout_ref[0] = S_ref[...]
        tout_ref[0] = xe_ref[C:C + 8, :]


def _gdn_call(proj, wconv8, lrow, drow, gw, *, nbatch, t, state=None):
    nt = t // CHUNK
    hh = GDN_HEADS
    const = lambda b, i: (0, 0)
    state_spec = pl.BlockSpec((1, hh, GDN_DK, GDN_DV), lambda b, i: (b, 0, 0, 0))
    tail_spec = pl.BlockSpec((1, 8, W_QKV), lambda b, i: (b, 0, 0))
    in_specs = [pl.BlockSpec((CHUNK, W_QKV), lambda b, i: (b * nt + i, OFF_QKV // W_QKV)),
                pl.BlockSpec((CHUNK, D_MODEL), lambda b, i: (b * nt + i, OFF_ZA // D_MODEL)),
                pl.BlockSpec((CHUNK, LANES), lambda b, i: (b * nt + i, OFF_SM // LANES)),
                pl.BlockSpec((8, W_QKV), const),
                pl.BlockSpec((1, LANES), const),
                pl.BlockSpec((1, LANES), const),
                pl.BlockSpec((1, GDN_DV), const)]
    args = [proj, proj, proj, wconv8, lrow, drow, gw]
    if state is not None:
        conv_prev8, s_all, layer = state
        in_specs += [tail_spec,
                     pl.BlockSpec((1, 1, hh, GDN_DK, GDN_DV), lambda b, i: (layer, b, 0, 0, 0))]
        args += [conv_prev8, s_all]
    return pl.pallas_call(
        functools.partial(_gdn_kernel, has_state=state is not None),
        grid=(nbatch, nt),
        in_specs=in_specs,
        out_specs=[pl.BlockSpec((CHUNK, hh * GDN_DV), lambda b, i: (b * nt + i, 0)),
                   state_spec, tail_spec],
        out_shape=[jax.ShapeDtypeStruct((nbatch * t, hh * GDN_DV), jnp.bfloat16),
                   jax.ShapeDtypeStruct((nbatch, hh, GDN_DK, GDN_DV), jnp.float32),
                   jax.ShapeDtypeStruct((nbatch, 8, W_QKV), jnp.float32)],
        scratch_shapes=[pltpu.VMEM((hh, GDN_DK, GDN_DV), jnp.float32),
                        pltpu.VMEM((8 + CHUNK, W_QKV), jnp.float32)],
        compiler_params=_cparams(("arbitrary", "arbitrary")),
        name="gdn",
    )(*args)


def _gdn_merge_call(proj, ob, wa_bf, wb_bf, wconv8, lrow, drow, gw, *, nbatch, t):
    nt = t // CHUNK
    hh = GDN_HEADS
    d = D_MODEL
    nsteps = nbatch * nt
    rows = MERGE_CHUNKS * CHUNK
    assert nt % MERGE_CHUNKS == 0 and d == MERGE_CHUNKS * MERGE_COLS

    def chunk(s):
        return jnp.minimum(s, nsteps - 1)

    def group(s):
        return jnp.maximum(s // MERGE_CHUNKS - 1, 0)

    const = lambda s: (0, 0)
    gate_spec = lambda off: pl.BlockSpec(
        (rows, MERGE_COLS), lambda s: (group(s), off // MERGE_COLS + s % MERGE_CHUNKS))
    w_spec = pl.BlockSpec((d, MERGE_COLS), lambda s: (0, s % MERGE_CHUNKS))
    in_specs = [pl.BlockSpec((CHUNK, W_QKV), lambda s: (chunk(s), OFF_QKV // W_QKV)),
                pl.BlockSpec((CHUNK, D_MODEL), lambda s: (chunk(s), OFF_ZA // D_MODEL)),
                pl.BlockSpec((CHUNK, LANES), lambda s: (chunk(s), OFF_SM // LANES)),
                pl.BlockSpec((8, W_QKV), const),
                pl.BlockSpec((1, LANES), const),
                pl.BlockSpec((1, LANES), const),
                pl.BlockSpec((1, GDN_DV), const),
                pl.BlockSpec((rows, d), lambda s: (group(s), 0)),
                w_spec, w_spec, gate_spec(OFF_GLA), gate_spec(OFF_GLB)]
    return pl.pallas_call(
        functools.partial(_gdn_kernel, has_state=False, merge=(nt, nsteps)),
        grid=(nsteps + MERGE_CHUNKS,),
        in_specs=in_specs,
        out_specs=[pl.BlockSpec((rows, MERGE_COLS), lambda s: (group(s), s % MERGE_CHUNKS)),
                   pl.BlockSpec((1, hh, GDN_DK, GDN_DV), lambda s: (chunk(s) // nt, 0, 0, 0)),
                   pl.BlockSpec((1, 8, W_QKV), lambda s: (chunk(s) // nt, 0, 0))],
        out_shape=[jax.ShapeDtypeStruct((nbatch * t, d), jnp.bfloat16),
                   jax.ShapeDtypeStruct((nbatch, hh, GDN_DK, GDN_DV), jnp.float32),
                   jax.ShapeDtypeStruct((nbatch, 8, W_QKV), jnp.float32)],
        scratch_shapes=[pltpu.VMEM((hh, GDN_DK, GDN_DV), jnp.float32),
                        pltpu.VMEM((8 + CHUNK, W_QKV), jnp.float32),
                        pltpu.VMEM((2, rows, d), jnp.bfloat16)],
        compiler_params=_cparams(("arbitrary",)),
        name="gdn_merge",
    )(proj, proj, proj, wconv8, lrow, drow, gw, ob, wa_bf, wb_bf, proj, proj)


_INT_MIN = -2147483648
_KEY_NEG_INF = -2139095041
ATT_GROUP = ATT_HEADS // KV_HEADS
CNT_VREGS = 8


def _visible_chunks(q0, rows, nreal, kc, nkc):
    lim_max = jnp.minimum(((q0 + rows - 1) // CHUNK + 1) * CHUNK, nreal)
    return jnp.minimum((lim_max + kc - 1) // kc, nkc)


def _sel_kernel(*refs, tq, rg, npart, t_cur, past, lp, kc, topk):
    if past:
        qi_ref, smq_ref, smk_ref, kip_ref, m_ref, kibf, key_ref, qis_ref = refs
    else:
        qi_ref, smq_ref, smk_ref, m_ref, kibf, key_ref, qis_ref = refs
    i = pl.program_id(1)
    nreal = past + t_cur
    nkc = lp // kc
    nl = kc // LANES
    groups = range(rg)
    pw = tq // npart

    @pl.when(i == 0)
    def _():
        for p in range(npart):
            if past:
                kibf[p, 0:past, :] = _bf(kip_ref[0, p])
            kibf[p, past:nreal, :] = _bf(smk_ref[p * t_cur:(p + 1) * t_cur, SM_KI:SM_KI + IDX_DIM])
            if lp > nreal:
                kibf[p, nreal:lp, :] = jnp.zeros((lp - nreal, IDX_DIM), jnp.bfloat16)

    q0 = past + i * (rg * pw)
    nvis = _visible_chunks(q0, rg * pw, nreal, kc, nkc)
    tlane = lax.broadcasted_iota(jnp.int32, (1, tq), 1)
    if npart > 1:
        tlane = tlane % pw
    lims = [jnp.minimum(((q0 + g * tq + tlane) // CHUNK + 1) * CHUNK, nreal) for g in groups]

    def rows_of(g):
        return slice(g * tq, (g + 1) * tq)

    def keys_at(start, w):
        return pl.ds(pl.multiple_of(start, LANES), w)

    def pos_at(start, n, width=tq):
        return start + lax.broadcasted_iota(jnp.int32, (n, width), 0)

    trim = npart == 1 and past == 0 and rg * tq == kc
    if trim:
        nfull = i
        diag = [(g + 1) * tq for g in groups]
    else:
        nfull = nvis
        diag = None

    for g in groups:
        for p in range(npart):
            part = slice(g * tq + p * pw, g * tq + (p + 1) * pw)
            w_t = (smq_ref[part, :] * ((IDX_HEADS ** -0.5) * (IDX_DIM ** -0.5))).T
            for hh in range(IDX_HEADS):
                qis_ref[hh * pw:(hh + 1) * pw, :] = _bf(qi_ref[part, hh * IDX_DIM:(hh + 1) * IDX_DIM])
            lim = lims[g][:, p * pw:(p + 1) * pw]

            def score(start, w, masked, p=p, part=part, w_t=w_t, lim=lim):
                d = _dot_nt(kibf[p, keys_at(start, w), :], qis_ref[0:IDX_HEADS * pw, :])
                acc = jnp.zeros((w, pw), jnp.float32)
                for hh in range(IDX_HEADS):
                    acc = acc + (w_t[SM_WI + hh:SM_WI + hh + 1, :]
                                 * jnp.maximum(d[:, hh * pw:(hh + 1) * pw], 0.0))
                if masked:
                    acc = jnp.where(pos_at(start, w, pw) < lim, acc, -jnp.inf)
                bits = pltpu.bitcast(acc, jnp.int32)
                key_ref[keys_at(start, w), part] = jnp.where(bits < 0, bits ^ 0x7FFFFFFF, bits)

            lax.fori_loop(0, nfull, lambda c, carry, score=score: (score(c * kc, kc, not trim), carry)[1], 0)
            if trim:
                score(q0, diag[g], True)

    all_visible = (not trim) and past >= (nkc - 1) * kc

    def count_ge(cands):
        def count(keys, w, accs, which=groups):
            accs = list(accs)
            for g in which:
                hit = jnp.where(key_ref[keys, rows_of(g)] >= cands[g], 1.0, 0.0)
                accs[g] = accs[g] + jnp.sum(hit.reshape(w // cnt_rows, cnt_rows, tq), axis=0)
            return tuple(accs)

        cnt_rows = 8 * max(1, CNT_VREGS // rg)
        accs = tuple(jnp.zeros((cnt_rows, tq), jnp.float32) for _ in groups)
        if all_visible:
            accs = count(slice(0, lp), lp, accs)
        else:
            accs = lax.fori_loop(0, nfull, lambda c, a: count(keys_at(c * kc, kc), kc, a), accs)
            if trim:
                for g in groups:
                    accs = count(keys_at(q0, diag[g]), diag[g], accs, which=[g])
        return [jnp.sum(a, axis=0, keepdims=True) for a in accs]

    def bit_step(it, taus_u):
        cands_u = [t | lax.shift_left(jnp.int32(1), 31 - it) for t in taus_u]
        cnts = count_ge([c ^ _INT_MIN for c in cands_u])
        return tuple(jnp.where(n >= float(topk), c, t) for n, c, t in zip(cnts, cands_u, taus_u))

    taus_u = lax.fori_loop(0, 32, bit_step, tuple(jnp.zeros((1, tq), jnp.int32) for _ in groups))
    taus = [t ^ _INT_MIN for t in taus_u]
    cnts_ge = count_ge(taus)
    cnts_gt = count_ge([t + 1 for t in taus])
    needs = [float(topk) - n for n in cnts_gt]
    any_excess = jnp.int32(0)
    for g in groups:
        excess = (cnts_ge[g] > float(topk)) & (taus[g] > _KEY_NEG_INF)
        any_excess = jnp.maximum(any_excess, jnp.max(jnp.where(excess, 1, 0)))

    ea = lax.broadcasted_iota(jnp.int32, (tq, tq), 0)
    eb = lax.broadcasted_iota(jnp.int32, (tq, tq), 1)
    eye = jnp.where(ea == eb, 1.0, 0.0).astype(jnp.bfloat16)

    def store_mask(g, start, w, sel_t):
        sel = _dot_nt(eye, jnp.where(sel_t, 1.0, 0.0).astype(jnp.bfloat16))
        m_ref[rows_of(g), keys_at(start, w)] = _bf(jnp.where(sel > 0.5, 0.0, -jnp.inf))

    @pl.when(any_excess == 0)
    def _():
        def span(g, start, w, masked):
            sel_t = key_ref[keys_at(start, w), rows_of(g)] >= taus[g]
            if masked:
                sel_t = sel_t & (pos_at(start, w) < lims[g])
            store_mask(g, start, w, sel_t)

        def body(c, carry):
            for g in groups:
                span(g, c * kc, kc, not trim)
            return carry

        lax.fori_loop(0, nfull, body, 0)
        if trim:
            for g in groups:
                span(g, q0, diag[g], True)

    @pl.when(any_excess != 0)
    def _():
        la = lax.broadcasted_iota(jnp.int32, (LANES, LANES), 0)
        lb = lax.broadcasted_iota(jnp.int32, (LANES, LANES), 1)
        lower = jnp.where(la >= lb, 1.0, 0.0).astype(jnp.bfloat16)

        def block(g, start, carry):
            key = key_ref[keys_at(start, LANES), rows_of(g)]
            eq = key == taus[g]
            pref = _dot(lower, jnp.where(eq, 1.0, 0.0).astype(jnp.bfloat16)) + carry
            sel_t = ((key > taus[g]) | (eq & (pref <= needs[g]))) & (pos_at(start, LANES) < lims[g])
            store_mask(g, start, LANES, sel_t)
            return pref[LANES - 1:LANES, :]

        def body(c, carries):
            carries = list(carries)
            for j in range(nl):
                for g in groups:
                    carries[g] = block(g, c * kc + j * LANES, carries[g])
            return tuple(carries)

        carries = lax.fori_loop(0, nfull, body, tuple(jnp.zeros((1, tq), jnp.float32) for _ in groups))
        if trim:
            for g in groups:
                carry = carries[g]
                for j in range(diag[g] // LANES):
                    carry = block(g, q0 + j * LANES, carry)

    if trim:
        for g in groups:
            if diag[g] < kc:
                m_ref[rows_of(g), keys_at(q0 + diag[g], kc - diag[g])] = jnp.full(
                    (tq, kc - diag[g]), -jnp.inf, jnp.bfloat16)

    def fill(c, carry):
        m_ref[:, keys_at(c * kc, kc)] = jnp.full((rg * tq, kc), -jnp.inf, jnp.bfloat16)
        return carry

    lax.fori_loop(nfull + 1 if trim else nvis, nkc, fill, 0)


def _sel_call(proj, *, nbatch, t, past_ki=None):
    past = 0 if past_ki is None else past_ki[0].shape[2]
    _, kc, lp, topk = _attn_geometry(t, past)
    npart = LANES // t if (t < LANES and nbatch % (LANES // t) == 0) else 1
    tq = min(LANES, t * npart)
    rg = min(4, (t * npart) // tq)
    rows = rg * tq
    nr = (t * npart) // rows
    qiw = IDX_HEADS * IDX_DIM
    in_specs = [pl.BlockSpec((rows, qiw), lambda b, i: (b * nr + i, OFF_QI // qiw)),
                pl.BlockSpec((rows, LANES), lambda b, i: (b * nr + i, OFF_SM // LANES)),
                pl.BlockSpec((npart * t, LANES), lambda b, i: (b, OFF_SM // LANES))]
    args = [proj, proj, proj]
    if past:
        ki_all, layer = past_ki
        in_specs.append(pl.BlockSpec((1, npart, past, IDX_DIM), lambda b, i: (layer, b, 0, 0)))
        args.append(ki_all)
    return pl.pallas_call(
        functools.partial(_sel_kernel, tq=tq, rg=rg, npart=npart, t_cur=t, past=past, lp=lp, kc=kc, topk=topk),
        grid=(nbatch // npart, nr),
        in_specs=in_specs,
        out_specs=pl.BlockSpec((rows, lp), lambda b, i: (b * nr + i, 0)),
        out_shape=jax.ShapeDtypeStruct((nbatch * t, lp), jnp.bfloat16),
        scratch_shapes=[pltpu.VMEM((npart, lp, IDX_DIM), jnp.bfloat16),
                        pltpu.VMEM((lp, rows), jnp.int32),
                        pltpu.VMEM((IDX_HEADS * tq, IDX_DIM), jnp.bfloat16)],
        compiler_params=_cparams(("arbitrary", "arbitrary")),
        name="sel_past" if past else "sel",
    )(*args)


def _attn_geometry(t, past):
    tq = min(128, t)
    nreal = past + t
    kc = 512 if nreal % 512 == 0 else 384
    if nreal < kc:
        kc = LANES * (-(-nreal // LANES))
    lp = kc * (-(-nreal // kc))
    assert past % LANES == 0 and past + LANES * (-(-t // LANES)) <= lp
    return tq, kc, lp, min(TOPK_MAX, nreal // 4)


def _attn_kernel(*refs, tq, t_cur, past, lp, kc):
    if past:
        (qb_ref, zb_ref, madd_ref, k_ref, v_ref, kp_ref, vp_ref, na_ref, nb_ref,
         o_ref, ko_ref, vo_ref, kbf, vbf, lg_ref, qs_ref, acc_ref, den_ref, mb_ref) = refs
    else:
        (qb_ref, zb_ref, madd_ref, k_ref, v_ref, na_ref, nb_ref,
         o_ref, ko_ref, vo_ref, kbf, vbf, lg_ref, qs_ref, acc_ref, den_ref, mb_ref) = refs
    i = pl.program_id(1)
    nreal = past + t_cur
    G = ATT_GROUP
    gt = G * tq

    @pl.when(i == 0)
    def _():
        for n in range(KV_HEADS):
            ncol = slice(n * HEAD_DIM, (n + 1) * HEAD_DIM)
            if past:
                kbf[0:past, ncol] = _bf(kp_ref[pl.ds(n, past, stride=KV_HEADS), :])
                vbf[0:past, ncol] = _bf(vp_ref[pl.ds(n, past, stride=KV_HEADS), :])
            ko_ref[pl.ds(n, t_cur, stride=KV_HEADS), :] = k_ref[:, ncol]
            vo_ref[pl.ds(n, t_cur, stride=KV_HEADS), :] = v_ref[:, ncol]
        kbf[past:nreal, :] = _bf(k_ref[...])
        vbf[past:nreal, :] = _bf(v_ref[...])
        if lp > nreal:
            kbf[nreal:lp, :] = jnp.zeros((lp - nreal, KV_HEADS * HEAD_DIM), jnp.bfloat16)
            vbf[nreal:lp, :] = jnp.zeros((lp - nreal, KV_HEADS * HEAD_DIM), jnp.bfloat16)

    q0 = past + i * tq

    far_end = jnp.maximum(q0 - LANES, 0)
    nfull = far_end // kc
    nleft = (far_end - nfull * kc) // LANES
    tail0 = pl.multiple_of(far_end, LANES)
    tw = 2 * LANES
    first = q0 == 0

    def keys_at(off, w):
        return pl.ds(pl.multiple_of(off, LANES), w)

    scale = HEAD_DIM ** -0.5
    for hd in range(ATT_HEADS):
        qs_ref[hd * tq:(hd + 1) * tq, :] = _bf(qb_ref[:, hd * HEAD_DIM:(hd + 1) * HEAD_DIM] * scale)

    for n in range(KV_HEADS):
        ncol = slice(n * HEAD_DIM, (n + 1) * HEAD_DIM)
        grows = slice(n * gt, (n + 1) * gt)

        def logits(off, w, bias=None):
            sc = _dot_nt(qs_ref[grows, :], kbf[keys_at(off, w), ncol])
            ma = madd_ref[:, keys_at(off, w)].astype(jnp.float32)
            for g in range(G):
                r = slice(g * tq, (g + 1) * tq)
                s = sc[r] + ma
                if bias is not None:
                    s = s + bias(n * G + g)
                lg_ref[r, keys_at(off, w)] = s
                mt = mb_ref[r, :]
                for j in range(w // LANES):
                    mt = jnp.maximum(mt, s[:, j * LANES:(j + 1) * LANES])
                mb_ref[r, :] = mt

        def tail_bias(hd):
            zero = jnp.zeros((tq, LANES), jnp.float32)
            return jnp.concatenate([jnp.where(first, nb_ref[hd], na_ref[hd]),
                                    jnp.where(first, zero, nb_ref[hd])], axis=1)

        def weighted_values(off, w):
            p = jnp.exp(lg_ref[:, keys_at(off, w)] - jnp.concatenate([mb_ref[...]] * (w // LANES), axis=1))
            den = den_ref[...]
            for j in range(w // LANES):
                den = den + p[:, j * LANES:(j + 1) * LANES]
            den_ref[...] = den
            acc_ref[...] = acc_ref[...] + _dot(_bf(p), vbf[keys_at(off, w), ncol])

        def walk(fn, tail_kwargs):
            lax.fori_loop(0, nfull, lambda c, carry: (fn(c * kc, kc), carry)[1], 0)
            lax.fori_loop(0, nleft, lambda b, carry: (fn(nfull * kc + b * LANES, LANES), carry)[1], 0)
            fn(tail0, tw, **tail_kwargs)

        mb_ref[...] = jnp.full((gt, LANES), -jnp.inf, jnp.float32)
        walk(logits, dict(bias=tail_bias))
        for g in range(G):
            r = slice(g * tq, (g + 1) * tq)
            mb_ref[r, :] = jnp.broadcast_to(jnp.max(mb_ref[r, :], axis=1, keepdims=True), (tq, LANES))

        acc_ref[...] = jnp.zeros((gt, HEAD_DIM), jnp.float32)
        den_ref[...] = jnp.zeros((gt, LANES), jnp.float32)
        walk(weighted_values, {})

        for g in range(G):
            r = slice(g * tq, (g + 1) * tq)
            hcol = slice((n * G + g) * HEAD_DIM, (n * G + g + 1) * HEAD_DIM)
            den = jnp.sum(den_ref[r, :], axis=1, keepdims=True)
            z = zb_ref[:, hcol]
            o_ref[:, hcol] = _bf((acc_ref[r, :] / den) * _silu(z))


def _attn_call(proj, madd, near_a, near_b, *, nbatch, t, past_kv=None):
    past = 0 if past_kv is None else past_kv[0].shape[2] // KV_HEADS
    tq, kc, lp, _ = _attn_geometry(t, past)
    nq = t // tq
    kvw = KV_HEADS * HEAD_DIM
    gt = ATT_GROUP * tq

    in_specs = [pl.BlockSpec((tq, D_MODEL), lambda b, i: (b * nq + i, OFF_QB // D_MODEL)),
                pl.BlockSpec((tq, D_MODEL), lambda b, i: (b * nq + i, OFF_ZB // D_MODEL)),
                pl.BlockSpec((tq, lp), lambda b, i: (b * nq + i, 0)),
                pl.BlockSpec((t, kvw), lambda b, i: (b, OFF_KB // kvw)),
                pl.BlockSpec((t, kvw), lambda b, i: (b, OFF_VB // kvw))]
    args = [proj, proj, madd, proj, proj]
    if past:
        k_all, v_all, layer = past_kv
        in_specs += [pl.BlockSpec((None, None, past * KV_HEADS, HEAD_DIM), lambda b, i: (layer, b, 0, 0)),
                     pl.BlockSpec((None, None, past * KV_HEADS, HEAD_DIM), lambda b, i: (layer, b, 0, 0))]
        args += [k_all, v_all]
    in_specs += [pl.BlockSpec((ATT_HEADS, tq, LANES), lambda b, i: (0, 0, 0)),
                 pl.BlockSpec((ATT_HEADS, tq, LANES), lambda b, i: (0, 0, 0))]
    args += [near_a, near_b]

    return pl.pallas_call(
        functools.partial(_attn_kernel, tq=tq, t_cur=t, past=past, lp=lp, kc=kc),
        grid=(nbatch, nq),
        in_specs=in_specs,
        out_specs=[pl.BlockSpec((tq, D_MODEL), lambda b, i: (b * nq + i, 0)),
                   pl.BlockSpec((None, t * KV_HEADS, HEAD_DIM), lambda b, i: (b, 0, 0)),
                   pl.BlockSpec((None, t * KV_HEADS, HEAD_DIM), lambda b, i: (b, 0, 0))],
        out_shape=[jax.ShapeDtypeStruct((nbatch * t, D_MODEL), jnp.bfloat16),
                   jax.ShapeDtypeStruct((nbatch, t * KV_HEADS, HEAD_DIM), jnp.float32),
                   jax.ShapeDtypeStruct((nbatch, t * KV_HEADS, HEAD_DIM), jnp.float32)],
        scratch_shapes=[pltpu.VMEM((lp, kvw), jnp.bfloat16),
                        pltpu.VMEM((lp, kvw), jnp.bfloat16),
                        pltpu.VMEM((gt, lp), jnp.float32),
                        pltpu.VMEM((ATT_HEADS * tq, HEAD_DIM), jnp.bfloat16),
                        pltpu.VMEM((gt, HEAD_DIM), jnp.float32),
                        pltpu.VMEM((gt, LANES), jnp.float32),
                        pltpu.VMEM((gt, LANES), jnp.float32)],
        compiler_params=_cparams(("arbitrary", "arbitrary")),
        name="attn_past" if past else "attn",
    )(*args)


def _merge_kernel(oa_ref, ob_ref, wa_ref, wb_ref, ga_ref, gb_ref, o_ref):
    ya = _dot(oa_ref[...], wa_ref[...])
    yb = _dot(ob_ref[...], wb_ref[...])
    o_ref[...] = _bf(jax.nn.sigmoid(ga_ref[...]) * ya + jax.nn.sigmoid(gb_ref[...]) * yb)


def _merge_call(oa, ob, wa_bf, wb_bf, proj):
    m, d = oa.shape
    tm = min(1024, m)
    tn = 512
    return pl.pallas_call(
        _merge_kernel,
        grid=(m // tm, d // tn),
        in_specs=[pl.BlockSpec((tm, d), lambda i, j: (i, 0)),
                  pl.BlockSpec((tm, d), lambda i, j: (i, 0)),
                  pl.BlockSpec((d, tn), lambda i, j: (0, j)),
                  pl.BlockSpec((d, tn), lambda i, j: (0, j)),
                  pl.BlockSpec((tm, tn), lambda i, j: (i, OFF_GLA // tn + j)),
                  pl.BlockSpec((tm, tn), lambda i, j: (i, OFF_GLB // tn + j))],
        out_specs=pl.BlockSpec((tm, tn), lambda i, j: (i, j)),
        out_shape=jax.ShapeDtypeStruct((m, d), jnp.bfloat16),
        compiler_params=_cparams(("arbitrary", "arbitrary")),
        name="merge",
    )(oa, ob, wa_bf, wb_bf, proj, proj)


def _outproj_kernel(*refs, tm, final_norm):
    if final_norm:
        mg_ref, w_ref, x_ref, gate_ref, nw_ref, o_ref = refs
    else:
        mg_ref, w_ref, x_ref, gate_ref, o_ref = refs
    y = _dot(mg_ref[...], w_ref[...])
    for g in range(tm // GROUP):
        r = slice(g * GROUP, (g + 1) * GROUP)
        xn = x_ref[r, :] + gate_ref[g:g + 1, :] * y[r]
        if final_norm:
            xn = xn * lax.rsqrt(jnp.mean(xn * xn, axis=-1, keepdims=True) + EPS) * nw_ref[...]
        o_ref[r, :] = xn


def _outproj_call(merged, wo_bf, x, gate_g, final_norm_w=None):
    m, d = x.shape
    tm = min(512, m)
    gpt = tm // GROUP
    in_specs = [pl.BlockSpec((tm, d), lambda i: (i, 0)),
                pl.BlockSpec((d, d), lambda i: (0, 0)),
                pl.BlockSpec((tm, d), lambda i: (i, 0)),
                pl.BlockSpec((gpt, d), lambda i: (i, 0))]
    args = [merged, wo_bf, x, gate_g]
    if final_norm_w is not None:
        in_specs.append(pl.BlockSpec((1, d), lambda i: (0, 0)))
        args.append(final_norm_w.reshape(1, d))
    return pl.pallas_call(
        functools.partial(_outproj_kernel, tm=tm, final_norm=final_norm_w is not None),
        grid=(m // tm,),
        in_specs=in_specs,
        out_specs=pl.BlockSpec((tm, d), lambda i: (i, 0)),
        out_shape=jax.ShapeDtypeStruct((m, d), jnp.float32),
        compiler_params=_cparams(("arbitrary",)),
        name="outproj",
    )(*args)


def _relayout_w_in(w):
    offs = [0]
    for s in _IN_SIZES:
        offs.append(offs[-1] + s)
    (qkv, za, ba, aa, qb, kb, vb, zb, qi, ki, wi, gla, glb) = [w[:, offs[n]:offs[n + 1]] for n in range(13)]
    d = w.shape[0]
    pad_sm = jnp.zeros((d, LANES - (IDX_DIM + 3 * GDN_HEADS)), w.dtype)
    cols = [qkv, za, qb, zb, gla, glb, qi, kb, vb, ki, ba, aa, wi, pad_sm]
    out = jnp.concatenate(cols, axis=1)
    pad = jnp.zeros((d, N_PROJ - out.shape[1]), w.dtype)
    return jnp.concatenate([out, pad], axis=1).astype(jnp.bfloat16)


def _rel_bucket(rel):
    nb = REL_BUCKETS // 2
    max_exact = nb // 2
    n = jnp.abs(rel)
    nf = jnp.maximum(n, 1).astype(jnp.float32)
    large = max_exact + (jnp.log(nf / max_exact) / math.log(REL_MAX_DIST / max_exact)
                         * (nb - max_exact)).astype(jnp.int32)
    large = jnp.minimum(large, nb - 1)
    return jnp.where(rel > 0, nb, 0) + jnp.where(n < max_exact, n, large)


def _bias_tables(rel_bias):
    tq = LANES
    trow = jnp.arange(tq)[:, None]
    col = jnp.arange(2 * LANES)[None, :]
    rel = (col - LANES) - trow
    bucket = _rel_bucket(rel)
    tab = sum(jnp.where(bucket == b, rel_bias[b][:, None, None], 0.0) for b in range(REL_BUCKETS))
    far = rel_bias[REL_BUCKETS // 2 - 1]
    tab = tab - far[:, None, None]
    return tab[:, :, :LANES], tab[:, :, LANES:]


def _pad_rows8(a):
    z = jnp.zeros(a.shape[:-2] + (8 - a.shape[-2], a.shape[-1]), a.dtype)
    return jnp.concatenate([z, a], axis=-2)


def _lane_row(vals, off):
    r = jnp.zeros((1, LANES), jnp.float32)
    return r.at[0, off:off + vals.shape[0]].set(vals)


def _layer(x, mod, lw, tables, *, nbatch, t, caches=None, final_norm_w=None):
    (norm_w, w_in_bf, wconv8, lrow, drow, gw, wa_bf, wb_bf, wo_bf) = lw
    near_a, near_b = tables
    d = D_MODEL
    gpb = t // GROUP

    def per_group(a):
        return jnp.broadcast_to(a[:, None, :], (nbatch, gpb, d)).reshape(nbatch * gpb, d)

    shift_g, scale_g, gate_g = [per_group(mod[:, n * d:(n + 1) * d]) for n in range(3)]
    if caches is None:
        state = past_ki = past_kv = None
    else:
        layer, k_all, v_all, ki_all, s_all, conv_prev = caches
        state = (_pad_rows8(conv_prev), s_all, layer)
        past_ki = (ki_all, layer)
        past_kv = (k_all, v_all, layer)

    proj = _inproj_call(x, norm_w, scale_g, shift_g, w_in_bf)
    tq = min(LANES, t)
    madd = _sel_call(proj, nbatch=nbatch, t=t, past_ki=past_ki)
    ob, k_rows, v_rows = _attn_call(proj, madd, near_a[:, :tq], near_b[:, :tq], nbatch=nbatch, t=t,
                                    past_kv=past_kv)
    if state is None and (t // CHUNK) % MERGE_CHUNKS == 0:
        merged, s_new, tails = _gdn_merge_call(proj, ob, wa_bf, wb_bf, wconv8, lrow, drow, gw, nbatch=nbatch, t=t)
    else:
        oa, s_new, tails = _gdn_call(proj, wconv8, lrow, drow, gw, nbatch=nbatch, t=t, state=state)
        merged = _merge_call(oa, ob, wa_bf, wb_bf, proj)
    x_new = _outproj_call(merged, wo_bf, x, gate_g, final_norm_w)

    k_new = k_rows.reshape(nbatch, t, KV_HEADS, HEAD_DIM)
    v_new = v_rows.reshape(nbatch, t, KV_HEADS, HEAD_DIM)
    ki_new = proj[:, OFF_SM + SM_KI:OFF_SM + SM_KI + IDX_DIM].reshape(nbatch, t, IDX_DIM)
    conv_new = tails[:, 8 - (CONV_W - 1):, :]
    return x_new, (k_new, v_new, ki_new, s_new, conv_new)


def kernel(x_prompt, x_sample, c_prompt, c_sample, cache_k, cache_v, cache_idx_k, state_gdn, state_conv,
           norm_w, w_ada, b_ada, w_in, w_conv, a_log, dt_bias, gdn_norm_w, w_branch_a, w_branch_b,
           w_out, rel_bias, final_norm_w):
    depth = w_in.shape[0]
    bp, tp, d = x_prompt.shape
    bs, ts, _ = x_sample.shape
    past = cache_k.shape[2]
    kvw = KV_HEADS * HEAD_DIM

    mod = _ada_call(jnp.concatenate([c_prompt, c_sample], axis=0), w_ada, b_ada)
    tables = _bias_tables(rel_bias)

    xp = x_prompt.reshape(bp * tp, d)
    xs = x_sample.reshape(bs * ts, d)
    new_p, new_s = [], []
    for l in range(depth):
        wconv8 = jnp.concatenate([w_conv[l], jnp.zeros((8 - CONV_W, w_conv.shape[2]), w_conv.dtype)], axis=0)
        lw = (norm_w[l], _relayout_w_in(w_in[l]), wconv8,
              _lane_row(a_log[l], SM_AA), _lane_row(dt_bias[l], SM_AA), gdn_norm_w[l].reshape(1, GDN_DV),
              w_branch_a[l].astype(jnp.bfloat16), w_branch_b[l].astype(jnp.bfloat16),
              w_out[l].astype(jnp.bfloat16))
        fnw = final_norm_w if l == depth - 1 else None
        xp, sp = _layer(xp, mod[l, :bp], lw, tables, nbatch=bp, t=tp, final_norm_w=fnw)
        caches = (l, cache_k.reshape(depth, bs, past * KV_HEADS, HEAD_DIM),
                  cache_v.reshape(depth, bs, past * KV_HEADS, HEAD_DIM), cache_idx_k, state_gdn, state_conv[l])
        xs, ss = _layer(xs, mod[l, bp:], lw, tables, nbatch=bs, t=ts, caches=caches, final_norm_w=fnw)
        new_p.append(sp)
        new_s.append(ss)

    y_prompt = xp.reshape(bp, tp, d)
    y_sample = xs.reshape(bs, ts, d)
    outs_p = [jnp.stack([s[n] for s in new_p]) for n in range(5)]
    outs_s = [jnp.stack([s[n] for s in new_s]) for n in range(5)]
    return (y_prompt, y_sample, *outs_p, *outs_s)
```

```python
import functools
import math

import jax
import jax.numpy as jnp
from jax import lax
from jax.experimental import pallas as pl
from jax.experimental.pallas import tpu as pltpu

D_MODEL = 2048
CHUNK = 64
GDN_HEADS = 16
GDN_DK = 128
GDN_DV = 128
CONV_W = 4
ATT_HEADS = 16
KV_HEADS = 2
HEAD_DIM = 128
IDX_HEADS = 16
IDX_DIM = 64
TOPK_MAX = 256
REL_BUCKETS = 32
REL_MAX_DIST = 128
EPS = 1e-6

LANES = 128
VMEM_LIMIT = 56 * 1024 * 1024

W_QKV = 3 * GDN_HEADS * GDN_DK
OFF_QKV = 0
OFF_ZA = OFF_QKV + W_QKV
OFF_QB = OFF_ZA + D_MODEL
OFF_ZB = OFF_QB + D_MODEL
OFF_GLA = OFF_ZB + D_MODEL
OFF_GLB = OFF_GLA + D_MODEL
OFF_QI = OFF_GLB + D_MODEL
OFF_KB = OFF_QI + IDX_HEADS * IDX_DIM
OFF_VB = OFF_KB + KV_HEADS * HEAD_DIM
OFF_SM = OFF_VB + KV_HEADS * HEAD_DIM
SM_KI, SM_BA, SM_AA, SM_WI = 0, 64, 80, 96
N_PROJ = 18432
GROUP = 64

_IN_SIZES = (W_QKV, D_MODEL, GDN_HEADS, GDN_HEADS, D_MODEL, KV_HEADS * HEAD_DIM, KV_HEADS * HEAD_DIM,
             D_MODEL, IDX_HEADS * IDX_DIM, IDX_DIM, IDX_HEADS, D_MODEL, D_MODEL)


def _cparams(sem):
    return pltpu.CompilerParams(dimension_semantics=sem, vmem_limit_bytes=VMEM_LIMIT)


def _bf(x):
    return x.astype(jnp.bfloat16)


def _dot(a, b):
    return jnp.dot(a, b, preferred_element_type=jnp.float32)


def _dot_nt(a, b):
    return lax.dot_general(a, b, (((1,), (1,)), ((), ())), preferred_element_type=jnp.float32)


def _ada_kernel(c_ref, w_ref, b_ref, o_ref):
    c = c_ref[...]
    a = _bf(c * jax.nn.sigmoid(c))
    o_ref[0] = _dot(a, _bf(w_ref[0])) + b_ref[0]


def _ada_call(c_all, w_ada, b_ada):
    depth, d, n = w_ada.shape
    nb = c_all.shape[0]
    tn = 1024
    return pl.pallas_call(
        _ada_kernel,
        grid=(depth, n // tn),
        in_specs=[pl.BlockSpec((nb, d), lambda l, j: (0, 0)),
                  pl.BlockSpec((1, d, tn), lambda l, j: (l, 0, j)),
                  pl.BlockSpec((1, 1, tn), lambda l, j: (l, 0, j))],
        out_specs=pl.BlockSpec((1, nb, tn), lambda l, j: (l, 0, j)),
        out_shape=jax.ShapeDtypeStruct((depth, nb, n), jnp.float32),
        compiler_params=_cparams(("arbitrary", "arbitrary")),
        name="ada_mod",
    )(c_all, w_ada, b_ada.reshape(depth, 1, n))


def _inproj_kernel(x_ref, nw_ref, sc_ref, sh_ref, w_ref, o_ref, h_ref, *, tm):
    @pl.when(pl.program_id(1) == 0)
    def _():
        nw = nw_ref[...]

        def body(g, carry):
            rows = pl.ds(pl.multiple_of(g * GROUP, GROUP), GROUP)
            x = x_ref[rows, :]
            y = x * lax.rsqrt(jnp.mean(x * x, axis=-1, keepdims=True) + EPS) * nw
            hh = y * (1.0 + sc_ref[pl.ds(g, 1), :]) + sh_ref[pl.ds(g, 1), :]
            h_ref[rows, :] = _bf(hh)
            return carry

        lax.fori_loop(0, tm // GROUP, body, 0)

    o_ref[...] = _dot(h_ref[...], w_ref[...])


def _inproj_call(x, norm_w, scale_g, shift_g, w_bf):
    m, d = x.shape
    n = w_bf.shape[1]
    tm = min(1024, m)
    tn = 1024
    gpt = tm // GROUP
    return pl.pallas_call(
        functools.partial(_inproj_kernel, tm=tm),
        grid=(m // tm, n // tn),
        in_specs=[pl.BlockSpec((tm, d), lambda i, j: (i, 0)),
                  pl.BlockSpec((1, d), lambda i, j: (0, 0)),
                  pl.BlockSpec((gpt, d), lambda i, j: (i, 0)),
                  pl.BlockSpec((gpt, d), lambda i, j: (i, 0)),
                  pl.BlockSpec((d, tn), lambda i, j: (0, j))],
        out_specs=pl.BlockSpec((tm, tn), lambda i, j: (i, j)),
        out_shape=jax.ShapeDtypeStruct((m, n), jnp.float32),
        scratch_shapes=[pltpu.VMEM((tm, d), jnp.bfloat16)],
        compiler_params=_cparams(("arbitrary", "arbitrary")),
        name="inproj",
    )(x, norm_w.reshape(1, d), scale_g, shift_g, w_bf)


def _silu(x):
    hx = 0.5 * x
    return hx * jnp.tanh(hx) + hx


def _l2norm(x):
    return x * lax.rsqrt(jnp.sum(x * x, axis=-1, keepdims=True) + EPS)


def _softplus(x):
    return jnp.maximum(x, 0.0) + jnp.log1p(jnp.exp(-jnp.abs(x)))


INV_BASE = 8
HEADS_PER_PASS = GDN_HEADS


def _unit_lower_inverses(As, ii, jj, eye):
    C = As[0].shape[0]
    sh = INV_BASE.bit_length() - 1
    Ns = [jnp.where((ii >> sh) == (jj >> sh), -A, 0.0) for A in As]
    Ps = [eye + N for N in Ns]
    m = 2
    while m < INV_BASE:
        Nbs = [_bf(N) for N in Ns]
        Ns = [_dot(Nb, Nb) for Nb in Nbs]
        Ps = [P + _dot(_bf(P), _bf(N)) for P, N in zip(Ps, Ns)]
        m *= 2
    s = INV_BASE
    while s < C:
        sh = s.bit_length() - 1
        off = ((ii >> (sh + 1)) == (jj >> (sh + 1))) & (((ii >> sh) & 1) == 1) & (((jj >> sh) & 1) == 0)
        Pbs = [_bf(P) for P in Ps]
        Xs = [_dot(Pb, _bf(jnp.where(off, A, 0.0))) for Pb, A in zip(Pbs, As)]
        Ps = [P - _dot(_bf(X), Pb) for P, X, Pb in zip(Ps, Xs, Pbs)]
        s *= 2
    return Ps


MERGE_CHUNKS = 8
MERGE_COLS = 256


def _gdn_kernel(*refs, has_state, merge=None):
    if merge is not None:
        (qkv_ref, z_ref, sm_ref, wc_ref, lrow_ref, drow_ref, gw_ref, ob_ref, wa_ref, wb_ref, ga_ref, gb_ref,
         mg_ref, sout_ref, tout_ref, S_ref, xe_ref, oa_ref) = refs
        nt, nsteps = merge
        step = pl.program_id(0)
        i = step % nt
        last = (i == nt - 1) & (step < nsteps)
    elif has_state:
        (qkv_ref, z_ref, sm_ref, wc_ref, lrow_ref, drow_ref, gw_ref, cp_ref, s0_ref,
         o_ref, sout_ref, tout_ref, S_ref, xe_ref) = refs
    else:
        (qkv_ref, z_ref, sm_ref, wc_ref, lrow_ref, drow_ref, gw_ref,
         o_ref, sout_ref, tout_ref, S_ref, xe_ref) = refs
    if merge is None:
        i = pl.program_id(1)
        last = i == pl.num_programs(1) - 1
    C = CHUNK
    hw = GDN_HEADS * GDN_DK

    if merge is not None:
        @pl.when(step == 0)
        def _():
            oa_ref[...] = jnp.zeros(oa_ref.shape, jnp.bfloat16)

        group = step // MERGE_CHUNKS
        prev = (group + 1) % 2
        ysum = [None, None]

        def merge_piece(kt):
            ks = slice(kt * MERGE_COLS, (kt + 1) * MERGE_COLS)
            for n, (lhs, w_ref) in enumerate(((oa_ref[prev, :, ks], wa_ref), (ob_ref[:, ks], wb_ref))):
                part = _dot(lhs, w_ref[ks, :])
                ysum[n] = part if ysum[n] is None else ysum[n] + part

        pieces = [functools.partial(merge_piece, kt) for kt in range(D_MODEL // MERGE_COLS)]
        o_rows = pl.ds(pl.multiple_of((step % MERGE_CHUNKS) * C, C), C)

        def store_o(cols, val):
            oa_ref[group % 2, o_rows, cols] = val
    else:
        pieces = []

        def store_o(cols, val):
            o_ref[:, cols] = val

    def filler():
        if pieces:
            pieces.pop(0)()

    @pl.when(i == 0)
    def _():
        if has_state:
            S_ref[...] = s0_ref[0, 0]
            xe_ref[0:8, :] = cp_ref[0]
        else:
            S_ref[...] = jnp.zeros(S_ref.shape, jnp.float32)
            xe_ref[0:8, :] = jnp.zeros((8, xe_ref.shape[1]), jnp.float32)

    @pl.when(i > 0)
    def _():
        xe_ref[0:8, :] = xe_ref[C:C + 8, :]

    xe_ref[8:8 + C, :] = qkv_ref[...]

    sm = sm_ref[...]
    beta_all = jax.nn.sigmoid(sm)
    g_all = -jnp.exp(lrow_ref[...]) * _softplus(sm + drow_ref[...])
    rowc = lax.broadcasted_iota(jnp.int32, (C, LANES), 0)
    gc = g_all
    s = 1
    while s < C:
        gc = gc + jnp.where(rowc >= s, pltpu.roll(gc, s, 0), 0.0)
        s *= 2
    glast = gc[C - 1:C, :]
    egc_all = jnp.exp(gc)
    ekd_all = jnp.exp(glast - gc)
    egl_all = jnp.exp(glast)
    gc_t = gc.T

    ii = lax.broadcasted_iota(jnp.int32, (C, C), 0)
    jj = lax.broadcasted_iota(jnp.int32, (C, C), 1)
    eye = jnp.where(ii == jj, 1.0, 0.0)
    gw = gw_ref[...]


    def col(p, h):
        return slice(p * hw + h * GDN_DK, p * hw + (h + 1) * GDN_DK)

    def conv(p, h):
        w = wc_ref[:, col(p, h)]
        y = xe_ref[8:8 + C, col(p, h)] * w[CONV_W - 1:CONV_W]
        for s in range(1, CONV_W):
            y = y + xe_ref[8 - s:8 - s + C, col(p, h)] * w[CONV_W - 1 - s:CONV_W - s]
        return _silu(y)

    def lane(a, l):
        return a[:, l:l + 1]

    def run(heads):
        ks, qs = [], []
        for n, h in enumerate(heads):
            ks.append(_l2norm(conv(1, h)))
            if n % 4 == 3:
                filler()
        for n, h in enumerate(heads):
            qs.append(_l2norm(conv(0, h)) * (GDN_DK ** -0.5))
            if n % 4 == 3:
                filler()
        kbs = [k * lane(beta_all, SM_BA + h) for h, k in zip(heads, ks)]
        kqs = [_dot_nt(_bf(jnp.concatenate([kb, q], axis=0)), _bf(k)) for kb, q, k in zip(kbs, qs, ks)]
        decays = [jnp.where(ii >= jj,
                            jnp.exp(jnp.minimum(lane(gc, SM_AA + h) - gc_t[SM_AA + h:SM_AA + h + 1, :], 0.0)),
                            0.0)
                  for h in heads]
        As = [jnp.where(ii > jj, kq[:C] * d, 0.0) for kq, d in zip(kqs, decays)]
        qks = [_bf(kq[C:] * d) for kq, d in zip(kqs, decays)]
        Ps = _unit_lower_inverses(As, ii, jj, eye)
        vs = [conv(2, h) for h in heads]
        rhs = [_bf(jnp.concatenate([v * lane(beta_all, SM_BA + h), kb * lane(egc_all, SM_AA + h)], axis=1))
               for h, v, kb in zip(heads, vs, kbs)]
        uws = [_dot(_bf(P), r) for P, r in zip(Ps, rhs)]
        Ss = [S_ref[h] for h in heads]
        wqs = [_dot(_bf(jnp.concatenate([uw[:, GDN_DV:], q * lane(egc_all, SM_AA + h)], axis=0)), _bf(S))
               for h, uw, q, S in zip(heads, uws, qs, Ss)]
        vnbs = [_bf(uw[:, :GDN_DV] - wq[:C]) for uw, wq in zip(uws, wqs)]
        kdts = [_bf((k * lane(ekd_all, SM_AA + h)).T) for h, k in zip(heads, ks)]
        for h, S, kdt, vnb in zip(heads, Ss, kdts, vnbs):
            S_ref[h] = S * lane(egl_all, SM_AA + h) + _dot(kdt, vnb)
        os_ = [wq[C:] + _dot(qk, vnb) for wq, qk, vnb in zip(wqs, qks, vnbs)]
        for h, o in zip(heads, os_):
            o = o * lax.rsqrt(jnp.mean(o * o, axis=-1, keepdims=True) + EPS) * gw
            z = z_ref[:, col(0, h)]
            store_o(col(0, h), _bf(o * _silu(z)))

    for h0 in range(0, GDN_HEADS, HEADS_PER_PASS):
        run(range(h0, h0 + HEADS_PER_PASS))

    if merge is not None:
        while pieces:
            filler()
        mg_ref[...] = _bf(jax.nn.sigmoid(ga_ref[...]) * ysum[0] + jax.nn.sigmoid(gb_ref[...]) * ysum[1])

    @pl.when(last)
    def _():
        sout_ref[0] = S_ref[...]
        tout_ref[0] = xe_ref[C:C + 8, :]


def _gdn_call(proj, wconv8, lrow, drow, gw, *, nbatch, t, state=None):
    nt = t // CHUNK
    hh = GDN_HEADS
    const = lambda b, i: (0, 0)
    state_spec = pl.BlockSpec((1, hh, GDN_DK, GDN_DV), lambda b, i: (b, 0, 0, 0))
    tail_spec = pl.BlockSpec((1, 8, W_QKV), lambda b, i: (b, 0, 0))
    in_specs = [pl.BlockSpec((CHUNK, W_QKV), lambda b, i: (b * nt + i, OFF_QKV // W_QKV)),
                pl.BlockSpec((CHUNK, D_MODEL), lambda b, i: (b * nt + i, OFF_ZA // D_MODEL)),
                pl.BlockSpec((CHUNK, LANES), lambda b, i: (b * nt + i, OFF_SM // LANES)),
                pl.BlockSpec((8, W_QKV), const),
                pl.BlockSpec((1, LANES), const),
                pl.BlockSpec((1, LANES), const),
                pl.BlockSpec((1, GDN_DV), const)]
    args = [proj, proj, proj, wconv8, lrow, drow, gw]
    if state is not None:
        conv_prev8, s_all, layer = state
        in_specs += [tail_spec,
                     pl.BlockSpec((1, 1, hh, GDN_DK, GDN_DV), lambda b, i: (layer, b, 0, 0, 0))]
        args += [conv_prev8, s_all]
    return pl.pallas_call(
        functools.partial(_gdn_kernel, has_state=state is not None),
        grid=(nbatch, nt),
        in_specs=in_specs,
        out_specs=[pl.BlockSpec((CHUNK, hh * GDN_DV), lambda b, i: (b * nt + i, 0)),
                   state_spec, tail_spec],
        out_shape=[jax.ShapeDtypeStruct((nbatch * t, hh * GDN_DV), jnp.bfloat16),
                   jax.ShapeDtypeStruct((nbatch, hh, GDN_DK, GDN_DV), jnp.float32),
                   jax.ShapeDtypeStruct((nbatch, 8, W_QKV), jnp.float32)],
        scratch_shapes=[pltpu.VMEM((hh, GDN_DK, GDN_DV), jnp.float32),
                        pltpu.VMEM((8 + CHUNK, W_QKV), jnp.float32)],
        compiler_params=_cparams(("arbitrary", "arbitrary")),
        name="gdn",
    )(*args)


def _gdn_merge_call(proj, ob, wa_bf, wb_bf, wconv8, lrow, drow, gw, *, nbatch, t):
    nt = t // CHUNK
    hh = GDN_HEADS
    d = D_MODEL
    nsteps = nbatch * nt
    ngroups = nsteps // MERGE_CHUNKS

    def out_group(s):
        return jnp.where(s < MERGE_CHUNKS, ngroups, s // MERGE_CHUNKS - 1)
    rows = MERGE_CHUNKS * CHUNK
    assert nt % MERGE_CHUNKS == 0 and d == MERGE_CHUNKS * MERGE_COLS

    def chunk(s):
        return jnp.minimum(s, nsteps - 1)

    def group(s):
        return jnp.maximum(s // MERGE_CHUNKS - 1, 0)

    const = lambda s: (0, 0)
    gate_spec = lambda off: pl.BlockSpec(
        (rows, MERGE_COLS), lambda s: (group(s), off // MERGE_COLS + s % MERGE_CHUNKS))
    w_spec = pl.BlockSpec((d, MERGE_COLS), lambda s: (0, s % MERGE_CHUNKS))
    in_specs = [pl.BlockSpec((CHUNK, W_QKV), lambda s: (chunk(s), OFF_QKV // W_QKV)),
                pl.BlockSpec((CHUNK, D_MODEL), lambda s: (chunk(s), OFF_ZA // D_MODEL)),
                pl.BlockSpec((CHUNK, LANES), lambda s: (chunk(s), OFF_SM // LANES)),
                pl.BlockSpec((8, W_QKV), const),
                pl.BlockSpec((1, LANES), const),
                pl.BlockSpec((1, LANES), const),
                pl.BlockSpec((1, GDN_DV), const),
                pl.BlockSpec((rows, d), lambda s: (group(s), 0)),
                w_spec, w_spec, gate_spec(OFF_GLA), gate_spec(OFF_GLB)]
    return pl.pallas_call(
        functools.partial(_gdn_kernel, has_state=False, merge=(nt, nsteps)),
        grid=(nsteps + MERGE_CHUNKS,),
        in_specs=in_specs,
        out_specs=[pl.BlockSpec((rows, MERGE_COLS), lambda s: (out_group(s), s % MERGE_CHUNKS)),
                   pl.BlockSpec((1, hh, GDN_DK, GDN_DV), lambda s: (chunk(s) // nt, 0, 0, 0)),
                   pl.BlockSpec((1, 8, W_QKV), lambda s: (chunk(s) // nt, 0, 0))],
        out_shape=[jax.ShapeDtypeStruct((nbatch * t + rows, d), jnp.bfloat16),
                   jax.ShapeDtypeStruct((nbatch, hh, GDN_DK, GDN_DV), jnp.float32),
                   jax.ShapeDtypeStruct((nbatch, 8, W_QKV), jnp.float32)],
        scratch_shapes=[pltpu.VMEM((hh, GDN_DK, GDN_DV), jnp.float32),
                        pltpu.VMEM((8 + CHUNK, W_QKV), jnp.float32),
                        pltpu.VMEM((2, rows, d), jnp.bfloat16)],
        compiler_params=_cparams(("arbitrary",)),
        name="gdn_merge",
    )(proj, proj, proj, wconv8, lrow, drow, gw, ob, wa_bf, wb_bf, proj, proj)


_INT_MIN = -2147483648
_KEY_NEG_INF = -2139095041
ATT_GROUP = ATT_HEADS // KV_HEADS
CNT_VREGS = 8


def _visible_chunks(q0, rows, nreal, kc, nkc):
    lim_max = jnp.minimum(((q0 + rows - 1) // CHUNK + 1) * CHUNK, nreal)
    return jnp.minimum((lim_max + kc - 1) // kc, nkc)


def _sel_kernel(*refs, tq, rg, npart, t_cur, past, lp, kc, topk):
    if past:
        qi_ref, smq_ref, smk_ref, kip_ref, m_ref, kibf, key_ref, qis_ref = refs
    else:
        qi_ref, smq_ref, smk_ref, m_ref, kibf, key_ref, qis_ref = refs
    i = pl.program_id(1)
    nreal = past + t_cur
    nkc = lp // kc
    nl = kc // LANES
    groups = range(rg)
    pw = tq // npart

    @pl.when(i == 0)
    def _():
        for p in range(npart):
            if past:
                kibf[p, 0:past, :] = _bf(kip_ref[0, p])
            kibf[p, past:nreal, :] = _bf(smk_ref[p * t_cur:(p + 1) * t_cur, SM_KI:SM_KI + IDX_DIM])
            if lp > nreal:
                kibf[p, nreal:lp, :] = jnp.zeros((lp - nreal, IDX_DIM), jnp.bfloat16)

    q0 = past + i * (rg * pw)
    nvis = _visible_chunks(q0, rg * pw, nreal, kc, nkc)
    tlane = lax.broadcasted_iota(jnp.int32, (1, tq), 1)
    if npart > 1:
        tlane = tlane % pw
    lims = [jnp.minimum(((q0 + g * tq + tlane) // CHUNK + 1) * CHUNK, nreal) for g in groups]

    def rows_of(g):
        return slice(g * tq, (g + 1) * tq)

    def keys_at(start, w):
        return pl.ds(pl.multiple_of(start, LANES), w)

    def pos_at(start, n, width=tq):
        return start + lax.broadcasted_iota(jnp.int32, (n, width), 0)

    trim = npart == 1 and past == 0 and rg * tq == kc
    if trim:
        nfull = i
        diag = [(g + 1) * tq for g in groups]
    else:
        nfull = nvis
        diag = None

    for g in groups:
        for p in range(npart):
            part = slice(g * tq + p * pw, g * tq + (p + 1) * pw)
            w_t = (smq_ref[part, :] * ((IDX_HEADS ** -0.5) * (IDX_DIM ** -0.5))).T
            for hh in range(IDX_HEADS):
                qis_ref[hh * pw:(hh + 1) * pw, :] = _bf(qi_ref[part, hh * IDX_DIM:(hh + 1) * IDX_DIM])
            lim = lims[g][:, p * pw:(p + 1) * pw]

            def score(start, w, masked, p=p, part=part, w_t=w_t, lim=lim):
                d = _dot_nt(kibf[p, keys_at(start, w), :], qis_ref[0:IDX_HEADS * pw, :])
                acc = jnp.zeros((w, pw), jnp.float32)
                for hh in range(IDX_HEADS):
                    acc = acc + (w_t[SM_WI + hh:SM_WI + hh + 1, :]
                                 * jnp.maximum(d[:, hh * pw:(hh + 1) * pw], 0.0))
                if masked:
                    acc = jnp.where(pos_at(start, w, pw) < lim, acc, -jnp.inf)
                bits = pltpu.bitcast(acc, jnp.int32)
                key_ref[keys_at(start, w), part] = jnp.where(bits < 0, bits ^ 0x7FFFFFFF, bits)

            lax.fori_loop(0, nfull, lambda c, carry, score=score: (score(c * kc, kc, not trim), carry)[1], 0)
            if trim:
                score(q0, diag[g], True)

    all_visible = (not trim) and past >= (nkc - 1) * kc

    def count_ge(cands):
        def count(keys, w, accs, which=groups):
            accs = list(accs)
            for g in which:
                hit = jnp.where(key_ref[keys, rows_of(g)] >= cands[g], 1.0, 0.0)
                accs[g] = accs[g] + jnp.sum(hit.reshape(w // cnt_rows, cnt_rows, tq), axis=0)
            return tuple(accs)

        cnt_rows = 8 * max(1, CNT_VREGS // rg)
        accs = tuple(jnp.zeros((cnt_rows, tq), jnp.float32) for _ in groups)
        if all_visible:
            accs = count(slice(0, lp), lp, accs)
        else:
            accs = lax.fori_loop(0, nfull, lambda c, a: count(keys_at(c * kc, kc), kc, a), accs)
            if trim:
                for g in groups:
                    accs = count(keys_at(q0, diag[g]), diag[g], accs, which=[g])
        return [jnp.sum(a, axis=0, keepdims=True) for a in accs]

    def bit_step(it, taus_u):
        cands_u = [t | lax.shift_left(jnp.int32(1), 31 - it) for t in taus_u]
        cnts = count_ge([c ^ _INT_MIN for c in cands_u])
        return tuple(jnp.where(n >= float(topk), c, t) for n, c, t in zip(cnts, cands_u, taus_u))

    taus_u = lax.fori_loop(0, 32, bit_step, tuple(jnp.zeros((1, tq), jnp.int32) for _ in groups))
    taus = [t ^ _INT_MIN for t in taus_u]
    cnts_ge = count_ge(taus)
    cnts_gt = count_ge([t + 1 for t in taus])
    needs = [float(topk) - n for n in cnts_gt]
    any_excess = jnp.int32(0)
    for g in groups:
        excess = (cnts_ge[g] > float(topk)) & (taus[g] > _KEY_NEG_INF)
        any_excess = jnp.maximum(any_excess, jnp.max(jnp.where(excess, 1, 0)))

    ea = lax.broadcasted_iota(jnp.int32, (tq, tq), 0)
    eb = lax.broadcasted_iota(jnp.int32, (tq, tq), 1)
    eye = jnp.where(ea == eb, 1.0, 0.0).astype(jnp.bfloat16)

    def store_mask(g, start, w, sel_t):
        sel = _dot_nt(eye, jnp.where(sel_t, 1.0, 0.0).astype(jnp.bfloat16))
        m_ref[rows_of(g), keys_at(start, w)] = _bf(jnp.where(sel > 0.5, 0.0, -jnp.inf))

    @pl.when(any_excess == 0)
    def _():
        def span(g, start, w, masked):
            sel_t = key_ref[keys_at(start, w), rows_of(g)] >= taus[g]
            if masked:
                sel_t = sel_t & (pos_at(start, w) < lims[g])
            store_mask(g, start, w, sel_t)

        def body(c, carry):
            for g in groups:
                span(g, c * kc, kc, not trim)
            return carry

        lax.fori_loop(0, nfull, body, 0)
        if trim:
            for g in groups:
                span(g, q0, diag[g], True)

    @pl.when(any_excess != 0)
    def _():
        la = lax.broadcasted_iota(jnp.int32, (LANES, LANES), 0)
        lb = lax.broadcasted_iota(jnp.int32, (LANES, LANES), 1)
        lower = jnp.where(la >= lb, 1.0, 0.0).astype(jnp.bfloat16)

        def block(g, start, carry):
            key = key_ref[keys_at(start, LANES), rows_of(g)]
            eq = key == taus[g]
            pref = _dot(lower, jnp.where(eq, 1.0, 0.0).astype(jnp.bfloat16)) + carry
            sel_t = ((key > taus[g]) | (eq & (pref <= needs[g]))) & (pos_at(start, LANES) < lims[g])
            store_mask(g, start, LANES, sel_t)
            return pref[LANES - 1:LANES, :]

        def body(c, carries):
            carries = list(carries)
            for j in range(nl):
                for g in groups:
                    carries[g] = block(g, c * kc + j * LANES, carries[g])
            return tuple(carries)

        carries = lax.fori_loop(0, nfull, body, tuple(jnp.zeros((1, tq), jnp.float32) for _ in groups))
        if trim:
            for g in groups:
                carry = carries[g]
                for j in range(diag[g] // LANES):
                    carry = block(g, q0 + j * LANES, carry)

    if trim:
        for g in groups:
            if diag[g] < kc:
                m_ref[rows_of(g), keys_at(q0 + diag[g], kc - diag[g])] = jnp.full(
                    (tq, kc - diag[g]), -jnp.inf, jnp.bfloat16)

    def fill(c, carry):
        m_ref[:, keys_at(c * kc, kc)] = jnp.full((rg * tq, kc), -jnp.inf, jnp.bfloat16)
        return carry

    lax.fori_loop(nfull + 1 if trim else nvis, nkc, fill, 0)


def _sel_call(proj, *, nbatch, t, past_ki=None):
    past = 0 if past_ki is None else past_ki[0].shape[2]
    _, kc, lp, topk = _attn_geometry(t, past)
    npart = LANES // t if (t < LANES and nbatch % (LANES // t) == 0) else 1
    tq = min(LANES, t * npart)
    rg = min(4, (t * npart) // tq)
    rows = rg * tq
    nr = (t * npart) // rows
    qiw = IDX_HEADS * IDX_DIM
    in_specs = [pl.BlockSpec((rows, qiw), lambda b, i: (b * nr + i, OFF_QI // qiw)),
                pl.BlockSpec((rows, LANES), lambda b, i: (b * nr + i, OFF_SM // LANES)),
                pl.BlockSpec((npart * t, LANES), lambda b, i: (b, OFF_SM // LANES))]
    args = [proj, proj, proj]
    if past:
        ki_all, layer = past_ki
        in_specs.append(pl.BlockSpec((1, npart, past, IDX_DIM), lambda b, i: (layer, b, 0, 0)))
        args.append(ki_all)
    return pl.pallas_call(
        functools.partial(_sel_kernel, tq=tq, rg=rg, npart=npart, t_cur=t, past=past, lp=lp, kc=kc, topk=topk),
        grid=(nbatch // npart, nr),
        in_specs=in_specs,
        out_specs=pl.BlockSpec((rows, lp), lambda b, i: (b * nr + i, 0)),
        out_shape=jax.ShapeDtypeStruct((nbatch * t, lp), jnp.bfloat16),
        scratch_shapes=[pltpu.VMEM((npart, lp, IDX_DIM), jnp.bfloat16),
                        pltpu.VMEM((lp, rows), jnp.int32),
                        pltpu.VMEM((IDX_HEADS * tq, IDX_DIM), jnp.bfloat16)],
        compiler_params=_cparams(("arbitrary", "arbitrary")),
        name="sel_past" if past else "sel",
    )(*args)


def _attn_geometry(t, past):
    tq = min(128, t)
    nreal = past + t
    kc = 512 if nreal % 512 == 0 else 384
    if nreal < kc:
        kc = LANES * (-(-nreal // LANES))
    lp = kc * (-(-nreal // kc))
    assert past % LANES == 0 and past + LANES * (-(-t // LANES)) <= lp
    return tq, kc, lp, min(TOPK_MAX, nreal // 4)


def _attn_kernel(*refs, tq, t_cur, past, lp, kc):
    if past:
        (qb_ref, zb_ref, madd_ref, k_ref, v_ref, kp_ref, vp_ref, na_ref, nb_ref,
         o_ref, ko_ref, vo_ref, kbf, vbf, lg_ref, qs_ref, acc_ref, den_ref, mb_ref) = refs
    else:
        (qb_ref, zb_ref, madd_ref, k_ref, v_ref, na_ref, nb_ref,
         o_ref, ko_ref, vo_ref, kbf, vbf, lg_ref, qs_ref, acc_ref, den_ref, mb_ref) = refs
    i = pl.program_id(1)
    nreal = past + t_cur
    G = ATT_GROUP
    gt = G * tq

    @pl.when(i == 0)
    def _():
        for n in range(KV_HEADS):
            ncol = slice(n * HEAD_DIM, (n + 1) * HEAD_DIM)
            if past:
                kbf[0:past, ncol] = _bf(kp_ref[pl.ds(n, past, stride=KV_HEADS), :])
                vbf[0:past, ncol] = _bf(vp_ref[pl.ds(n, past, stride=KV_HEADS), :])
            ko_ref[pl.ds(n, t_cur, stride=KV_HEADS), :] = k_ref[:, ncol]
            vo_ref[pl.ds(n, t_cur, stride=KV_HEADS), :] = v_ref[:, ncol]
        kbf[past:nreal, :] = _bf(k_ref[...])
        vbf[past:nreal, :] = _bf(v_ref[...])
        if lp > nreal:
            kbf[nreal:lp, :] = jnp.zeros((lp - nreal, KV_HEADS * HEAD_DIM), jnp.bfloat16)
            vbf[nreal:lp, :] = jnp.zeros((lp - nreal, KV_HEADS * HEAD_DIM), jnp.bfloat16)

    q0 = past + i * tq

    far_end = jnp.maximum(q0 - LANES, 0)
    nfull = far_end // kc
    nleft = (far_end - nfull * kc) // LANES
    tail0 = pl.multiple_of(far_end, LANES)
    tw = 2 * LANES
    first = q0 == 0

    def keys_at(off, w):
        return pl.ds(pl.multiple_of(off, LANES), w)

    scale = HEAD_DIM ** -0.5
    for hd in range(ATT_HEADS):
        qs_ref[hd * tq:(hd + 1) * tq, :] = _bf(qb_ref[:, hd * HEAD_DIM:(hd + 1) * HEAD_DIM] * scale)

    for n in range(KV_HEADS):
        ncol = slice(n * HEAD_DIM, (n + 1) * HEAD_DIM)
        grows = slice(n * gt, (n + 1) * gt)

        def logits(off, w, bias=None):
            sc = _dot_nt(qs_ref[grows, :], kbf[keys_at(off, w), ncol])
            ma = madd_ref[:, keys_at(off, w)].astype(jnp.float32)
            for g in range(G):
                r = slice(g * tq, (g + 1) * tq)
                s = sc[r] + ma
                if bias is not None:
                    s = s + bias(n * G + g)
                lg_ref[r, keys_at(off, w)] = s
                mt = mb_ref[r, :]
                for j in range(w // LANES):
                    mt = jnp.maximum(mt, s[:, j * LANES:(j + 1) * LANES])
                mb_ref[r, :] = mt

        def tail_bias(hd):
            zero = jnp.zeros((tq, LANES), jnp.float32)
            return jnp.concatenate([jnp.where(first, nb_ref[hd], na_ref[hd]),
                                    jnp.where(first, zero, nb_ref[hd])], axis=1)

        def weighted_values(off, w):
            p = jnp.exp(lg_ref[:, keys_at(off, w)] - jnp.concatenate([mb_ref[...]] * (w // LANES), axis=1))
            den = den_ref[...]
            for j in range(w // LANES):
                den = den + p[:, j * LANES:(j + 1) * LANES]
            den_ref[...] = den
            acc_ref[...] = acc_ref[...] + _dot(_bf(p), vbf[keys_at(off, w), ncol])

        def walk(fn, tail_kwargs):
            lax.fori_loop(0, nfull, lambda c, carry: (fn(c * kc, kc), carry)[1], 0)
            lax.fori_loop(0, nleft, lambda b, carry: (fn(nfull * kc + b * LANES, LANES), carry)[1], 0)
            fn(tail0, tw, **tail_kwargs)

        mb_ref[...] = jnp.full((gt, LANES), -jnp.inf, jnp.float32)
        walk(logits, dict(bias=tail_bias))
        for g in range(G):
            r = slice(g * tq, (g + 1) * tq)
            mb_ref[r, :] = jnp.broadcast_to(jnp.max(mb_ref[r, :], axis=1, keepdims=True), (tq, LANES))

        acc_ref[...] = jnp.zeros((gt, HEAD_DIM), jnp.float32)
        den_ref[...] = jnp.zeros((gt, LANES), jnp.float32)
        walk(weighted_values, {})

        for g in range(G):
            r = slice(g * tq, (g + 1) * tq)
            hcol = slice((n * G + g) * HEAD_DIM, (n * G + g + 1) * HEAD_DIM)
            den = jnp.sum(den_ref[r, :], axis=1, keepdims=True)
            z = zb_ref[:, hcol]
            o_ref[:, hcol] = _bf((acc_ref[r, :] / den) * _silu(z))


def _attn_call(proj, madd, near_a, near_b, *, nbatch, t, past_kv=None):
    past = 0 if past_kv is None else past_kv[0].shape[2] // KV_HEADS
    tq, kc, lp, _ = _attn_geometry(t, past)
    nq = t // tq
    kvw = KV_HEADS * HEAD_DIM
    gt = ATT_GROUP * tq

    in_specs = [pl.BlockSpec((tq, D_MODEL), lambda b, i: (b * nq + i, OFF_QB // D_MODEL)),
                pl.BlockSpec((tq, D_MODEL), lambda b, i: (b * nq + i, OFF_ZB // D_MODEL)),
                pl.BlockSpec((tq, lp), lambda b, i: (b * nq + i, 0)),
                pl.BlockSpec((t, kvw), lambda b, i: (b, OFF_KB // kvw)),
                pl.BlockSpec((t, kvw), lambda b, i: (b, OFF_VB // kvw))]
    args = [proj, proj, madd, proj, proj]
    if past:
        k_all, v_all, layer = past_kv
        in_specs += [pl.BlockSpec((None, None, past * KV_HEADS, HEAD_DIM), lambda b, i: (layer, b, 0, 0)),
                     pl.BlockSpec((None, None, past * KV_HEADS, HEAD_DIM), lambda b, i: (layer, b, 0, 0))]
        args += [k_all, v_all]
    in_specs += [pl.BlockSpec((ATT_HEADS, tq, LANES), lambda b, i: (0, 0, 0)),
                 pl.BlockSpec((ATT_HEADS, tq, LANES), lambda b, i: (0, 0, 0))]
    args += [near_a, near_b]

    return pl.pallas_call(
        functools.partial(_attn_kernel, tq=tq, t_cur=t, past=past, lp=lp, kc=kc),
        grid=(nbatch, nq),
        in_specs=in_specs,
        out_specs=[pl.BlockSpec((tq, D_MODEL), lambda b, i: (b * nq + i, 0)),
                   pl.BlockSpec((None, t * KV_HEADS, HEAD_DIM), lambda b, i: (b, 0, 0)),
                   pl.BlockSpec((None, t * KV_HEADS, HEAD_DIM), lambda b, i: (b, 0, 0))],
        out_shape=[jax.ShapeDtypeStruct((nbatch * t, D_MODEL), jnp.bfloat16),
                   jax.ShapeDtypeStruct((nbatch, t * KV_HEADS, HEAD_DIM), jnp.float32),
                   jax.ShapeDtypeStruct((nbatch, t * KV_HEADS, HEAD_DIM), jnp.float32)],
        scratch_shapes=[pltpu.VMEM((lp, kvw), jnp.bfloat16),
                        pltpu.VMEM((lp, kvw), jnp.bfloat16),
                        pltpu.VMEM((gt, lp), jnp.float32),
                        pltpu.VMEM((ATT_HEADS * tq, HEAD_DIM), jnp.bfloat16),
                        pltpu.VMEM((gt, HEAD_DIM), jnp.float32),
                        pltpu.VMEM((gt, LANES), jnp.float32),
                        pltpu.VMEM((gt, LANES), jnp.float32)],
        compiler_params=_cparams(("arbitrary", "arbitrary")),
        name="attn_past" if past else "attn",
    )(*args)


def _merge_kernel(oa_ref, ob_ref, wa_ref, wb_ref, ga_ref, gb_ref, o_ref):
    ya = _dot(oa_ref[...], wa_ref[...])
    yb = _dot(ob_ref[...], wb_ref[...])
    o_ref[...] = _bf(jax.nn.sigmoid(ga_ref[...]) * ya + jax.nn.sigmoid(gb_ref[...]) * yb)


def _merge_call(oa, ob, wa_bf, wb_bf, proj):
    m, d = oa.shape
    tm = min(1024, m)
    tn = 512
    return pl.pallas_call(
        _merge_kernel,
        grid=(m // tm, d // tn),
        in_specs=[pl.BlockSpec((tm, d), lambda i, j: (i, 0)),
                  pl.BlockSpec((tm, d), lambda i, j: (i, 0)),
                  pl.BlockSpec((d, tn), lambda i, j: (0, j)),
                  pl.BlockSpec((d, tn), lambda i, j: (0, j)),
                  pl.BlockSpec((tm, tn), lambda i, j: (i, OFF_GLA // tn + j)),
                  pl.BlockSpec((tm, tn), lambda i, j: (i, OFF_GLB // tn + j))],
        out_specs=pl.BlockSpec((tm, tn), lambda i, j: (i, j)),
        out_shape=jax.ShapeDtypeStruct((m, d), jnp.bfloat16),
        compiler_params=_cparams(("arbitrary", "arbitrary")),
        name="merge",
    )(oa, ob, wa_bf, wb_bf, proj, proj)


def _outproj_kernel(*refs, tm, final_norm):
    if final_norm:
        mg_ref, w_ref, x_ref, gate_ref, nw_ref, o_ref = refs
    else:
        mg_ref, w_ref, x_ref, gate_ref, o_ref = refs
    y = _dot(mg_ref[...], w_ref[...])
    for g in range(tm // GROUP):
        r = slice(g * GROUP, (g + 1) * GROUP)
        xn = x_ref[r, :] + gate_ref[g:g + 1, :] * y[r]
        if final_norm:
            xn = xn * lax.rsqrt(jnp.mean(xn * xn, axis=-1, keepdims=True) + EPS) * nw_ref[...]
        o_ref[r, :] = xn


def _outproj_call(merged, wo_bf, x, gate_g, final_norm_w=None):
    m, d = x.shape
    tm = min(512, m)
    gpt = tm // GROUP
    in_specs = [pl.BlockSpec((tm, d), lambda i: (i, 0)),
                pl.BlockSpec((d, d), lambda i: (0, 0)),
                pl.BlockSpec((tm, d), lambda i: (i, 0)),
                pl.BlockSpec((gpt, d), lambda i: (i, 0))]
    args = [merged, wo_bf, x, gate_g]
    if final_norm_w is not None:
        in_specs.append(pl.BlockSpec((1, d), lambda i: (0, 0)))
        args.append(final_norm_w.reshape(1, d))
    return pl.pallas_call(
        functools.partial(_outproj_kernel, tm=tm, final_norm=final_norm_w is not None),
        grid=(m // tm,),
        in_specs=in_specs,
        out_specs=pl.BlockSpec((tm, d), lambda i: (i, 0)),
        out_shape=jax.ShapeDtypeStruct((m, d), jnp.float32),
        compiler_params=_cparams(("arbitrary",)),
        name="outproj",
    )(*args)


def _relayout_w_in(w):
    offs = [0]
    for s in _IN_SIZES:
        offs.append(offs[-1] + s)
    (qkv, za, ba, aa, qb, kb, vb, zb, qi, ki, wi, gla, glb) = [w[:, offs[n]:offs[n + 1]] for n in range(13)]
    d = w.shape[0]
    pad_sm = jnp.zeros((d, LANES - (IDX_DIM + 3 * GDN_HEADS)), w.dtype)
    cols = [qkv, za, qb, zb, gla, glb, qi, kb, vb, ki, ba, aa, wi, pad_sm]
    out = jnp.concatenate(cols, axis=1)
    pad = jnp.zeros((d, N_PROJ - out.shape[1]), w.dtype)
    return jnp.concatenate([out, pad], axis=1).astype(jnp.bfloat16)


def _rel_bucket(rel):
    nb = REL_BUCKETS // 2
    max_exact = nb // 2
    n = jnp.abs(rel)
    nf = jnp.maximum(n, 1).astype(jnp.float32)
    large = max_exact + (jnp.log(nf / max_exact) / math.log(REL_MAX_DIST / max_exact)
                         * (nb - max_exact)).astype(jnp.int32)
    large = jnp.minimum(large, nb - 1)
    return jnp.where(rel > 0, nb, 0) + jnp.where(n < max_exact, n, large)


def _bias_tables(rel_bias):
    tq = LANES
    trow = jnp.arange(tq)[:, None]
    col = jnp.arange(2 * LANES)[None, :]
    rel = (col - LANES) - trow
    bucket = _rel_bucket(rel)
    tab = sum(jnp.where(bucket == b, rel_bias[b][:, None, None], 0.0) for b in range(REL_BUCKETS))
    far = rel_bias[REL_BUCKETS // 2 - 1]
    tab = tab - far[:, None, None]
    return tab[:, :, :LANES], tab[:, :, LANES:]


def _pad_rows8(a):
    z = jnp.zeros(a.shape[:-2] + (8 - a.shape[-2], a.shape[-1]), a.dtype)
    return jnp.concatenate([z, a], axis=-2)


def _lane_row(vals, off):
    r = jnp.zeros((1, LANES), jnp.float32)
    return r.at[0, off:off + vals.shape[0]].set(vals)


def _layer(x, mod, lw, tables, *, nbatch, t, caches=None, final_norm_w=None):
    (norm_w, w_in_bf, wconv8, lrow, drow, gw, wa_bf, wb_bf, wo_bf) = lw
    near_a, near_b = tables
    d = D_MODEL
    gpb = t // GROUP

    def per_group(a):
        return jnp.broadcast_to(a[:, None, :], (nbatch, gpb, d)).reshape(nbatch * gpb, d)

    shift_g, scale_g, gate_g = [per_group(mod[:, n * d:(n + 1) * d]) for n in range(3)]
    if caches is None:
        state = past_ki = past_kv = None
    else:
        layer, k_all, v_all, ki_all, s_all, conv_prev = caches
        state = (_pad_rows8(conv_prev), s_all, layer)
        past_ki = (ki_all, layer)
        past_kv = (k_all, v_all, layer)

    proj = _inproj_call(x, norm_w, scale_g, shift_g, w_in_bf)
    tq = min(LANES, t)
    madd = _sel_call(proj, nbatch=nbatch, t=t, past_ki=past_ki)
    ob, k_rows, v_rows = _attn_call(proj, madd, near_a[:, :tq], near_b[:, :tq], nbatch=nbatch, t=t,
                                    past_kv=past_kv)
    if state is None and (t // CHUNK) % MERGE_CHUNKS == 0:
        merged, s_new, tails = _gdn_merge_call(proj, ob, wa_bf, wb_bf, wconv8, lrow, drow, gw, nbatch=nbatch, t=t)
    else:
        oa, s_new, tails = _gdn_call(proj, wconv8, lrow, drow, gw, nbatch=nbatch, t=t, state=state)
        merged = _merge_call(oa, ob, wa_bf, wb_bf, proj)
    x_new = _outproj_call(merged, wo_bf, x, gate_g, final_norm_w)

    k_new = k_rows.reshape(nbatch, t, KV_HEADS, HEAD_DIM)
    v_new = v_rows.reshape(nbatch, t, KV_HEADS, HEAD_DIM)
    ki_new = proj[:, OFF_SM + SM_KI:OFF_SM + SM_KI + IDX_DIM].reshape(nbatch, t, IDX_DIM)
    conv_new = tails[:, 8 - (CONV_W - 1):, :]
    return x_new, (k_new, v_new, ki_new, s_new, conv_new)


def kernel(x_prompt, x_sample, c_prompt, c_sample, cache_k, cache_v, cache_idx_k, state_gdn, state_conv,
           norm_w, w_ada, b_ada, w_in, w_conv, a_log, dt_bias, gdn_norm_w, w_branch_a, w_branch_b,
           w_out, rel_bias, final_norm_w):
    depth = w_in.shape[0]
    bp, tp, d = x_prompt.shape
    bs, ts, _ = x_sample.shape
    past = cache_k.shape[2]
    kvw = KV_HEADS * HEAD_DIM

    mod = _ada_call(jnp.concatenate([c_prompt, c_sample], axis=0), w_ada, b_ada)
    tables = _bias_tables(rel_bias)

    xp = x_prompt.reshape(bp * tp, d)
    xs = x_sample.reshape(bs * ts, d)
    new_p, new_s = [], []
    for l in range(depth):
        wconv8 = jnp.concatenate([w_conv[l], jnp.zeros((8 - CONV_W, w_conv.shape[2]), w_conv.dtype)], axis=0)
        lw = (norm_w[l], _relayout_w_in(w_in[l]), wconv8,
              _lane_row(a_log[l], SM_AA), _lane_row(dt_bias[l], SM_AA), gdn_norm_w[l].reshape(1, GDN_DV),
              w_branch_a[l].astype(jnp.bfloat16), w_branch_b[l].astype(jnp.bfloat16),
              w_out[l].astype(jnp.bfloat16))
        fnw = final_norm_w if l == depth - 1 else None
        xp, sp = _layer(xp, mod[l, :bp], lw, tables, nbatch=bp, t=tp, final_norm_w=fnw)
        caches = (l, cache_k.reshape(depth, bs, past * KV_HEADS, HEAD_DIM),
                  cache_v.reshape(depth, bs, past * KV_HEADS, HEAD_DIM), cache_idx_k, state_gdn, state_conv[l])
        xs, ss = _layer(xs, mod[l, bp:], lw, tables, nbatch=bs, t=ts, caches=caches, final_norm_w=fnw)
        new_p.append(sp)
        new_s.append(ss)

    y_prompt = xp.reshape(bp, tp, d)
    y_sample = xs.reshape(bs, ts, d)
    outs_p = [jnp.stack([s[n] for s in new_p]) for n in range(5)]
    outs_s = [jnp.stack([s[n] for s in new_s]) for n in range(5)]
    return (y_prompt, y_sample, *outs_p, *outs_s)
```

```python
import functools
import math

import jax
import jax.numpy as jnp
from jax import lax
from jax.experimental import pallas as pl
from jax.experimental.pallas import tpu as pltpu

D_MODEL = 2048
CHUNK = 64
GDN_HEADS = 16
GDN_DK = 128
GDN_DV = 128
CONV_W = 4
ATT_HEADS = 16
KV_HEADS = 2
HEAD_DIM = 128
IDX_HEADS = 16
IDX_DIM = 64
TOPK_MAX = 256
REL_BUCKETS = 32
REL_MAX_DIST = 128
EPS = 1e-6

LANES = 128
VMEM_LIMIT = 56 * 1024 * 1024
ROW_TILE = 1024
COL_TILE = 1024
MERGE_COL_TILE = 512
OUT_ROW_TILE = 512
SEL_GROUPS = 4
KEY_CHUNK = 512
KEY_CHUNK_ODD = 384

W_QKV = 3 * GDN_HEADS * GDN_DK
OFF_QKV = 0
OFF_ZA = OFF_QKV + W_QKV
OFF_QB = OFF_ZA + D_MODEL
OFF_ZB = OFF_QB + D_MODEL
OFF_GLA = OFF_ZB + D_MODEL
OFF_GLB = OFF_GLA + D_MODEL
OFF_QI = OFF_GLB + D_MODEL
OFF_KB = OFF_QI + IDX_HEADS * IDX_DIM
OFF_VB = OFF_KB + KV_HEADS * HEAD_DIM
OFF_SM = OFF_VB + KV_HEADS * HEAD_DIM
SM_KI, SM_BA, SM_AA, SM_WI = 0, 64, 80, 96
N_PROJ = 18432
GROUP = 64

_IN_SIZES = (W_QKV, D_MODEL, GDN_HEADS, GDN_HEADS, D_MODEL, KV_HEADS * HEAD_DIM, KV_HEADS * HEAD_DIM,
             D_MODEL, IDX_HEADS * IDX_DIM, IDX_DIM, IDX_HEADS, D_MODEL, D_MODEL)


def _cparams(sem):
    return pltpu.CompilerParams(dimension_semantics=sem, vmem_limit_bytes=VMEM_LIMIT)


def _bf(x):
    return x.astype(jnp.bfloat16)


def _dot(a, b):
    return jnp.dot(a, b, preferred_element_type=jnp.float32)


def _dot_nt(a, b):
    return lax.dot_general(a, b, (((1,), (1,)), ((), ())), preferred_element_type=jnp.float32)


def _ada_kernel(c_ref, w_ref, b_ref, o_ref):
    c = c_ref[...]
    a = _bf(c * jax.nn.sigmoid(c))
    o_ref[0] = _dot(a, _bf(w_ref[0])) + b_ref[0]


def _ada_call(c_all, w_ada, b_ada):
    depth, d, n = w_ada.shape
    nb = c_all.shape[0]
    tn = COL_TILE
    return pl.pallas_call(
        _ada_kernel,
        grid=(depth, n // tn),
        in_specs=[pl.BlockSpec((nb, d), lambda l, j: (0, 0)),
                  pl.BlockSpec((1, d, tn), lambda l, j: (l, 0, j)),
                  pl.BlockSpec((1, 1, tn), lambda l, j: (l, 0, j))],
        out_specs=pl.BlockSpec((1, nb, tn), lambda l, j: (l, 0, j)),
        out_shape=jax.ShapeDtypeStruct((depth, nb, n), jnp.float32),
        compiler_params=_cparams(("arbitrary", "arbitrary")),
        name="ada_mod",
    )(c_all, w_ada, b_ada.reshape(depth, 1, n))


def _inproj_kernel(x_ref, nw_ref, sc_ref, sh_ref, w_ref, o_ref, h_ref, *, tm):
    @pl.when(pl.program_id(1) == 0)
    def _():
        nw = nw_ref[...]

        def body(g, carry):
            rows = pl.ds(pl.multiple_of(g * GROUP, GROUP), GROUP)
            x = x_ref[rows, :]
            y = x * lax.rsqrt(jnp.mean(x * x, axis=-1, keepdims=True) + EPS) * nw
            hh = y * (1.0 + sc_ref[pl.ds(g, 1), :]) + sh_ref[pl.ds(g, 1), :]
            h_ref[rows, :] = _bf(hh)
            return carry

        lax.fori_loop(0, tm // GROUP, body, 0)

    o_ref[...] = _dot(h_ref[...], w_ref[...])


def _inproj_call(x, norm_w, scale_g, shift_g, w_bf):
    m, d = x.shape
    n = w_bf.shape[1]
    tm = min(ROW_TILE, m)
    tn = COL_TILE
    gpt = tm // GROUP
    return pl.pallas_call(
        functools.partial(_inproj_kernel, tm=tm),
        grid=(m // tm, n // tn),
        in_specs=[pl.BlockSpec((tm, d), lambda i, j: (i, 0)),
                  pl.BlockSpec((1, d), lambda i, j: (0, 0)),
                  pl.BlockSpec((gpt, d), lambda i, j: (i, 0)),
                  pl.BlockSpec((gpt, d), lambda i, j: (i, 0)),
                  pl.BlockSpec((d, tn), lambda i, j: (0, j))],
        out_specs=pl.BlockSpec((tm, tn), lambda i, j: (i, j)),
        out_shape=jax.ShapeDtypeStruct((m, n), jnp.float32),
        scratch_shapes=[pltpu.VMEM((tm, d), jnp.bfloat16)],
        compiler_params=_cparams(("arbitrary", "arbitrary")),
        name="inproj",
    )(x, norm_w.reshape(1, d), scale_g, shift_g, w_bf)


def _silu(x):
    hx = 0.5 * x
    return hx * jnp.tanh(hx) + hx


def _l2norm(x):
    return x * lax.rsqrt(jnp.sum(x * x, axis=-1, keepdims=True) + EPS)


def _softplus(x):
    return jnp.maximum(x, 0.0) + jnp.log1p(jnp.exp(-jnp.abs(x)))


INV_BASE = 8
HEADS_PER_PASS = GDN_HEADS


def _unit_lower_inverses(As, ii, jj, eye):
    C = As[0].shape[0]
    sh = INV_BASE.bit_length() - 1
    Ns = [jnp.where((ii >> sh) == (jj >> sh), -A, 0.0) for A in As]
    Ps = [eye + N for N in Ns]
    m = 2
    while m < INV_BASE:
        Nbs = [_bf(N) for N in Ns]
        Ns = [_dot(Nb, Nb) for Nb in Nbs]
        Ps = [P + _dot(_bf(P), _bf(N)) for P, N in zip(Ps, Ns)]
        m *= 2
    s = INV_BASE
    while s < C:
        sh = s.bit_length() - 1
        off = ((ii >> (sh + 1)) == (jj >> (sh + 1))) & (((ii >> sh) & 1) == 1) & (((jj >> sh) & 1) == 0)
        Pbs = [_bf(P) for P in Ps]
        Xs = [_dot(Pb, _bf(jnp.where(off, A, 0.0))) for Pb, A in zip(Pbs, As)]
        Ps = [P - _dot(_bf(X), Pb) for P, X, Pb in zip(Ps, Xs, Pbs)]
        s *= 2
    return Ps


MERGE_CHUNKS = 8
MERGE_COLS = 256


def _gdn_kernel(*refs, has_state, merge=None):
    if merge is not None:
        (qkv_ref, z_ref, sm_ref, wc_ref, lrow_ref, drow_ref, gw_ref, ob_ref, wa_ref, wb_ref, ga_ref, gb_ref,
         mg_ref, sout_ref, tout_ref, S_ref, xe_ref, oa_ref) = refs
        nt, nsteps = merge
        step = pl.program_id(0)
        i = step % nt
        last = (i == nt - 1) & (step < nsteps)
    elif has_state:
        (qkv_ref, z_ref, sm_ref, wc_ref, lrow_ref, drow_ref, gw_ref, cp_ref, s0_ref,
         o_ref, sout_ref, tout_ref, S_ref, xe_ref) = refs
    else:
        (qkv_ref, z_ref, sm_ref, wc_ref, lrow_ref, drow_ref, gw_ref,
         o_ref, sout_ref, tout_ref, S_ref, xe_ref) = refs
    if merge is None:
        i = pl.program_id(1)
        last = i == pl.num_programs(1) - 1
    C = CHUNK
    hw = GDN_HEADS * GDN_DK

    if merge is not None:
        @pl.when(step == 0)
        def _():
            oa_ref[...] = jnp.zeros(oa_ref.shape, jnp.bfloat16)

        group = step // MERGE_CHUNKS
        prev = (group + 1) % 2
        ysum = [None, None]

        def merge_piece(kt):
            ks = slice(kt * MERGE_COLS, (kt + 1) * MERGE_COLS)
            for n, (lhs, w_ref) in enumerate(((oa_ref[prev, :, ks], wa_ref), (ob_ref[:, ks], wb_ref))):
                part = _dot(lhs, w_ref[ks, :])
                ysum[n] = part if ysum[n] is None else ysum[n] + part

        pieces = [functools.partial(merge_piece, kt) for kt in range(D_MODEL // MERGE_COLS)]
        o_rows = pl.ds(pl.multiple_of((step % MERGE_CHUNKS) * C, C), C)

        def store_o(cols, val):
            oa_ref[group % 2, o_rows, cols] = val
    else:
        pieces = []

        def store_o(cols, val):
            o_ref[:, cols] = val

    def filler():
        if pieces:
            pieces.pop(0)()

    @pl.when(i == 0)
    def _():
        if has_state:
            S_ref[...] = s0_ref[0, 0]
            xe_ref[0:8, :] = cp_ref[0]
        else:
            S_ref[...] = jnp.zeros(S_ref.shape, jnp.float32)
            xe_ref[0:8, :] = jnp.zeros((8, xe_ref.shape[1]), jnp.float32)

    @pl.when(i > 0)
    def _():
        xe_ref[0:8, :] = xe_ref[C:C + 8, :]

    xe_ref[8:8 + C, :] = qkv_ref[...]

    sm = sm_ref[...]
    beta_all = jax.nn.sigmoid(sm)
    g_all = -jnp.exp(lrow_ref[...]) * _softplus(sm + drow_ref[...])
    rowc = lax.broadcasted_iota(jnp.int32, (C, LANES), 0)
    gc = g_all
    s = 1
    while s < C:
        gc = gc + jnp.where(rowc >= s, pltpu.roll(gc, s, 0), 0.0)
        s *= 2
    glast = gc[C - 1:C, :]
    egc_all = jnp.exp(gc)
    ekd_all = jnp.exp(glast - gc)
    egl_all = jnp.exp(glast)
    gc_t = gc.T

    ii = lax.broadcasted_iota(jnp.int32, (C, C), 0)
    jj = lax.broadcasted_iota(jnp.int32, (C, C), 1)
    eye = jnp.where(ii == jj, 1.0, 0.0)
    gw = gw_ref[...]


    def col(p, h):
        return slice(p * hw + h * GDN_DK, p * hw + (h + 1) * GDN_DK)

    def conv(p, h):
        w = wc_ref[:, col(p, h)]
        y = xe_ref[8:8 + C, col(p, h)] * w[CONV_W - 1:CONV_W]
        for s in range(1, CONV_W):
            y = y + xe_ref[8 - s:8 - s + C, col(p, h)] * w[CONV_W - 1 - s:CONV_W - s]
        return _silu(y)

    def lane(a, l):
        return a[:, l:l + 1]

    def run(heads):
        ks, qs = [], []
        for n, h in enumerate(heads):
            ks.append(_l2norm(conv(1, h)))
            if n % 4 == 3:
                filler()
        for n, h in enumerate(heads):
            qs.append(_l2norm(conv(0, h)) * (GDN_DK ** -0.5))
            if n % 4 == 3:
                filler()
        kbs = [k * lane(beta_all, SM_BA + h) for h, k in zip(heads, ks)]
        kqs = [_dot_nt(_bf(jnp.concatenate([kb, q], axis=0)), _bf(k)) for kb, q, k in zip(kbs, qs, ks)]
        decays = [jnp.where(ii >= jj,
                            jnp.exp(jnp.minimum(lane(gc, SM_AA + h) - gc_t[SM_AA + h:SM_AA + h + 1, :], 0.0)),
                            0.0)
                  for h in heads]
        As = [jnp.where(ii > jj, kq[:C] * d, 0.0) for kq, d in zip(kqs, decays)]
        qks = [_bf(kq[C:] * d) for kq, d in zip(kqs, decays)]
        Ps = _unit_lower_inverses(As, ii, jj, eye)
        vs = [conv(2, h) for h in heads]
        rhs = [_bf(jnp.concatenate([v * lane(beta_all, SM_BA + h), kb * lane(egc_all, SM_AA + h)], axis=1))
               for h, v, kb in zip(heads, vs, kbs)]
        uws = [_dot(_bf(P), r) for P, r in zip(Ps, rhs)]
        Ss = [S_ref[h] for h in heads]
        wqs = [_dot(_bf(jnp.concatenate([uw[:, GDN_DV:], q * lane(egc_all, SM_AA + h)], axis=0)), _bf(S))
               for h, uw, q, S in zip(heads, uws, qs, Ss)]
        vnbs = [_bf(uw[:, :GDN_DV] - wq[:C]) for uw, wq in zip(uws, wqs)]
        kdts = [_bf((k * lane(ekd_all, SM_AA + h)).T) for h, k in zip(heads, ks)]
        for h, S, kdt, vnb in zip(heads, Ss, kdts, vnbs):
            S_ref[h] = S * lane(egl_all, SM_AA + h) + _dot(kdt, vnb)
        os_ = [wq[C:] + _dot(qk, vnb) for wq, qk, vnb in zip(wqs, qks, vnbs)]
        for h, o in zip(heads, os_):
            o = o * lax.rsqrt(jnp.mean(o * o, axis=-1, keepdims=True) + EPS) * gw
            z = z_ref[:, col(0, h)]
            store_o(col(0, h), _bf(o * _silu(z)))

    for h0 in range(0, GDN_HEADS, HEADS_PER_PASS):
        run(range(h0, h0 + HEADS_PER_PASS))

    if merge is not None:
        while pieces:
            filler()
        mg_ref[...] = _bf(jax.nn.sigmoid(ga_ref[...]) * ysum[0] + jax.nn.sigmoid(gb_ref[...]) * ysum[1])

    @pl.when(last)
    def _():
        sout_ref[0] = S_ref[...]
        tout_ref[0] = xe_ref[C:C + 8, :]


def _gdn_call(proj, wconv8, lrow, drow, gw, *, nbatch, t, state=None):
    nt = t // CHUNK
    hh = GDN_HEADS
    const = lambda b, i: (0, 0)
    state_spec = pl.BlockSpec((1, hh, GDN_DK, GDN_DV), lambda b, i: (b, 0, 0, 0))
    tail_spec = pl.BlockSpec((1, 8, W_QKV), lambda b, i: (b, 0, 0))
    in_specs = [pl.BlockSpec((CHUNK, W_QKV), lambda b, i: (b * nt + i, OFF_QKV // W_QKV)),
                pl.BlockSpec((CHUNK, D_MODEL), lambda b, i: (b * nt + i, OFF_ZA // D_MODEL)),
                pl.BlockSpec((CHUNK, LANES), lambda b, i: (b * nt + i, OFF_SM // LANES)),
                pl.BlockSpec((8, W_QKV), const),
                pl.BlockSpec((1, LANES), const),
                pl.BlockSpec((1, LANES), const),
                pl.BlockSpec((1, GDN_DV), const)]
    args = [proj, proj, proj, wconv8, lrow, drow, gw]
    if state is not None:
        conv_prev8, s_all, layer = state
        in_specs += [tail_spec,
                     pl.BlockSpec((1, 1, hh, GDN_DK, GDN_DV), lambda b, i: (layer, b, 0, 0, 0))]
        args += [conv_prev8, s_all]
    return pl.pallas_call(
        functools.partial(_gdn_kernel, has_state=state is not None),
        grid=(nbatch, nt),
        in_specs=in_specs,
        out_specs=[pl.BlockSpec((CHUNK, hh * GDN_DV), lambda b, i: (b * nt + i, 0)),
                   state_spec, tail_spec],
        out_shape=[jax.ShapeDtypeStruct((nbatch * t, hh * GDN_DV), jnp.bfloat16),
                   jax.ShapeDtypeStruct((nbatch, hh, GDN_DK, GDN_DV), jnp.float32),
                   jax.ShapeDtypeStruct((nbatch, 8, W_QKV), jnp.float32)],
        scratch_shapes=[pltpu.VMEM((hh, GDN_DK, GDN_DV), jnp.float32),
                        pltpu.VMEM((8 + CHUNK, W_QKV), jnp.float32)],
        compiler_params=_cparams(("arbitrary", "arbitrary")),
        name="gdn",
    )(*args)


def _gdn_merge_call(proj, ob, wa_bf, wb_bf, wconv8, lrow, drow, gw, *, nbatch, t):
    nt = t // CHUNK
    hh = GDN_HEADS
    d = D_MODEL
    nsteps = nbatch * nt
    ngroups = nsteps // MERGE_CHUNKS

    def out_group(s):
        return jnp.where(s < MERGE_CHUNKS, ngroups, s // MERGE_CHUNKS - 1)
    rows = MERGE_CHUNKS * CHUNK
    assert nt % MERGE_CHUNKS == 0 and d == MERGE_CHUNKS * MERGE_COLS

    def chunk(s):
        return jnp.minimum(s, nsteps - 1)

    def group(s):
        return jnp.maximum(s // MERGE_CHUNKS - 1, 0)

    const = lambda s: (0, 0)
    gate_spec = lambda off: pl.BlockSpec(
        (rows, MERGE_COLS), lambda s: (group(s), off // MERGE_COLS + s % MERGE_CHUNKS))
    w_spec = pl.BlockSpec((d, MERGE_COLS), lambda s: (0, s % MERGE_CHUNKS))
    in_specs = [pl.BlockSpec((CHUNK, W_QKV), lambda s: (chunk(s), OFF_QKV // W_QKV)),
                pl.BlockSpec((CHUNK, D_MODEL), lambda s: (chunk(s), OFF_ZA // D_MODEL)),
                pl.BlockSpec((CHUNK, LANES), lambda s: (chunk(s), OFF_SM // LANES)),
                pl.BlockSpec((8, W_QKV), const),
                pl.BlockSpec((1, LANES), const),
                pl.BlockSpec((1, LANES), const),
                pl.BlockSpec((1, GDN_DV), const),
                pl.BlockSpec((rows, d), lambda s: (group(s), 0)),
                w_spec, w_spec, gate_spec(OFF_GLA), gate_spec(OFF_GLB)]
    return pl.pallas_call(
        functools.partial(_gdn_kernel, has_state=False, merge=(nt, nsteps)),
        grid=(nsteps + MERGE_CHUNKS,),
        in_specs=in_specs,
        out_specs=[pl.BlockSpec((rows, MERGE_COLS), lambda s: (out_group(s), s % MERGE_CHUNKS)),
                   pl.BlockSpec((1, hh, GDN_DK, GDN_DV), lambda s: (chunk(s) // nt, 0, 0, 0)),
                   pl.BlockSpec((1, 8, W_QKV), lambda s: (chunk(s) // nt, 0, 0))],
        out_shape=[jax.ShapeDtypeStruct((nbatch * t + rows, d), jnp.bfloat16),
                   jax.ShapeDtypeStruct((nbatch, hh, GDN_DK, GDN_DV), jnp.float32),
                   jax.ShapeDtypeStruct((nbatch, 8, W_QKV), jnp.float32)],
        scratch_shapes=[pltpu.VMEM((hh, GDN_DK, GDN_DV), jnp.float32),
                        pltpu.VMEM((8 + CHUNK, W_QKV), jnp.float32),
                        pltpu.VMEM((2, rows, d), jnp.bfloat16)],
        compiler_params=_cparams(("arbitrary",)),
        name="gdn_merge",
    )(proj, proj, proj, wconv8, lrow, drow, gw, ob, wa_bf, wb_bf, proj, proj)


_INT_MIN = -2147483648
_KEY_NEG_INF = -2139095041
ATT_GROUP = ATT_HEADS // KV_HEADS
CNT_VREGS = 8


def _visible_chunks(q0, rows, nreal, kc, nkc):
    lim_max = jnp.minimum(((q0 + rows - 1) // CHUNK + 1) * CHUNK, nreal)
    return jnp.minimum((lim_max + kc - 1) // kc, nkc)


def _sel_kernel(*refs, tq, rg, npart, t_cur, past, lp, kc, topk):
    if past:
        qi_ref, smq_ref, smk_ref, kip_ref, m_ref, kibf, key_ref, qis_ref = refs
    else:
        qi_ref, smq_ref, smk_ref, m_ref, kibf, key_ref, qis_ref = refs
    i = pl.program_id(1)
    nreal = past + t_cur
    nkc = lp // kc
    nl = kc // LANES
    groups = range(rg)
    pw = tq // npart

    @pl.when(i == 0)
    def _():
        for p in range(npart):
            if past:
                kibf[p, 0:past, :] = _bf(kip_ref[0, p])
            kibf[p, past:nreal, :] = _bf(smk_ref[p * t_cur:(p + 1) * t_cur, SM_KI:SM_KI + IDX_DIM])
            if lp > nreal:
                kibf[p, nreal:lp, :] = jnp.zeros((lp - nreal, IDX_DIM), jnp.bfloat16)

    q0 = past + i * (rg * pw)
    nvis = _visible_chunks(q0, rg * pw, nreal, kc, nkc)
    tlane = lax.broadcasted_iota(jnp.int32, (1, tq), 1)
    if npart > 1:
        tlane = tlane % pw
    lims = [jnp.minimum(((q0 + g * tq + tlane) // CHUNK + 1) * CHUNK, nreal) for g in groups]

    def rows_of(g):
        return slice(g * tq, (g + 1) * tq)

    def keys_at(start, w):
        return pl.ds(pl.multiple_of(start, LANES), w)

    def pos_at(start, n, width=tq):
        return start + lax.broadcasted_iota(jnp.int32, (n, width), 0)

    trim = npart == 1 and past == 0 and rg * tq == kc
    if trim:
        nfull = i
        diag = [(g + 1) * tq for g in groups]
    else:
        nfull = nvis
        diag = None

    for g in groups:
        for p in range(npart):
            part = slice(g * tq + p * pw, g * tq + (p + 1) * pw)
            w_t = (smq_ref[part, :] * ((IDX_HEADS ** -0.5) * (IDX_DIM ** -0.5))).T
            for hh in range(IDX_HEADS):
                qis_ref[hh * pw:(hh + 1) * pw, :] = _bf(qi_ref[part, hh * IDX_DIM:(hh + 1) * IDX_DIM])
            lim = lims[g][:, p * pw:(p + 1) * pw]

            def score(start, w, masked, p=p, part=part, w_t=w_t, lim=lim):
                d = _dot_nt(kibf[p, keys_at(start, w), :], qis_ref[0:IDX_HEADS * pw, :])
                acc = jnp.zeros((w, pw), jnp.float32)
                for hh in range(IDX_HEADS):
                    acc = acc + (w_t[SM_WI + hh:SM_WI + hh + 1, :]
                                 * jnp.maximum(d[:, hh * pw:(hh + 1) * pw], 0.0))
                if masked:
                    acc = jnp.where(pos_at(start, w, pw) < lim, acc, -jnp.inf)
                bits = pltpu.bitcast(acc, jnp.int32)
                key_ref[keys_at(start, w), part] = jnp.where(bits < 0, bits ^ 0x7FFFFFFF, bits)

            lax.fori_loop(0, nfull, lambda c, carry, score=score: (score(c * kc, kc, not trim), carry)[1], 0)
            if trim:
                score(q0, diag[g], True)

    all_visible = (not trim) and past >= (nkc - 1) * kc

    def count_ge(cands):
        def count(keys, w, accs, which=groups):
            accs = list(accs)
            for g in which:
                hit = jnp.where(key_ref[keys, rows_of(g)] >= cands[g], 1.0, 0.0)
                accs[g] = accs[g] + jnp.sum(hit.reshape(w // cnt_rows, cnt_rows, tq), axis=0)
            return tuple(accs)

        cnt_rows = 8 * max(1, CNT_VREGS // rg)
        accs = tuple(jnp.zeros((cnt_rows, tq), jnp.float32) for _ in groups)
        if all_visible:
            accs = count(slice(0, lp), lp, accs)
        else:
            accs = lax.fori_loop(0, nfull, lambda c, a: count(keys_at(c * kc, kc), kc, a), accs)
            if trim:
                for g in groups:
                    accs = count(keys_at(q0, diag[g]), diag[g], accs, which=[g])
        return [jnp.sum(a, axis=0, keepdims=True) for a in accs]

    def bit_step(it, taus_u):
        cands_u = [t | lax.shift_left(jnp.int32(1), 31 - it) for t in taus_u]
        cnts = count_ge([c ^ _INT_MIN for c in cands_u])
        return tuple(jnp.where(n >= float(topk), c, t) for n, c, t in zip(cnts, cands_u, taus_u))

    taus_u = lax.fori_loop(0, 32, bit_step, tuple(jnp.zeros((1, tq), jnp.int32) for _ in groups))
    taus = [t ^ _INT_MIN for t in taus_u]
    cnts_ge = count_ge(taus)
    cnts_gt = count_ge([t + 1 for t in taus])
    needs = [float(topk) - n for n in cnts_gt]
    any_excess = jnp.int32(0)
    for g in groups:
        excess = (cnts_ge[g] > float(topk)) & (taus[g] > _KEY_NEG_INF)
        any_excess = jnp.maximum(any_excess, jnp.max(jnp.where(excess, 1, 0)))

    ea = lax.broadcasted_iota(jnp.int32, (tq, tq), 0)
    eb = lax.broadcasted_iota(jnp.int32, (tq, tq), 1)
    eye = jnp.where(ea == eb, 1.0, 0.0).astype(jnp.bfloat16)

    def store_mask(g, start, w, sel_t):
        sel = _dot_nt(eye, jnp.where(sel_t, 1.0, 0.0).astype(jnp.bfloat16))
        m_ref[rows_of(g), keys_at(start, w)] = _bf(jnp.where(sel > 0.5, 0.0, -jnp.inf))

    @pl.when(any_excess == 0)
    def _():
        def span(g, start, w, masked):
            sel_t = key_ref[keys_at(start, w), rows_of(g)] >= taus[g]
            if masked:
                sel_t = sel_t & (pos_at(start, w) < lims[g])
            store_mask(g, start, w, sel_t)

        def body(c, carry):
            for g in groups:
                span(g, c * kc, kc, not trim)
            return carry

        lax.fori_loop(0, nfull, body, 0)
        if trim:
            for g in groups:
                span(g, q0, diag[g], True)

    @pl.when(any_excess != 0)
    def _():
        la = lax.broadcasted_iota(jnp.int32, (LANES, LANES), 0)
        lb = lax.broadcasted_iota(jnp.int32, (LANES, LANES), 1)
        lower = jnp.where(la >= lb, 1.0, 0.0).astype(jnp.bfloat16)

        def block(g, start, carry):
            key = key_ref[keys_at(start, LANES), rows_of(g)]
            eq = key == taus[g]
            pref = _dot(lower, jnp.where(eq, 1.0, 0.0).astype(jnp.bfloat16)) + carry
            sel_t = ((key > taus[g]) | (eq & (pref <= needs[g]))) & (pos_at(start, LANES) < lims[g])
            store_mask(g, start, LANES, sel_t)
            return pref[LANES - 1:LANES, :]

        def body(c, carries):
            carries = list(carries)
            for j in range(nl):
                for g in groups:
                    carries[g] = block(g, c * kc + j * LANES, carries[g])
            return tuple(carries)

        carries = lax.fori_loop(0, nfull, body, tuple(jnp.zeros((1, tq), jnp.float32) for _ in groups))
        if trim:
            for g in groups:
                carry = carries[g]
                for j in range(diag[g] // LANES):
                    carry = block(g, q0 + j * LANES, carry)

    if trim:
        for g in groups:
            if diag[g] < kc:
                m_ref[rows_of(g), keys_at(q0 + diag[g], kc - diag[g])] = jnp.full(
                    (tq, kc - diag[g]), -jnp.inf, jnp.bfloat16)

    def fill(c, carry):
        m_ref[:, keys_at(c * kc, kc)] = jnp.full((rg * tq, kc), -jnp.inf, jnp.bfloat16)
        return carry

    lax.fori_loop(nfull + 1 if trim else nvis, nkc, fill, 0)


def _sel_call(proj, *, nbatch, t, past_ki=None):
    past = 0 if past_ki is None else past_ki[0].shape[2]
    _, kc, lp, topk = _attn_geometry(t, past)
    npart = LANES // t if (t < LANES and nbatch % (LANES // t) == 0) else 1
    tq = min(LANES, t * npart)
    rg = min(SEL_GROUPS, (t * npart) // tq)
    rows = rg * tq
    nr = (t * npart) // rows
    qiw = IDX_HEADS * IDX_DIM
    in_specs = [pl.BlockSpec((rows, qiw), lambda b, i: (b * nr + i, OFF_QI // qiw)),
                pl.BlockSpec((rows, LANES), lambda b, i: (b * nr + i, OFF_SM // LANES)),
                pl.BlockSpec((npart * t, LANES), lambda b, i: (b, OFF_SM // LANES))]
    args = [proj, proj, proj]
    if past:
        ki_all, layer = past_ki
        in_specs.append(pl.BlockSpec((1, npart, past, IDX_DIM), lambda b, i: (layer, b, 0, 0)))
        args.append(ki_all)
    return pl.pallas_call(
        functools.partial(_sel_kernel, tq=tq, rg=rg, npart=npart, t_cur=t, past=past, lp=lp, kc=kc, topk=topk),
        grid=(nbatch // npart, nr),
        in_specs=in_specs,
        out_specs=pl.BlockSpec((rows, lp), lambda b, i: (b * nr + i, 0)),
        out_shape=jax.ShapeDtypeStruct((nbatch * t, lp), jnp.bfloat16),
        scratch_shapes=[pltpu.VMEM((npart, lp, IDX_DIM), jnp.bfloat16),
                        pltpu.VMEM((lp, rows), jnp.int32),
                        pltpu.VMEM((IDX_HEADS * tq, IDX_DIM), jnp.bfloat16)],
        compiler_params=_cparams(("arbitrary", "arbitrary")),
        name="sel_past" if past else "sel",
    )(*args)


def _attn_geometry(t, past):
    tq = min(LANES, t)
    nreal = past + t
    kc = KEY_CHUNK if nreal % KEY_CHUNK == 0 else KEY_CHUNK_ODD
    if nreal < kc:
        kc = LANES * (-(-nreal // LANES))
    lp = kc * (-(-nreal // kc))
    assert past % LANES == 0 and past + LANES * (-(-t // LANES)) <= lp
    return tq, kc, lp, min(TOPK_MAX, nreal // 4)


def _attn_kernel(*refs, tq, t_cur, past, lp, kc):
    if past:
        (qb_ref, zb_ref, madd_ref, k_ref, v_ref, kp_ref, vp_ref, na_ref, nb_ref,
         o_ref, ko_ref, vo_ref, kbf, vbf, lg_ref, qs_ref, acc_ref, den_ref, mb_ref) = refs
    else:
        (qb_ref, zb_ref, madd_ref, k_ref, v_ref, na_ref, nb_ref,
         o_ref, ko_ref, vo_ref, kbf, vbf, lg_ref, qs_ref, acc_ref, den_ref, mb_ref) = refs
    i = pl.program_id(1)
    nreal = past + t_cur
    G = ATT_GROUP
    gt = G * tq

    @pl.when(i == 0)
    def _():
        for n in range(KV_HEADS):
            ncol = slice(n * HEAD_DIM, (n + 1) * HEAD_DIM)
            if past:
                kbf[0:past, ncol] = _bf(kp_ref[pl.ds(n, past, stride=KV_HEADS), :])
                vbf[0:past, ncol] = _bf(vp_ref[pl.ds(n, past, stride=KV_HEADS), :])
            ko_ref[pl.ds(n, t_cur, stride=KV_HEADS), :] = k_ref[:, ncol]
            vo_ref[pl.ds(n, t_cur, stride=KV_HEADS), :] = v_ref[:, ncol]
        kbf[past:nreal, :] = _bf(k_ref[...])
        vbf[past:nreal, :] = _bf(v_ref[...])
        if lp > nreal:
            kbf[nreal:lp, :] = jnp.zeros((lp - nreal, KV_HEADS * HEAD_DIM), jnp.bfloat16)
            vbf[nreal:lp, :] = jnp.zeros((lp - nreal, KV_HEADS * HEAD_DIM), jnp.bfloat16)

    q0 = past + i * tq

    far_end = jnp.maximum(q0 - LANES, 0)
    nfull = far_end // kc
    nleft = (far_end - nfull * kc) // LANES
    tail0 = pl.multiple_of(far_end, LANES)
    tw = 2 * LANES
    first = q0 == 0

    def keys_at(off, w):
        return pl.ds(pl.multiple_of(off, LANES), w)

    scale = HEAD_DIM ** -0.5
    for hd in range(ATT_HEADS):
        qs_ref[hd * tq:(hd + 1) * tq, :] = _bf(qb_ref[:, hd * HEAD_DIM:(hd + 1) * HEAD_DIM] * scale)

    for n in range(KV_HEADS):
        ncol = slice(n * HEAD_DIM, (n + 1) * HEAD_DIM)
        grows = slice(n * gt, (n + 1) * gt)

        def logits(off, w, bias=None):
            sc = _dot_nt(qs_ref[grows, :], kbf[keys_at(off, w), ncol])
            ma = madd_ref[:, keys_at(off, w)].astype(jnp.float32)
            for g in range(G):
                r = slice(g * tq, (g + 1) * tq)
                s = sc[r] + ma
                if bias is not None:
                    s = s + bias(n * G + g)
                lg_ref[r, keys_at(off, w)] = s
                mt = mb_ref[r, :]
                for j in range(w // LANES):
                    mt = jnp.maximum(mt, s[:, j * LANES:(j + 1) * LANES])
                mb_ref[r, :] = mt

        def tail_bias(hd):
            zero = jnp.zeros((tq, LANES), jnp.float32)
            return jnp.concatenate([jnp.where(first, nb_ref[hd], na_ref[hd]),
                                    jnp.where(first, zero, nb_ref[hd])], axis=1)

        def weighted_values(off, w):
            p = jnp.exp(lg_ref[:, keys_at(off, w)] - jnp.concatenate([mb_ref[...]] * (w // LANES), axis=1))
            den = den_ref[...]
            for j in range(w // LANES):
                den = den + p[:, j * LANES:(j + 1) * LANES]
            den_ref[...] = den
            acc_ref[...] = acc_ref[...] + _dot(_bf(p), vbf[keys_at(off, w), ncol])

        def walk(fn, tail_kwargs):
            lax.fori_loop(0, nfull, lambda c, carry: (fn(c * kc, kc), carry)[1], 0)
            lax.fori_loop(0, nleft, lambda b, carry: (fn(nfull * kc + b * LANES, LANES), carry)[1], 0)
            fn(tail0, tw, **tail_kwargs)

        mb_ref[...] = jnp.full((gt, LANES), -jnp.inf, jnp.float32)
        walk(logits, dict(bias=tail_bias))
        for g in range(G):
            r = slice(g * tq, (g + 1) * tq)
            mb_ref[r, :] = jnp.broadcast_to(jnp.max(mb_ref[r, :], axis=1, keepdims=True), (tq, LANES))

        acc_ref[...] = jnp.zeros((gt, HEAD_DIM), jnp.float32)
        den_ref[...] = jnp.zeros((gt, LANES), jnp.float32)
        walk(weighted_values, {})

        for g in range(G):
            r = slice(g * tq, (g + 1) * tq)
            hcol = slice((n * G + g) * HEAD_DIM, (n * G + g + 1) * HEAD_DIM)
            den = jnp.sum(den_ref[r, :], axis=1, keepdims=True)
            z = zb_ref[:, hcol]
            o_ref[:, hcol] = _bf((acc_ref[r, :] / den) * _silu(z))


def _attn_call(proj, madd, near_a, near_b, *, nbatch, t, past_kv=None):
    past = 0 if past_kv is None else past_kv[0].shape[2] // KV_HEADS
    tq, kc, lp, _ = _attn_geometry(t, past)
    nq = t // tq
    kvw = KV_HEADS * HEAD_DIM
    gt = ATT_GROUP * tq

    in_specs = [pl.BlockSpec((tq, D_MODEL), lambda b, i: (b * nq + i, OFF_QB // D_MODEL)),
                pl.BlockSpec((tq, D_MODEL), lambda b, i: (b * nq + i, OFF_ZB // D_MODEL)),
                pl.BlockSpec((tq, lp), lambda b, i: (b * nq + i, 0)),
                pl.BlockSpec((t, kvw), lambda b, i: (b, OFF_KB // kvw)),
                pl.BlockSpec((t, kvw), lambda b, i: (b, OFF_VB // kvw))]
    args = [proj, proj, madd, proj, proj]
    if past:
        k_all, v_all, layer = past_kv
        in_specs += [pl.BlockSpec((None, None, past * KV_HEADS, HEAD_DIM), lambda b, i: (layer, b, 0, 0)),
                     pl.BlockSpec((None, None, past * KV_HEADS, HEAD_DIM), lambda b, i: (layer, b, 0, 0))]
        args += [k_all, v_all]
    in_specs += [pl.BlockSpec((ATT_HEADS, tq, LANES), lambda b, i: (0, 0, 0)),
                 pl.BlockSpec((ATT_HEADS, tq, LANES), lambda b, i: (0, 0, 0))]
    args += [near_a, near_b]

    return pl.pallas_call(
        functools.partial(_attn_kernel, tq=tq, t_cur=t, past=past, lp=lp, kc=kc),
        grid=(nbatch, nq),
        in_specs=in_specs,
        out_specs=[pl.BlockSpec((tq, D_MODEL), lambda b, i: (b * nq + i, 0)),
                   pl.BlockSpec((None, t * KV_HEADS, HEAD_DIM), lambda b, i: (b, 0, 0)),
                   pl.BlockSpec((None, t * KV_HEADS, HEAD_DIM), lambda b, i: (b, 0, 0))],
        out_shape=[jax.ShapeDtypeStruct((nbatch * t, D_MODEL), jnp.bfloat16),
                   jax.ShapeDtypeStruct((nbatch, t * KV_HEADS, HEAD_DIM), jnp.float32),
                   jax.ShapeDtypeStruct((nbatch, t * KV_HEADS, HEAD_DIM), jnp.float32)],
        scratch_shapes=[pltpu.VMEM((lp, kvw), jnp.bfloat16),
                        pltpu.VMEM((lp, kvw), jnp.bfloat16),
                        pltpu.VMEM((gt, lp), jnp.float32),
                        pltpu.VMEM((ATT_HEADS * tq, HEAD_DIM), jnp.bfloat16),
                        pltpu.VMEM((gt, HEAD_DIM), jnp.float32),
                        pltpu.VMEM((gt, LANES), jnp.float32),
                        pltpu.VMEM((gt, LANES), jnp.float32)],
        compiler_params=_cparams(("arbitrary", "arbitrary")),
        name="attn_past" if past else "attn",
    )(*args)


def _merge_kernel(oa_ref, ob_ref, wa_ref, wb_ref, ga_ref, gb_ref, o_ref):
    ya = _dot(oa_ref[...], wa_ref[...])
    yb = _dot(ob_ref[...], wb_ref[...])
    o_ref[...] = _bf(jax.nn.sigmoid(ga_ref[...]) * ya + jax.nn.sigmoid(gb_ref[...]) * yb)


def _merge_call(oa, ob, wa_bf, wb_bf, proj):
    m, d = oa.shape
    tm = min(ROW_TILE, m)
    tn = MERGE_COL_TILE
    return pl.pallas_call(
        _merge_kernel,
        grid=(m // tm, d // tn),
        in_specs=[pl.BlockSpec((tm, d), lambda i, j: (i, 0)),
                  pl.BlockSpec((tm, d), lambda i, j: (i, 0)),
                  pl.BlockSpec((d, tn), lambda i, j: (0, j)),
                  pl.BlockSpec((d, tn), lambda i, j: (0, j)),
                  pl.BlockSpec((tm, tn), lambda i, j: (i, OFF_GLA // tn + j)),
                  pl.BlockSpec((tm, tn), lambda i, j: (i, OFF_GLB // tn + j))],
        out_specs=pl.BlockSpec((tm, tn), lambda i, j: (i, j)),
        out_shape=jax.ShapeDtypeStruct((m, d), jnp.bfloat16),
        compiler_params=_cparams(("arbitrary", "arbitrary")),
        name="merge",
    )(oa, ob, wa_bf, wb_bf, proj, proj)


def _outproj_kernel(*refs, tm, final_norm):
    if final_norm:
        mg_ref, w_ref, x_ref, gate_ref, nw_ref, o_ref = refs
    else:
        mg_ref, w_ref, x_ref, gate_ref, o_ref = refs
    y = _dot(mg_ref[...], w_ref[...])
    for g in range(tm // GROUP):
        r = slice(g * GROUP, (g + 1) * GROUP)
        xn = x_ref[r, :] + gate_ref[g:g + 1, :] * y[r]
        if final_norm:
            xn = xn * lax.rsqrt(jnp.mean(xn * xn, axis=-1, keepdims=True) + EPS) * nw_ref[...]
        o_ref[r, :] = xn


def _outproj_call(merged, wo_bf, x, gate_g, final_norm_w=None):
    m, d = x.shape
    tm = min(OUT_ROW_TILE, m)
    gpt = tm // GROUP
    in_specs = [pl.BlockSpec((tm, d), lambda i: (i, 0)),
                pl.BlockSpec((d, d), lambda i: (0, 0)),
                pl.BlockSpec((tm, d), lambda i: (i, 0)),
                pl.BlockSpec((gpt, d), lambda i: (i, 0))]
    args = [merged, wo_bf, x, gate_g]
    if final_norm_w is not None:
        in_specs.append(pl.BlockSpec((1, d), lambda i: (0, 0)))
        args.append(final_norm_w.reshape(1, d))
    return pl.pallas_call(
        functools.partial(_outproj_kernel, tm=tm, final_norm=final_norm_w is not None),
        grid=(m // tm,),
        in_specs=in_specs,
        out_specs=pl.BlockSpec((tm, d), lambda i: (i, 0)),
        out_shape=jax.ShapeDtypeStruct((m, d), jnp.float32),
        compiler_params=_cparams(("arbitrary",)),
        name="outproj",
    )(*args)


def _relayout_w_in(w):
    offs = [0]
    for s in _IN_SIZES:
        offs.append(offs[-1] + s)
    (qkv, za, ba, aa, qb, kb, vb, zb, qi, ki, wi, gla, glb) = [w[:, offs[n]:offs[n + 1]] for n in range(13)]
    d = w.shape[0]
    pad_sm = jnp.zeros((d, LANES - (IDX_DIM + 3 * GDN_HEADS)), w.dtype)
    cols = [qkv, za, qb, zb, gla, glb, qi, kb, vb, ki, ba, aa, wi, pad_sm]
    out = jnp.concatenate(cols, axis=1)
    pad = jnp.zeros((d, N_PROJ - out.shape[1]), w.dtype)
    return jnp.concatenate([out, pad], axis=1).astype(jnp.bfloat16)


def _rel_bucket(rel):
    nb = REL_BUCKETS // 2
    max_exact = nb // 2
    n = jnp.abs(rel)
    nf = jnp.maximum(n, 1).astype(jnp.float32)
    large = max_exact + (jnp.log(nf / max_exact) / math.log(REL_MAX_DIST / max_exact)
                         * (nb - max_exact)).astype(jnp.int32)
    large = jnp.minimum(large, nb - 1)
    return jnp.where(rel > 0, nb, 0) + jnp.where(n < max_exact, n, large)


def _bias_tables(rel_bias):
    tq = LANES
    trow = jnp.arange(tq)[:, None]
    col = jnp.arange(2 * LANES)[None, :]
    rel = (col - LANES) - trow
    bucket = _rel_bucket(rel)
    tab = sum(jnp.where(bucket == b, rel_bias[b][:, None, None], 0.0) for b in range(REL_BUCKETS))
    far = rel_bias[REL_BUCKETS // 2 - 1]
    tab = tab - far[:, None, None]
    return tab[:, :, :LANES], tab[:, :, LANES:]


def _pad_rows8(a):
    z = jnp.zeros(a.shape[:-2] + (8 - a.shape[-2], a.shape[-1]), a.dtype)
    return jnp.concatenate([z, a], axis=-2)


def _lane_row(vals, off):
    r = jnp.zeros((1, LANES), jnp.float32)
    return r.at[0, off:off + vals.shape[0]].set(vals)


def _layer(x, mod, lw, tables, *, nbatch, t, caches=None, final_norm_w=None):
    (norm_w, w_in_bf, wconv8, lrow, drow, gw, wa_bf, wb_bf, wo_bf) = lw
    near_a, near_b = tables
    d = D_MODEL
    gpb = t // GROUP

    def per_group(a):
        return jnp.broadcast_to(a[:, None, :], (nbatch, gpb, d)).reshape(nbatch * gpb, d)

    shift_g, scale_g, gate_g = [per_group(mod[:, n * d:(n + 1) * d]) for n in range(3)]
    if caches is None:
        state = past_ki = past_kv = None
    else:
        layer, k_all, v_all, ki_all, s_all, conv_prev = caches
        state = (_pad_rows8(conv_prev), s_all, layer)
        past_ki = (ki_all, layer)
        past_kv = (k_all, v_all, layer)

    proj = _inproj_call(x, norm_w, scale_g, shift_g, w_in_bf)
    tq = min(LANES, t)
    madd = _sel_call(proj, nbatch=nbatch, t=t, past_ki=past_ki)
    ob, k_rows, v_rows = _attn_call(proj, madd, near_a[:, :tq], near_b[:, :tq], nbatch=nbatch, t=t,
                                    past_kv=past_kv)
    if state is None and (t // CHUNK) % MERGE_CHUNKS == 0:
        merged, s_new, tails = _gdn_merge_call(proj, ob, wa_bf, wb_bf, wconv8, lrow, drow, gw, nbatch=nbatch, t=t)
    else:
        oa, s_new, tails = _gdn_call(proj, wconv8, lrow, drow, gw, nbatch=nbatch, t=t, state=state)
        merged = _merge_call(oa, ob, wa_bf, wb_bf, proj)
    x_new = _outproj_call(merged, wo_bf, x, gate_g, final_norm_w)

    k_new = k_rows.reshape(nbatch, t, KV_HEADS, HEAD_DIM)
    v_new = v_rows.reshape(nbatch, t, KV_HEADS, HEAD_DIM)
    ki_new = proj[:, OFF_SM + SM_KI:OFF_SM + SM_KI + IDX_DIM].reshape(nbatch, t, IDX_DIM)
    conv_new = tails[:, 8 - (CONV_W - 1):, :]
    return x_new, (k_new, v_new, ki_new, s_new, conv_new)


def kernel(x_prompt, x_sample, c_prompt, c_sample, cache_k, cache_v, cache_idx_k, state_gdn, state_conv,
           norm_w, w_ada, b_ada, w_in, w_conv, a_log, dt_bias, gdn_norm_w, w_branch_a, w_branch_b,
           w_out, rel_bias, final_norm_w):
    depth = w_in.shape[0]
    bp, tp, d = x_prompt.shape
    bs, ts, _ = x_sample.shape
    past = cache_k.shape[2]
    kvw = KV_HEADS * HEAD_DIM

    mod = _ada_call(jnp.concatenate([c_prompt, c_sample], axis=0), w_ada, b_ada)
    tables = _bias_tables(rel_bias)

    xp = x_prompt.reshape(bp * tp, d)
    xs = x_sample.reshape(bs * ts, d)
    new_p, new_s = [], []
    for l in range(depth):
        wconv8 = jnp.concatenate([w_conv[l], jnp.zeros((8 - CONV_W, w_conv.shape[2]), w_conv.dtype)], axis=0)
        lw = (norm_w[l], _relayout_w_in(w_in[l]), wconv8,
              _lane_row(a_log[l], SM_AA), _lane_row(dt_bias[l], SM_AA), gdn_norm_w[l].reshape(1, GDN_DV),
              w_branch_a[l].astype(jnp.bfloat16), w_branch_b[l].astype(jnp.bfloat16),
              w_out[l].astype(jnp.bfloat16))
        fnw = final_norm_w if l == depth - 1 else None
        xp, sp = _layer(xp, mod[l, :bp], lw, tables, nbatch=bp, t=tp, final_norm_w=fnw)
        caches = (l, cache_k.reshape(depth, bs, past * KV_HEADS, HEAD_DIM),
                  cache_v.reshape(depth, bs, past * KV_HEADS, HEAD_DIM), cache_idx_k, state_gdn, state_conv[l])
        xs, ss = _layer(xs, mod[l, bp:], lw, tables, nbatch=bs, t=ts, caches=caches, final_norm_w=fnw)
        new_p.append(sp)
        new_s.append(ss)

    y_prompt = xp.reshape(bp, tp, d)
    y_sample = xs.reshape(bs, ts, d)
    outs_p = [jnp.stack([s[n] for s in new_p]) for n in range(5)]
    outs_s = [jnp.stack([s[n] for s in new_s]) for n in range(5)]
    return (y_prompt, y_sample, *outs_p, *outs_s)
```

```python
import functools
import math

import jax
import jax.numpy as jnp
from jax import lax
from jax.experimental import pallas as pl
from jax.experimental.pallas import tpu as pltpu

D_MODEL = 2048
CHUNK = 64
GDN_HEADS = 16
GDN_DK = 128
GDN_DV = 128
CONV_W = 4
ATT_HEADS = 16
KV_HEADS = 2
HEAD_DIM = 128
IDX_HEADS = 16
IDX_DIM = 64
TOPK_MAX = 256
REL_BUCKETS = 32
REL_MAX_DIST = 128
EPS = 1e-6

LANES = 128
VMEM_LIMIT = 56 * 1024 * 1024
ROW_TILE = 1024
COL_TILE = 1024
MERGE_COL_TILE = 512
OUT_ROW_TILE = 512
SEL_GROUPS = 4
KEY_CHUNK = 512
KEY_CHUNK_ODD = 384

W_QKV = 3 * GDN_HEADS * GDN_DK
OFF_QKV = 0
OFF_ZA = OFF_QKV + W_QKV
OFF_QB = OFF_ZA + D_MODEL
OFF_ZB = OFF_QB + D_MODEL
OFF_GLA = OFF_ZB + D_MODEL
OFF_GLB = OFF_GLA + D_MODEL
OFF_QI = OFF_GLB + D_MODEL
OFF_KB = OFF_QI + IDX_HEADS * IDX_DIM
OFF_VB = OFF_KB + KV_HEADS * HEAD_DIM
OFF_SM = OFF_VB + KV_HEADS * HEAD_DIM
SM_KI, SM_BA, SM_AA, SM_WI = 0, 64, 80, 96
N_PROJ = 18432
GROUP = 64

_IN_SIZES = (W_QKV, D_MODEL, GDN_HEADS, GDN_HEADS, D_MODEL, KV_HEADS * HEAD_DIM, KV_HEADS * HEAD_DIM,
             D_MODEL, IDX_HEADS * IDX_DIM, IDX_DIM, IDX_HEADS, D_MODEL, D_MODEL)


def _cparams(sem):
    return pltpu.CompilerParams(dimension_semantics=sem, vmem_limit_bytes=VMEM_LIMIT)


def _bf(x):
    return x.astype(jnp.bfloat16)


def _dot(a, b):
    return jnp.dot(a, b, preferred_element_type=jnp.float32)


def _dot_nt(a, b):
    return lax.dot_general(a, b, (((1,), (1,)), ((), ())), preferred_element_type=jnp.float32)


def _ada_kernel(c_ref, w_ref, b_ref, o_ref):
    c = c_ref[...]
    a = _bf(c * jax.nn.sigmoid(c))
    o_ref[0] = _dot(a, _bf(w_ref[0])) + b_ref[0]


def _ada_call(c_all, w_ada, b_ada):
    depth, d, n = w_ada.shape
    nb = c_all.shape[0]
    tn = COL_TILE
    return pl.pallas_call(
        _ada_kernel,
        grid=(depth, n // tn),
        in_specs=[pl.BlockSpec((nb, d), lambda l, j: (0, 0)),
                  pl.BlockSpec((1, d, tn), lambda l, j: (l, 0, j)),
                  pl.BlockSpec((1, 1, tn), lambda l, j: (l, 0, j))],
        out_specs=pl.BlockSpec((1, nb, tn), lambda l, j: (l, 0, j)),
        out_shape=jax.ShapeDtypeStruct((depth, nb, n), jnp.float32),
        compiler_params=_cparams(("arbitrary", "arbitrary")),
        name="ada_mod",
    )(c_all, w_ada, b_ada.reshape(depth, 1, n))


def _inproj_kernel(x_ref, nw_ref, sc_ref, sh_ref, w_ref, o_ref, h_ref, *, tm):
    @pl.when(pl.program_id(1) == 0)
    def _():
        nw = nw_ref[...]

        def body(g, carry):
            rows = pl.ds(pl.multiple_of(g * GROUP, GROUP), GROUP)
            x = x_ref[rows, :]
            y = x * lax.rsqrt(jnp.mean(x * x, axis=-1, keepdims=True) + EPS) * nw
            hh = y * (1.0 + sc_ref[pl.ds(g, 1), :]) + sh_ref[pl.ds(g, 1), :]
            h_ref[rows, :] = _bf(hh)
            return carry

        lax.fori_loop(0, tm // GROUP, body, 0)

    o_ref[...] = _dot(h_ref[...], w_ref[...])


def _inproj_call(x, norm_w, scale_g, shift_g, w_bf):
    m, d = x.shape
    n = w_bf.shape[1]
    tm = min(ROW_TILE, m)
    tn = COL_TILE
    gpt = tm // GROUP
    return pl.pallas_call(
        functools.partial(_inproj_kernel, tm=tm),
        grid=(m // tm, n // tn),
        in_specs=[pl.BlockSpec((tm, d), lambda i, j: (i, 0)),
                  pl.BlockSpec((1, d), lambda i, j: (0, 0)),
                  pl.BlockSpec((gpt, d), lambda i, j: (i, 0)),
                  pl.BlockSpec((gpt, d), lambda i, j: (i, 0)),
                  pl.BlockSpec((d, tn), lambda i, j: (0, j))],
        out_specs=pl.BlockSpec((tm, tn), lambda i, j: (i, j)),
        out_shape=jax.ShapeDtypeStruct((m, n), jnp.float32),
        scratch_shapes=[pltpu.VMEM((tm, d), jnp.bfloat16)],
        compiler_params=_cparams(("arbitrary", "arbitrary")),
        name="inproj",
    )(x, norm_w.reshape(1, d), scale_g, shift_g, w_bf)


def _silu(x):
    hx = 0.5 * x
    return hx * jnp.tanh(hx) + hx


def _l2norm(x):
    return x * lax.rsqrt(jnp.sum(x * x, axis=-1, keepdims=True) + EPS)


def _softplus(x):
    return jnp.maximum(x, 0.0) + jnp.log1p(jnp.exp(-jnp.abs(x)))


INV_BASE = 8
HEADS_PER_PASS = GDN_HEADS


def _unit_lower_inverses(As, ii, jj, eye):
    C = As[0].shape[0]
    sh = INV_BASE.bit_length() - 1
    Ns = [jnp.where((ii >> sh) == (jj >> sh), -A, 0.0) for A in As]
    Ps = [eye + N for N in Ns]
    m = 2
    while m < INV_BASE:
        Nbs = [_bf(N) for N in Ns]
        Ns = [_dot(Nb, Nb) for Nb in Nbs]
        Ps = [P + _dot(_bf(P), _bf(N)) for P, N in zip(Ps, Ns)]
        m *= 2
    s = INV_BASE
    while s < C:
        sh = s.bit_length() - 1
        off = ((ii >> (sh + 1)) == (jj >> (sh + 1))) & (((ii >> sh) & 1) == 1) & (((jj >> sh) & 1) == 0)
        Pbs = [_bf(P) for P in Ps]
        Xs = [_dot(Pb, _bf(jnp.where(off, A, 0.0))) for Pb, A in zip(Pbs, As)]
        Ps = [P - _dot(_bf(X), Pb) for P, X, Pb in zip(Ps, Xs, Pbs)]
        s *= 2
    return Ps


MERGE_CHUNKS = 8
MERGE_COLS = 256


def _gdn_kernel(*refs, has_state, merge=None):
    if merge is not None:
        (qkv_ref, z_ref, sm_ref, wc_ref, lrow_ref, drow_ref, gw_ref, ob_ref, wa_ref, wb_ref, ga_ref, gb_ref,
         mg_ref, sout_ref, tout_ref, S_ref, xe_ref, oa_ref) = refs
        nt, nsteps = merge
        step = pl.program_id(0)
        i = step % nt
        last = (i == nt - 1) & (step < nsteps)
    elif has_state:
        (qkv_ref, z_ref, sm_ref, wc_ref, lrow_ref, drow_ref, gw_ref, cp_ref, s0_ref,
         o_ref, sout_ref, tout_ref, S_ref, xe_ref) = refs
    else:
        (qkv_ref, z_ref, sm_ref, wc_ref, lrow_ref, drow_ref, gw_ref,
         o_ref, sout_ref, tout_ref, S_ref, xe_ref) = refs
    if merge is None:
        i = pl.program_id(1)
        last = i == pl.num_programs(1) - 1
    C = CHUNK
    hw = GDN_HEADS * GDN_DK

    if merge is not None:
        @pl.when(step == 0)
        def _():
            oa_ref[...] = jnp.zeros(oa_ref.shape, jnp.bfloat16)

        group = step // MERGE_CHUNKS
        prev = (group + 1) % 2
        ysum = [None, None]

        def merge_piece(kt):
            ks = slice(kt * MERGE_COLS, (kt + 1) * MERGE_COLS)
            for n, (lhs, w_ref) in enumerate(((oa_ref[prev, :, ks], wa_ref), (ob_ref[:, ks], wb_ref))):
                part = _dot(lhs, w_ref[ks, :])
                ysum[n] = part if ysum[n] is None else ysum[n] + part

        pieces = [functools.partial(merge_piece, kt) for kt in range(D_MODEL // MERGE_COLS)]
        o_rows = pl.ds(pl.multiple_of((step % MERGE_CHUNKS) * C, C), C)

        def store_o(cols, val):
            oa_ref[group % 2, o_rows, cols] = val
    else:
        pieces = []

        def store_o(cols, val):
            o_ref[:, cols] = val

    def filler():
        if pieces:
            pieces.pop(0)()

    @pl.when(i == 0)
    def _():
        if has_state:
            S_ref[...] = s0_ref[0, 0]
            xe_ref[0:8, :] = cp_ref[0]
        else:
            S_ref[...] = jnp.zeros(S_ref.shape, jnp.float32)
            xe_ref[0:8, :] = jnp.zeros((8, xe_ref.shape[1]), jnp.float32)

    @pl.when(i > 0)
    def _():
        xe_ref[0:8, :] = xe_ref[C:C + 8, :]

    xe_ref[8:8 + C, :] = qkv_ref[...]

    sm = sm_ref[...]
    beta_all = jax.nn.sigmoid(sm)
    g_all = -jnp.exp(lrow_ref[...]) * _softplus(sm + drow_ref[...])
    rowc = lax.broadcasted_iota(jnp.int32, (C, LANES), 0)
    gc = g_all
    s = 1
    while s < C:
        gc = gc + jnp.where(rowc >= s, pltpu.roll(gc, s, 0), 0.0)
        s *= 2
    glast = gc[C - 1:C, :]
    egc_all = jnp.exp(gc)
    ekd_all = jnp.exp(glast - gc)
    egl_all = jnp.exp(glast)
    gc_t = gc.T

    ii = lax.broadcasted_iota(jnp.int32, (C, C), 0)
    jj = lax.broadcasted_iota(jnp.int32, (C, C), 1)
    eye = jnp.where(ii == jj, 1.0, 0.0)
    gw = gw_ref[...]


    def col(p, h):
        return slice(p * hw + h * GDN_DK, p * hw + (h + 1) * GDN_DK)

    def conv(p, h):
        w = wc_ref[:, col(p, h)]
        y = xe_ref[8:8 + C, col(p, h)] * w[CONV_W - 1:CONV_W]
        for s in range(1, CONV_W):
            y = y + xe_ref[8 - s:8 - s + C, col(p, h)] * w[CONV_W - 1 - s:CONV_W - s]
        return _silu(y)

    def lane(a, l):
        return a[:, l:l + 1]

    def run(heads):
        ks, qs = [], []
        for n, h in enumerate(heads):
            ks.append(_l2norm(conv(1, h)))
            if n % 4 == 3:
                filler()
        for n, h in enumerate(heads):
            qs.append(_l2norm(conv(0, h)) * (GDN_DK ** -0.5))
            if n % 4 == 3:
                filler()
        kbs = [k * lane(beta_all, SM_BA + h) for h, k in zip(heads, ks)]
        kqs = [_dot_nt(_bf(jnp.concatenate([kb, q], axis=0)), _bf(k)) for kb, q, k in zip(kbs, qs, ks)]
        decays = [jnp.where(ii >= jj,
                            jnp.exp(jnp.minimum(lane(gc, SM_AA + h) - gc_t[SM_AA + h:SM_AA + h + 1, :], 0.0)),
                            0.0)
                  for h in heads]
        As = [jnp.where(ii > jj, kq[:C] * d, 0.0) for kq, d in zip(kqs, decays)]
        qks = [_bf(kq[C:] * d) for kq, d in zip(kqs, decays)]
        Ps = _unit_lower_inverses(As, ii, jj, eye)
        vs = [conv(2, h) for h in heads]
        rhs = [_bf(jnp.concatenate([v * lane(beta_all, SM_BA + h), kb * lane(egc_all, SM_AA + h)], axis=1))
               for h, v, kb in zip(heads, vs, kbs)]
        uws = [_dot(_bf(P), r) for P, r in zip(Ps, rhs)]
        Ss = [S_ref[h] for h in heads]
        wqs = [_dot(_bf(jnp.concatenate([uw[:, GDN_DV:], q * lane(egc_all, SM_AA + h)], axis=0)), _bf(S))
               for h, uw, q, S in zip(heads, uws, qs, Ss)]
        vnbs = [_bf(uw[:, :GDN_DV] - wq[:C]) for uw, wq in zip(uws, wqs)]
        kdts = [_bf((k * lane(ekd_all, SM_AA + h)).T) for h, k in zip(heads, ks)]
        for h, S, kdt, vnb in zip(heads, Ss, kdts, vnbs):
            S_ref[h] = S * lane(egl_all, SM_AA + h) + _dot(kdt, vnb)
        os_ = [wq[C:] + _dot(qk, vnb) for wq, qk, vnb in zip(wqs, qks, vnbs)]
        for h, o in zip(heads, os_):
            o = o * lax.rsqrt(jnp.mean(o * o, axis=-1, keepdims=True) + EPS) * gw
            z = z_ref[:, col(0, h)]
            store_o(col(0, h), _bf(o * _silu(z)))

    for h0 in range(0, GDN_HEADS, HEADS_PER_PASS):
        run(range(h0, h0 + HEADS_PER_PASS))

    if merge is not None:
        while pieces:
            filler()
        mg_ref[...] = _bf(jax.nn.sigmoid(ga_ref[...]) * ysum[0] + jax.nn.sigmoid(gb_ref[...]) * ysum[1])

    @pl.when(last)
    def _():
        sout_ref[0] = S_ref[...]
        tout_ref[0] = xe_ref[C:C + 8, :]


def _gdn_call(proj, wconv8, lrow, drow, gw, *, nbatch, t, state=None):
    nt = t // CHUNK
    hh = GDN_HEADS
    const = lambda b, i: (0, 0)
    state_spec = pl.BlockSpec((1, hh, GDN_DK, GDN_DV), lambda b, i: (b, 0, 0, 0))
    tail_spec = pl.BlockSpec((1, 8, W_QKV), lambda b, i: (b, 0, 0))
    in_specs = [pl.BlockSpec((CHUNK, W_QKV), lambda b, i: (b * nt + i, OFF_QKV // W_QKV)),
                pl.BlockSpec((CHUNK, D_MODEL), lambda b, i: (b * nt + i, OFF_ZA // D_MODEL)),
                pl.BlockSpec((CHUNK, LANES), lambda b, i: (b * nt + i, OFF_SM // LANES)),
                pl.BlockSpec((8, W_QKV), const),
                pl.BlockSpec((1, LANES), const),
                pl.BlockSpec((1, LANES), const),
                pl.BlockSpec((1, GDN_DV), const)]
    args = [proj, proj, proj, wconv8, lrow, drow, gw]
    if state is not None:
        conv_prev8, s_all, layer = state
        in_specs += [tail_spec,
                     pl.BlockSpec((1, 1, hh, GDN_DK, GDN_DV), lambda b, i: (layer, b, 0, 0, 0))]
        args += [conv_prev8, s_all]
    return pl.pallas_call(
        functools.partial(_gdn_kernel, has_state=state is not None),
        grid=(nbatch, nt),
        in_specs=in_specs,
        out_specs=[pl.BlockSpec((CHUNK, hh * GDN_DV), lambda b, i: (b * nt + i, 0)),
                   state_spec, tail_spec],
        out_shape=[jax.ShapeDtypeStruct((nbatch * t, hh * GDN_DV), jnp.bfloat16),
                   jax.ShapeDtypeStruct((nbatch, hh, GDN_DK, GDN_DV), jnp.float32),
                   jax.ShapeDtypeStruct((nbatch, 8, W_QKV), jnp.float32)],
        scratch_shapes=[pltpu.VMEM((hh, GDN_DK, GDN_DV), jnp.float32),
                        pltpu.VMEM((8 + CHUNK, W_QKV), jnp.float32)],
        compiler_params=_cparams(("arbitrary", "arbitrary")),
        name="gdn",
    )(*args)


def _gdn_merge_call(proj, ob, wa_bf, wb_bf, wconv8, lrow, drow, gw, *, nbatch, t):
    nt = t // CHUNK
    hh = GDN_HEADS
    d = D_MODEL
    nsteps = nbatch * nt
    ngroups = nsteps // MERGE_CHUNKS

    def out_group(s):
        return jnp.where(s < MERGE_CHUNKS, ngroups, s // MERGE_CHUNKS - 1)
    rows = MERGE_CHUNKS * CHUNK
    assert nt % MERGE_CHUNKS == 0 and d == MERGE_CHUNKS * MERGE_COLS

    def chunk(s):
        return jnp.minimum(s, nsteps - 1)

    def group(s):
        return jnp.maximum(s // MERGE_CHUNKS - 1, 0)

    const = lambda s: (0, 0)
    gate_spec = lambda off: pl.BlockSpec(
        (rows, MERGE_COLS), lambda s: (group(s), off // MERGE_COLS + s % MERGE_CHUNKS))
    w_spec = pl.BlockSpec((d, MERGE_COLS), lambda s: (0, s % MERGE_CHUNKS))
    in_specs = [pl.BlockSpec((CHUNK, W_QKV), lambda s: (chunk(s), OFF_QKV // W_QKV)),
                pl.BlockSpec((CHUNK, D_MODEL), lambda s: (chunk(s), OFF_ZA // D_MODEL)),
                pl.BlockSpec((CHUNK, LANES), lambda s: (chunk(s), OFF_SM // LANES)),
                pl.BlockSpec((8, W_QKV), const),
                pl.BlockSpec((1, LANES), const),
                pl.BlockSpec((1, LANES), const),
                pl.BlockSpec((1, GDN_DV), const),
                pl.BlockSpec((rows, d), lambda s: (group(s), 0)),
                w_spec, w_spec, gate_spec(OFF_GLA), gate_spec(OFF_GLB)]
    return pl.pallas_call(
        functools.partial(_gdn_kernel, has_state=False, merge=(nt, nsteps)),
        grid=(nsteps + MERGE_CHUNKS,),
        in_specs=in_specs,
        out_specs=[pl.BlockSpec((rows, MERGE_COLS), lambda s: (out_group(s), s % MERGE_CHUNKS)),
                   pl.BlockSpec((1, hh, GDN_DK, GDN_DV), lambda s: (chunk(s) // nt, 0, 0, 0)),
                   pl.BlockSpec((1, 8, W_QKV), lambda s: (chunk(s) // nt, 0, 0))],
        out_shape=[jax.ShapeDtypeStruct((nbatch * t + rows, d), jnp.bfloat16),
                   jax.ShapeDtypeStruct((nbatch, hh, GDN_DK, GDN_DV), jnp.float32),
                   jax.ShapeDtypeStruct((nbatch, 8, W_QKV), jnp.float32)],
        scratch_shapes=[pltpu.VMEM((hh, GDN_DK, GDN_DV), jnp.float32),
                        pltpu.VMEM((8 + CHUNK, W_QKV), jnp.float32),
                        pltpu.VMEM((2, rows, d), jnp.bfloat16)],
        compiler_params=_cparams(("arbitrary",)),
        name="gdn_merge",
    )(proj, proj, proj, wconv8, lrow, drow, gw, ob, wa_bf, wb_bf, proj, proj)


_INT_MIN = -2147483648
_KEY_NEG_INF = -2139095041
ATT_GROUP = ATT_HEADS // KV_HEADS
CNT_VREGS = 8


def _visible_chunks(q0, rows, nreal, kc, nkc):
    lim_max = jnp.minimum(((q0 + rows - 1) // CHUNK + 1) * CHUNK, nreal)
    return jnp.minimum((lim_max + kc - 1) // kc, nkc)


def _sel_kernel(*refs, tq, rg, npart, t_cur, past, lp, kc, topk):
    if past:
        qi_ref, smq_ref, smk_ref, kip_ref, m_ref, kibf, key_ref, qis_ref = refs
    else:
        qi_ref, smq_ref, smk_ref, m_ref, kibf, key_ref, qis_ref = refs
    i = pl.program_id(1)
    nreal = past + t_cur
    nkc = lp // kc
    nl = kc // LANES
    groups = range(rg)
    pw = tq // npart

    @pl.when(i == 0)
    def _():
        for p in range(npart):
            if past:
                kibf[p, 0:past, :] = _bf(kip_ref[0, p])
            kibf[p, past:nreal, :] = _bf(smk_ref[p * t_cur:(p + 1) * t_cur, SM_KI:SM_KI + IDX_DIM])
            if lp > nreal:
                kibf[p, nreal:lp, :] = jnp.zeros((lp - nreal, IDX_DIM), jnp.bfloat16)

    q0 = past + i * (rg * pw)
    nvis = _visible_chunks(q0, rg * pw, nreal, kc, nkc)
    tlane = lax.broadcasted_iota(jnp.int32, (1, tq), 1)
    if npart > 1:
        tlane = tlane % pw
    lims = [jnp.minimum(((q0 + g * tq + tlane) // CHUNK + 1) * CHUNK, nreal) for g in groups]

    def rows_of(g):
        return slice(g * tq, (g + 1) * tq)

    def keys_at(start, w):
        return pl.ds(pl.multiple_of(start, LANES), w)

    def pos_at(start, n, width=tq):
        return start + lax.broadcasted_iota(jnp.int32, (n, width), 0)

    trim = npart == 1 and past == 0 and rg * tq == kc
    if trim:
        nfull = i
        diag = [(g + 1) * tq for g in groups]
    else:
        nfull = nvis
        diag = None

    for g in groups:
        for p in range(npart):
            part = slice(g * tq + p * pw, g * tq + (p + 1) * pw)
            w_t = (smq_ref[part, :] * ((IDX_HEADS ** -0.5) * (IDX_DIM ** -0.5))).T
            for hh in range(IDX_HEADS):
                qis_ref[hh * pw:(hh + 1) * pw, :] = _bf(qi_ref[part, hh * IDX_DIM:(hh + 1) * IDX_DIM])
            lim = lims[g][:, p * pw:(p + 1) * pw]

            def score(start, w, masked, p=p, part=part, w_t=w_t, lim=lim):
                d = _dot_nt(kibf[p, keys_at(start, w), :], qis_ref[0:IDX_HEADS * pw, :])
                acc = jnp.zeros((w, pw), jnp.float32)
                for hh in range(IDX_HEADS):
                    acc = acc + (w_t[SM_WI + hh:SM_WI + hh + 1, :]
                                 * jnp.maximum(d[:, hh * pw:(hh + 1) * pw], 0.0))
                if masked:
                    acc = jnp.where(pos_at(start, w, pw) < lim, acc, -jnp.inf)
                bits = pltpu.bitcast(acc, jnp.int32)
                key_ref[keys_at(start, w), part] = jnp.where(bits < 0, bits ^ 0x7FFFFFFF, bits)

            lax.fori_loop(0, nfull, lambda c, carry, score=score: (score(c * kc, kc, not trim), carry)[1], 0)
            if trim:
                score(q0, diag[g], True)

    all_visible = (not trim) and past >= (nkc - 1) * kc

    def count_ge(cands):
        def count(keys, w, accs, which=groups):
            accs = list(accs)
            for g in which:
                hit = jnp.where(key_ref[keys, rows_of(g)] >= cands[g], 1.0, 0.0)
                accs[g] = accs[g] + jnp.sum(hit.reshape(w // cnt_rows, cnt_rows, tq), axis=0)
            return tuple(accs)

        cnt_rows = 8 * max(1, CNT_VREGS // rg)
        accs = tuple(jnp.zeros((cnt_rows, tq), jnp.float32) for _ in groups)
        if all_visible:
            accs = count(slice(0, lp), lp, accs)
        else:
            accs = lax.fori_loop(0, nfull, lambda c, a: count(keys_at(c * kc, kc), kc, a), accs)
            if trim:
                for g in groups:
                    accs = count(keys_at(q0, diag[g]), diag[g], accs, which=[g])
        return [jnp.sum(a, axis=0, keepdims=True) for a in accs]

    def bit_step(it, taus_u):
        cands_u = [t | lax.shift_left(jnp.int32(1), 31 - it) for t in taus_u]
        cnts = count_ge([c ^ _INT_MIN for c in cands_u])
        return tuple(jnp.where(n >= float(topk), c, t) for n, c, t in zip(cnts, cands_u, taus_u))

    taus_u = lax.fori_loop(0, 32, bit_step, tuple(jnp.zeros((1, tq), jnp.int32) for _ in groups))
    taus = [t ^ _INT_MIN for t in taus_u]
    cnts_ge = count_ge(taus)
    cnts_gt = count_ge([t + 1 for t in taus])
    needs = [float(topk) - n for n in cnts_gt]
    any_excess = jnp.int32(0)
    for g in groups:
        excess = (cnts_ge[g] > float(topk)) & (taus[g] > _KEY_NEG_INF)
        any_excess = jnp.maximum(any_excess, jnp.max(jnp.where(excess, 1, 0)))

    ea = lax.broadcasted_iota(jnp.int32, (tq, tq), 0)
    eb = lax.broadcasted_iota(jnp.int32, (tq, tq), 1)
    eye = jnp.where(ea == eb, 1.0, 0.0).astype(jnp.bfloat16)

    def store_mask(g, start, w, sel_t):
        sel = _dot_nt(eye, jnp.where(sel_t, 1.0, 0.0).astype(jnp.bfloat16))
        m_ref[rows_of(g), keys_at(start, w)] = _bf(jnp.where(sel > 0.5, 0.0, -jnp.inf))

    @pl.when(any_excess == 0)
    def _():
        def span(g, start, w, masked):
            sel_t = key_ref[keys_at(start, w), rows_of(g)] >= taus[g]
            if masked:
                sel_t = sel_t & (pos_at(start, w) < lims[g])
            store_mask(g, start, w, sel_t)

        def body(c, carry):
            for g in groups:
                span(g, c * kc, kc, not trim)
            return carry

        lax.fori_loop(0, nfull, body, 0)
        if trim:
            for g in groups:
                span(g, q0, diag[g], True)

    @pl.when(any_excess != 0)
    def _():
        la = lax.broadcasted_iota(jnp.int32, (LANES, LANES), 0)
        lb = lax.broadcasted_iota(jnp.int32, (LANES, LANES), 1)
        lower = jnp.where(la >= lb, 1.0, 0.0).astype(jnp.bfloat16)

        def block(g, start, carry):
            key = key_ref[keys_at(start, LANES), rows_of(g)]
            eq = key == taus[g]
            pref = _dot(lower, jnp.where(eq, 1.0, 0.0).astype(jnp.bfloat16)) + carry
            sel_t = ((key > taus[g]) | (eq & (pref <= needs[g]))) & (pos_at(start, LANES) < lims[g])
            store_mask(g, start, LANES, sel_t)
            return pref[LANES - 1:LANES, :]

        def body(c, carries):
            carries = list(carries)
            for j in range(nl):
                for g in groups:
                    carries[g] = block(g, c * kc + j * LANES, carries[g])
            return tuple(carries)

        carries = lax.fori_loop(0, nfull, body, tuple(jnp.zeros((1, tq), jnp.float32) for _ in groups))
        if trim:
            for g in groups:
                carry = carries[g]
                for j in range(diag[g] // LANES):
                    carry = block(g, q0 + j * LANES, carry)

    if trim:
        for g in groups:
            if diag[g] < kc:
                m_ref[rows_of(g), keys_at(q0 + diag[g], kc - diag[g])] = jnp.full(
                    (tq, kc - diag[g]), -jnp.inf, jnp.bfloat16)

    def fill(c, carry):
        m_ref[:, keys_at(c * kc, kc)] = jnp.full((rg * tq, kc), -jnp.inf, jnp.bfloat16)
        return carry

    lax.fori_loop(nfull + 1 if trim else nvis, nkc, fill, 0)


def _sel_call(proj, *, nbatch, t, past_ki=None):
    past = 0 if past_ki is None else past_ki[0].shape[2]
    _, kc, lp, topk = _attn_geometry(t, past)
    npart = LANES // t if (t < LANES and nbatch % (LANES // t) == 0) else 1
    tq = min(LANES, t * npart)
    rg = min(SEL_GROUPS, (t * npart) // tq)
    rows = rg * tq
    nr = (t * npart) // rows
    qiw = IDX_HEADS * IDX_DIM
    in_specs = [pl.BlockSpec((rows, qiw), lambda b, i: (b * nr + i, OFF_QI // qiw)),
                pl.BlockSpec((rows, LANES), lambda b, i: (b * nr + i, OFF_SM // LANES)),
                pl.BlockSpec((npart * t, LANES), lambda b, i: (b, OFF_SM // LANES))]
    args = [proj, proj, proj]
    if past:
        ki_all, layer = past_ki
        in_specs.append(pl.BlockSpec((1, npart, past, IDX_DIM), lambda b, i: (layer, b, 0, 0)))
        args.append(ki_all)
    return pl.pallas_call(
        functools.partial(_sel_kernel, tq=tq, rg=rg, npart=npart, t_cur=t, past=past, lp=lp, kc=kc, topk=topk),
        grid=(nbatch // npart, nr),
        in_specs=in_specs,
        out_specs=pl.BlockSpec((rows, lp), lambda b, i: (b * nr + i, 0)),
        out_shape=jax.ShapeDtypeStruct((nbatch * t, lp), jnp.bfloat16),
        scratch_shapes=[pltpu.VMEM((npart, lp, IDX_DIM), jnp.bfloat16),
                        pltpu.VMEM((lp, rows), jnp.int32),
                        pltpu.VMEM((IDX_HEADS * tq, IDX_DIM), jnp.bfloat16)],
        compiler_params=_cparams(("arbitrary", "arbitrary")),
        name="sel_past" if past else "sel",
    )(*args)


def _attn_geometry(t, past):
    tq = min(LANES, t)
    nreal = past + t
    kc = KEY_CHUNK if nreal % KEY_CHUNK == 0 else KEY_CHUNK_ODD
    if nreal < kc:
        kc = LANES * (-(-nreal // LANES))
    lp = kc * (-(-nreal // kc))
    assert past % LANES == 0 and past + LANES * (-(-t // LANES)) <= lp
    return tq, kc, lp, min(TOPK_MAX, nreal // 4)


def _attn_kernel(*refs, tq, t_cur, past, lp, kc):
    if past:
        (qb_ref, zb_ref, madd_ref, k_ref, v_ref, kp_ref, vp_ref, na_ref, nb_ref,
         o_ref, ko_ref, vo_ref, kbf, vbf, lg_ref, qs_ref, acc_ref, den_ref, mb_ref) = refs
    else:
        (qb_ref, zb_ref, madd_ref, k_ref, v_ref, na_ref, nb_ref,
         o_ref, ko_ref, vo_ref, kbf, vbf, lg_ref, qs_ref, acc_ref, den_ref, mb_ref) = refs
    i = pl.program_id(1)
    nreal = past + t_cur
    G = ATT_GROUP
    gt = G * tq

    @pl.when(i == 0)
    def _():
        for n in range(KV_HEADS):
            ncol = slice(n * HEAD_DIM, (n + 1) * HEAD_DIM)
            if past:
                kbf[0:past, ncol] = _bf(kp_ref[pl.ds(n, past, stride=KV_HEADS), :])
                vbf[0:past, ncol] = _bf(vp_ref[pl.ds(n, past, stride=KV_HEADS), :])
            ko_ref[pl.ds(n, t_cur, stride=KV_HEADS), :] = k_ref[:, ncol]
            vo_ref[pl.ds(n, t_cur, stride=KV_HEADS), :] = v_ref[:, ncol]
        kbf[past:nreal, :] = _bf(k_ref[...])
        vbf[past:nreal, :] = _bf(v_ref[...])
        if lp > nreal:
            kbf[nreal:lp, :] = jnp.zeros((lp - nreal, KV_HEADS * HEAD_DIM), jnp.bfloat16)
            vbf[nreal:lp, :] = jnp.zeros((lp - nreal, KV_HEADS * HEAD_DIM), jnp.bfloat16)

    q0 = past + i * tq

    far_end = jnp.maximum(q0 - LANES, 0)
    nfull = far_end // kc
    nleft = (far_end - nfull * kc) // LANES
    tw = 2 * LANES
    first = q0 == 0

    def keys_at(off, w):
        return pl.ds(pl.multiple_of(off, LANES), w)

    scale = HEAD_DIM ** -0.5
    for hd in range(ATT_HEADS):
        qs_ref[hd * tq:(hd + 1) * tq, :] = _bf(qb_ref[:, hd * HEAD_DIM:(hd + 1) * HEAD_DIM] * scale)

    for n in range(KV_HEADS):
        ncol = slice(n * HEAD_DIM, (n + 1) * HEAD_DIM)
        grows = slice(n * gt, (n + 1) * gt)

        def logits(off, w, bias=None):
            sc = _dot_nt(qs_ref[grows, :], kbf[keys_at(off, w), ncol])
            ma = madd_ref[:, keys_at(off, w)].astype(jnp.float32)
            for g in range(G):
                r = slice(g * tq, (g + 1) * tq)
                s = sc[r] + ma
                if bias is not None:
                    sb = s[:, w - tw:] + bias(n * G + g)
                    s = sb if w == tw else jnp.concatenate([s[:, :w - tw], sb], axis=1)
                lg_ref[r, keys_at(off, w)] = s
                mt = mb_ref[r, :]
                for j in range(w // LANES):
                    mt = jnp.maximum(mt, s[:, j * LANES:(j + 1) * LANES])
                mb_ref[r, :] = mt

        def tail_bias(hd):
            zero = jnp.zeros((tq, LANES), jnp.float32)
            return jnp.concatenate([jnp.where(first, nb_ref[hd], na_ref[hd]),
                                    jnp.where(first, zero, nb_ref[hd])], axis=1)

        def weighted_values(off, w):
            p = jnp.exp(lg_ref[:, keys_at(off, w)] - jnp.concatenate([mb_ref[...]] * (w // LANES), axis=1))
            den = den_ref[...]
            for j in range(w // LANES):
                den = den + p[:, j * LANES:(j + 1) * LANES]
            den_ref[...] = den
            acc_ref[...] = acc_ref[...] + _dot(_bf(p), vbf[keys_at(off, w), ncol])

        def walk(fn, tail_kwargs):
            lax.fori_loop(0, nfull, lambda c, carry: (fn(c * kc, kc), carry)[1], 0)
            for v in range(min(kc, lp - tw + LANES) // LANES):
                @pl.when(nleft == v)
                def _(v=v):
                    fn(nfull * kc, v * LANES + tw, **tail_kwargs)

        mb_ref[...] = jnp.full((gt, LANES), -jnp.inf, jnp.float32)
        walk(logits, dict(bias=tail_bias))
        for g in range(G):
            r = slice(g * tq, (g + 1) * tq)
            mb_ref[r, :] = jnp.broadcast_to(jnp.max(mb_ref[r, :], axis=1, keepdims=True), (tq, LANES))

        acc_ref[...] = jnp.zeros((gt, HEAD_DIM), jnp.float32)
        den_ref[...] = jnp.zeros((gt, LANES), jnp.float32)
        walk(weighted_values, {})

        for g in range(G):
            r = slice(g * tq, (g + 1) * tq)
            hcol = slice((n * G + g) * HEAD_DIM, (n * G + g + 1) * HEAD_DIM)
            den = jnp.sum(den_ref[r, :], axis=1, keepdims=True)
            z = zb_ref[:, hcol]
            o_ref[:, hcol] = _bf((acc_ref[r, :] / den) * _silu(z))


def _attn_call(proj, madd, near_a, near_b, *, nbatch, t, past_kv=None):
    past = 0 if past_kv is None else past_kv[0].shape[2] // KV_HEADS
    tq, kc, lp, _ = _attn_geometry(t, past)
    nq = t // tq
    kvw = KV_HEADS * HEAD_DIM
    gt = ATT_GROUP * tq

    in_specs = [pl.BlockSpec((tq, D_MODEL), lambda b, i: (b * nq + i, OFF_QB // D_MODEL)),
                pl.BlockSpec((tq, D_MODEL), lambda b, i: (b * nq + i, OFF_ZB // D_MODEL)),
                pl.BlockSpec((tq, lp), lambda b, i: (b * nq + i, 0)),
                pl.BlockSpec((t, kvw), lambda b, i: (b, OFF_KB // kvw)),
                pl.BlockSpec((t, kvw), lambda b, i: (b, OFF_VB // kvw))]
    args = [proj, proj, madd, proj, proj]
    if past:
        k_all, v_all, layer = past_kv
        in_specs += [pl.BlockSpec((None, None, past * KV_HEADS, HEAD_DIM), lambda b, i: (layer, b, 0, 0)),
                     pl.BlockSpec((None, None, past * KV_HEADS, HEAD_DIM), lambda b, i: (layer, b, 0, 0))]
        args += [k_all, v_all]
    in_specs += [pl.BlockSpec((ATT_HEADS, tq, LANES), lambda b, i: (0, 0, 0)),
                 pl.BlockSpec((ATT_HEADS, tq, LANES), lambda b, i: (0, 0, 0))]
    args += [near_a, near_b]

    return pl.pallas_call(
        functools.partial(_attn_kernel, tq=tq, t_cur=t, past=past, lp=lp, kc=kc),
        grid=(nbatch, nq),
        in_specs=in_specs,
        out_specs=[pl.BlockSpec((tq, D_MODEL), lambda b, i: (b * nq + i, 0)),
                   pl.BlockSpec((None, t * KV_HEADS, HEAD_DIM), lambda b, i: (b, 0, 0)),
                   pl.BlockSpec((None, t * KV_HEADS, HEAD_DIM), lambda b, i: (b, 0, 0))],
        out_shape=[jax.ShapeDtypeStruct((nbatch * t, D_MODEL), jnp.bfloat16),
                   jax.ShapeDtypeStruct((nbatch, t * KV_HEADS, HEAD_DIM), jnp.float32),
                   jax.ShapeDtypeStruct((nbatch, t * KV_HEADS, HEAD_DIM), jnp.float32)],
        scratch_shapes=[pltpu.VMEM((lp, kvw), jnp.bfloat16),
                        pltpu.VMEM((lp, kvw), jnp.bfloat16),
                        pltpu.VMEM((gt, lp), jnp.float32),
                        pltpu.VMEM((ATT_HEADS * tq, HEAD_DIM), jnp.bfloat16),
                        pltpu.VMEM((gt, HEAD_DIM), jnp.float32),
                        pltpu.VMEM((gt, LANES), jnp.float32),
                        pltpu.VMEM((gt, LANES), jnp.float32)],
        compiler_params=_cparams(("arbitrary", "arbitrary")),
        name="attn_past" if past else "attn",
    )(*args)


def _merge_kernel(oa_ref, ob_ref, wa_ref, wb_ref, ga_ref, gb_ref, o_ref):
    ya = _dot(oa_ref[...], wa_ref[...])
    yb = _dot(ob_ref[...], wb_ref[...])
    o_ref[...] = _bf(jax.nn.sigmoid(ga_ref[...]) * ya + jax.nn.sigmoid(gb_ref[...]) * yb)


def _merge_call(oa, ob, wa_bf, wb_bf, proj):
    m, d = oa.shape
    tm = min(ROW_TILE, m)
    tn = MERGE_COL_TILE
    return pl.pallas_call(
        _merge_kernel,
        grid=(m // tm, d // tn),
        in_specs=[pl.BlockSpec((tm, d), lambda i, j: (i, 0)),
                  pl.BlockSpec((tm, d), lambda i, j: (i, 0)),
                  pl.BlockSpec((d, tn), lambda i, j: (0, j)),
                  pl.BlockSpec((d, tn), lambda i, j: (0, j)),
                  pl.BlockSpec((tm, tn), lambda i, j: (i, OFF_GLA // tn + j)),
                  pl.BlockSpec((tm, tn), lambda i, j: (i, OFF_GLB // tn + j))],
        out_specs=pl.BlockSpec((tm, tn), lambda i, j: (i, j)),
        out_shape=jax.ShapeDtypeStruct((m, d), jnp.bfloat16),
        compiler_params=_cparams(("arbitrary", "arbitrary")),
        name="merge",
    )(oa, ob, wa_bf, wb_bf, proj, proj)


def _outproj_kernel(*refs, tm, final_norm):
    if final_norm:
        mg_ref, w_ref, x_ref, gate_ref, nw_ref, o_ref = refs
    else:
        mg_ref, w_ref, x_ref, gate_ref, o_ref = refs
    y = _dot(mg_ref[...], w_ref[...])
    for g in range(tm // GROUP):
        r = slice(g * GROUP, (g + 1) * GROUP)
        xn = x_ref[r, :] + gate_ref[g:g + 1, :] * y[r]
        if final_norm:
            xn = xn * lax.rsqrt(jnp.mean(xn * xn, axis=-1, keepdims=True) + EPS) * nw_ref[...]
        o_ref[r, :] = xn


def _outproj_call(merged, wo_bf, x, gate_g, final_norm_w=None):
    m, d = x.shape
    tm = min(OUT_ROW_TILE, m)
    gpt = tm // GROUP
    in_specs = [pl.BlockSpec((tm, d), lambda i: (i, 0)),
                pl.BlockSpec((d, d), lambda i: (0, 0)),
                pl.BlockSpec((tm, d), lambda i: (i, 0)),
                pl.BlockSpec((gpt, d), lambda i: (i, 0))]
    args = [merged, wo_bf, x, gate_g]
    if final_norm_w is not None:
        in_specs.append(pl.BlockSpec((1, d), lambda i: (0, 0)))
        args.append(final_norm_w.reshape(1, d))
    return pl.pallas_call(
        functools.partial(_outproj_kernel, tm=tm, final_norm=final_norm_w is not None),
        grid=(m // tm,),
        in_specs=in_specs,
        out_specs=pl.BlockSpec((tm, d), lambda i: (i, 0)),
        out_shape=jax.ShapeDtypeStruct((m, d), jnp.float32),
        compiler_params=_cparams(("arbitrary",)),
        name="outproj",
    )(*args)


def _relayout_w_in(w):
    offs = [0]
    for s in _IN_SIZES:
        offs.append(offs[-1] + s)
    (qkv, za, ba, aa, qb, kb, vb, zb, qi, ki, wi, gla, glb) = [w[:, offs[n]:offs[n + 1]] for n in range(13)]
    d = w.shape[0]
    pad_sm = jnp.zeros((d, LANES - (IDX_DIM + 3 * GDN_HEADS)), w.dtype)
    cols = [qkv, za, qb, zb, gla, glb, qi, kb, vb, ki, ba, aa, wi, pad_sm]
    out = jnp.concatenate(cols, axis=1)
    pad = jnp.zeros((d, N_PROJ - out.shape[1]), w.dtype)
    return jnp.concatenate([out, pad], axis=1).astype(jnp.bfloat16)


def _rel_bucket(rel):
    nb = REL_BUCKETS // 2
    max_exact = nb // 2
    n = jnp.abs(rel)
    nf = jnp.maximum(n, 1).astype(jnp.float32)
    large = max_exact + (jnp.log(nf / max_exact) / math.log(REL_MAX_DIST / max_exact)
                         * (nb - max_exact)).astype(jnp.int32)
    large = jnp.minimum(large, nb - 1)
    return jnp.where(rel > 0, nb, 0) + jnp.where(n < max_exact, n, large)


def _bias_tables(rel_bias):
    tq = LANES
    trow = jnp.arange(tq)[:, None]
    col = jnp.arange(2 * LANES)[None, :]
    rel = (col - LANES) - trow
    bucket = _rel_bucket(rel)
    tab = sum(jnp.where(bucket == b, rel_bias[b][:, None, None], 0.0) for b in range(REL_BUCKETS))
    far = rel_bias[REL_BUCKETS // 2 - 1]
    tab = tab - far[:, None, None]
    return tab[:, :, :LANES], tab[:, :, LANES:]


def _pad_rows8(a):
    z = jnp.zeros(a.shape[:-2] + (8 - a.shape[-2], a.shape[-1]), a.dtype)
    return jnp.concatenate([z, a], axis=-2)


def _lane_row(vals, off):
    r = jnp.zeros((1, LANES), jnp.float32)
    return r.at[0, off:off + vals.shape[0]].set(vals)


def _layer(x, mod, lw, tables, *, nbatch, t, caches=None, final_norm_w=None):
    (norm_w, w_in_bf, wconv8, lrow, drow, gw, wa_bf, wb_bf, wo_bf) = lw
    near_a, near_b = tables
    d = D_MODEL
    gpb = t // GROUP

    def per_group(a):
        return jnp.broadcast_to(a[:, None, :], (nbatch, gpb, d)).reshape(nbatch * gpb, d)

    shift_g, scale_g, gate_g = [per_group(mod[:, n * d:(n + 1) * d]) for n in range(3)]
    if caches is None:
        state = past_ki = past_kv = None
    else:
        layer, k_all, v_all, ki_all, s_all, conv_prev = caches
        state = (_pad_rows8(conv_prev), s_all, layer)
        past_ki = (ki_all, layer)
        past_kv = (k_all, v_all, layer)

    proj = _inproj_call(x, norm_w, scale_g, shift_g, w_in_bf)
    tq = min(LANES, t)
    madd = _sel_call(proj, nbatch=nbatch, t=t, past_ki=past_ki)
    ob, k_rows, v_rows = _attn_call(proj, madd, near_a[:, :tq], near_b[:, :tq], nbatch=nbatch, t=t,
                                    past_kv=past_kv)
    if state is None and (t // CHUNK) % MERGE_CHUNKS == 0:
        merged, s_new, tails = _gdn_merge_call(proj, ob, wa_bf, wb_bf, wconv8, lrow, drow, gw, nbatch=nbatch, t=t)
    else:
        oa, s_new, tails = _gdn_call(proj, wconv8, lrow, drow, gw, nbatch=nbatch, t=t, state=state)
        merged = _merge_call(oa, ob, wa_bf, wb_bf, proj)
    x_new = _outproj_call(merged, wo_bf, x, gate_g, final_norm_w)

    k_new = k_rows.reshape(nbatch, t, KV_HEADS, HEAD_DIM)
    v_new = v_rows.reshape(nbatch, t, KV_HEADS, HEAD_DIM)
    ki_new = proj[:, OFF_SM + SM_KI:OFF_SM + SM_KI + IDX_DIM].reshape(nbatch, t, IDX_DIM)
    conv_new = tails[:, 8 - (CONV_W - 1):, :]
    return x_new, (k_new, v_new, ki_new, s_new, conv_new)


def kernel(x_prompt, x_sample, c_prompt, c_sample, cache_k, cache_v, cache_idx_k, state_gdn, state_conv,
           norm_w, w_ada, b_ada, w_in, w_conv, a_log, dt_bias, gdn_norm_w, w_branch_a, w_branch_b,
           w_out, rel_bias, final_norm_w):
    depth = w_in.shape[0]
    bp, tp, d = x_prompt.shape
    bs, ts, _ = x_sample.shape
    past = cache_k.shape[2]
    kvw = KV_HEADS * HEAD_DIM

    mod = _ada_call(jnp.concatenate([c_prompt, c_sample], axis=0), w_ada, b_ada)
    tables = _bias_tables(rel_bias)

    xp = x_prompt.reshape(bp * tp, d)
    xs = x_sample.reshape(bs * ts, d)
    new_p, new_s = [], []
    for l in range(depth):
        wconv8 = jnp.concatenate([w_conv[l], jnp.zeros((8 - CONV_W, w_conv.shape[2]), w_conv.dtype)], axis=0)
        lw = (norm_w[l], _relayout_w_in(w_in[l]), wconv8,
              _lane_row(a_log[l], SM_AA), _lane_row(dt_bias[l], SM_AA), gdn_norm_w[l].reshape(1, GDN_DV),
              w_branch_a[l].astype(jnp.bfloat16), w_branch_b[l].astype(jnp.bfloat16),
              w_out[l].astype(jnp.bfloat16))
        fnw = final_norm_w if l == depth - 1 else None
        xp, sp = _layer(xp, mod[l, :bp], lw, tables, nbatch=bp, t=tp, final_norm_w=fnw)
        caches = (l, cache_k.reshape(depth, bs, past * KV_HEADS, HEAD_DIM),
                  cache_v.reshape(depth, bs, past * KV_HEADS, HEAD_DIM), cache_idx_k, state_gdn, state_conv[l])
        xs, ss = _layer(xs, mod[l, bp:], lw, tables, nbatch=bs, t=ts, caches=caches, final_norm_w=fnw)
        new_p.append(sp)
        new_s.append(ss)

    y_prompt = xp.reshape(bp, tp, d)
    y_sample = xs.reshape(bs, ts, d)
    outs_p = [jnp.stack([s[n] for s in new_p]) for n in range(5)]
    outs_s = [jnp.stack([s[n] for s in new_s]) for n in range(5)]
    return (y_prompt, y_sample, *outs_p, *outs_s)
```

```python
import functools
import math

import jax
import jax.numpy as jnp
from jax import lax
from jax.experimental import pallas as pl
from jax.experimental.pallas import tpu as pltpu

D_MODEL = 2048
CHUNK = 64
GDN_HEADS = 16
GDN_DK = 128
GDN_DV = 128
CONV_W = 4
ATT_HEADS = 16
KV_HEADS = 2
HEAD_DIM = 128
IDX_HEADS = 16
IDX_DIM = 64
TOPK_MAX = 256
REL_BUCKETS = 32
REL_MAX_DIST = 128
EPS = 1e-6

LANES = 128
VMEM_LIMIT = 56 * 1024 * 1024
ROW_TILE = 1024
COL_TILE = 1536
MERGE_COL_TILE = 512
OUT_ROW_TILE = 512
SEL_GROUPS = 4
KEY_CHUNK = 512
KEY_CHUNK_ODD = 384

W_QKV = 3 * GDN_HEADS * GDN_DK
OFF_QKV = 0
OFF_ZA = OFF_QKV + W_QKV
OFF_QB = OFF_ZA + D_MODEL
OFF_ZB = OFF_QB + D_MODEL
OFF_GLA = OFF_ZB + D_MODEL
OFF_GLB = OFF_GLA + D_MODEL
OFF_QI = OFF_GLB + D_MODEL
OFF_KB = OFF_QI + IDX_HEADS * IDX_DIM
OFF_VB = OFF_KB + KV_HEADS * HEAD_DIM
OFF_SM = OFF_VB + KV_HEADS * HEAD_DIM
SM_KI, SM_BA, SM_AA, SM_WI = 0, 64, 80, 96
N_PROJ = 18432
GROUP = 64

_IN_SIZES = (W_QKV, D_MODEL, GDN_HEADS, GDN_HEADS, D_MODEL, KV_HEADS * HEAD_DIM, KV_HEADS * HEAD_DIM,
             D_MODEL, IDX_HEADS * IDX_DIM, IDX_DIM, IDX_HEADS, D_MODEL, D_MODEL)


def _cparams(sem):
    return pltpu.CompilerParams(dimension_semantics=sem, vmem_limit_bytes=VMEM_LIMIT)


def _bf(x):
    return x.astype(jnp.bfloat16)


def _dot(a, b):
    return jnp.dot(a, b, preferred_element_type=jnp.float32)


def _dot_nt(a, b):
    return lax.dot_general(a, b, (((1,), (1,)), ((), ())), preferred_element_type=jnp.float32)


def _ada_kernel(c_ref, w_ref, b_ref, o_ref):
    c = c_ref[...]
    a = _bf(c * jax.nn.sigmoid(c))
    o_ref[0] = _dot(a, _bf(w_ref[0])) + b_ref[0]


def _ada_call(c_all, w_ada, b_ada):
    depth, d, n = w_ada.shape
    nb = c_all.shape[0]
    tn = COL_TILE
    return pl.pallas_call(
        _ada_kernel,
        grid=(depth, n // tn),
        in_specs=[pl.BlockSpec((nb, d), lambda l, j: (0, 0)),
                  pl.BlockSpec((1, d, tn), lambda l, j: (l, 0, j)),
                  pl.BlockSpec((1, 1, tn), lambda l, j: (l, 0, j))],
        out_specs=pl.BlockSpec((1, nb, tn), lambda l, j: (l, 0, j)),
        out_shape=jax.ShapeDtypeStruct((depth, nb, n), jnp.float32),
        compiler_params=_cparams(("arbitrary", "arbitrary")),
        name="ada_mod",
    )(c_all, w_ada, b_ada.reshape(depth, 1, n))


def _inproj_kernel(x_ref, nw_ref, sc_ref, sh_ref, w_ref, o_ref, h_ref, *, tm):
    @pl.when(pl.program_id(1) == 0)
    def _():
        nw = nw_ref[...]

        def body(g, carry):
            rows = pl.ds(pl.multiple_of(g * GROUP, GROUP), GROUP)
            x = x_ref[rows, :]
            y = x * lax.rsqrt(jnp.mean(x * x, axis=-1, keepdims=True) + EPS) * nw
            hh = y * (1.0 + sc_ref[pl.ds(g, 1), :]) + sh_ref[pl.ds(g, 1), :]
            h_ref[rows, :] = _bf(hh)
            return carry

        lax.fori_loop(0, tm // GROUP, body, 0)

    o_ref[...] = _dot(h_ref[...], w_ref[...])


def _inproj_call(x, norm_w, scale_g, shift_g, w_bf):
    m, d = x.shape
    n = w_bf.shape[1]
    tm = min(ROW_TILE, m)
    tn = COL_TILE
    gpt = tm // GROUP
    return pl.pallas_call(
        functools.partial(_inproj_kernel, tm=tm),
        grid=(m // tm, n // tn),
        in_specs=[pl.BlockSpec((tm, d), lambda i, j: (i, 0)),
                  pl.BlockSpec((1, d), lambda i, j: (0, 0)),
                  pl.BlockSpec((gpt, d), lambda i, j: (i, 0)),
                  pl.BlockSpec((gpt, d), lambda i, j: (i, 0)),
                  pl.BlockSpec((d, tn), lambda i, j: (0, j))],
        out_specs=pl.BlockSpec((tm, tn), lambda i, j: (i, j)),
        out_shape=jax.ShapeDtypeStruct((m, n), jnp.float32),
        scratch_shapes=[pltpu.VMEM((tm, d), jnp.bfloat16)],
        compiler_params=_cparams(("arbitrary", "arbitrary")),
        name="inproj",
    )(x, norm_w.reshape(1, d), scale_g, shift_g, w_bf)


def _silu(x):
    hx = 0.5 * x
    return hx * jnp.tanh(hx) + hx


def _l2norm(x):
    return x * lax.rsqrt(jnp.sum(x * x, axis=-1, keepdims=True) + EPS)


def _softplus(x):
    return jnp.maximum(x, 0.0) + jnp.log1p(jnp.exp(-jnp.abs(x)))


INV_BASE = 8
HEADS_PER_PASS = GDN_HEADS


def _unit_lower_inverses(As, ii, jj, eye):
    C = As[0].shape[0]
    sh = INV_BASE.bit_length() - 1
    Ns = [jnp.where((ii >> sh) == (jj >> sh), -A, 0.0) for A in As]
    Ps = [eye + N for N in Ns]
    m = 2
    while m < INV_BASE:
        Nbs = [_bf(N) for N in Ns]
        Ns = [_dot(Nb, Nb) for Nb in Nbs]
        Ps = [P + _dot(_bf(P), _bf(N)) for P, N in zip(Ps, Ns)]
        m *= 2
    s = INV_BASE
    while s < C:
        sh = s.bit_length() - 1
        off = ((ii >> (sh + 1)) == (jj >> (sh + 1))) & (((ii >> sh) & 1) == 1) & (((jj >> sh) & 1) == 0)
        Pbs = [_bf(P) for P in Ps]
        Xs = [_dot(Pb, _bf(jnp.where(off, A, 0.0))) for Pb, A in zip(Pbs, As)]
        Ps = [P - _dot(_bf(X), Pb) for P, X, Pb in zip(Ps, Xs, Pbs)]
        s *= 2
    return Ps


MERGE_CHUNKS = 8
MERGE_COLS = 256


def _gdn_kernel(*refs, has_state, merge=None):
    if merge is not None:
        (qkv_ref, z_ref, sm_ref, wc_ref, lrow_ref, drow_ref, gw_ref, ob_ref, wa_ref, wb_ref, ga_ref, gb_ref,
         mg_ref, sout_ref, tout_ref, S_ref, xe_ref, oa_ref) = refs
        nt, nsteps = merge
        step = pl.program_id(0)
        i = step % nt
        last = (i == nt - 1) & (step < nsteps)
    elif has_state:
        (qkv_ref, z_ref, sm_ref, wc_ref, lrow_ref, drow_ref, gw_ref, cp_ref, s0_ref,
         o_ref, sout_ref, tout_ref, S_ref, xe_ref) = refs
    else:
        (qkv_ref, z_ref, sm_ref, wc_ref, lrow_ref, drow_ref, gw_ref,
         o_ref, sout_ref, tout_ref, S_ref, xe_ref) = refs
    if merge is None:
        i = pl.program_id(1)
        last = i == pl.num_programs(1) - 1
    C = CHUNK
    hw = GDN_HEADS * GDN_DK

    if merge is not None:
        @pl.when(step == 0)
        def _():
            oa_ref[...] = jnp.zeros(oa_ref.shape, jnp.bfloat16)

        group = step // MERGE_CHUNKS
        prev = (group + 1) % 2
        ysum = [None, None]

        def merge_piece(kt):
            ks = slice(kt * MERGE_COLS, (kt + 1) * MERGE_COLS)
            for n, (lhs, w_ref) in enumerate(((oa_ref[prev, :, ks], wa_ref), (ob_ref[:, ks], wb_ref))):
                part = _dot(lhs, w_ref[ks, :])
                ysum[n] = part if ysum[n] is None else ysum[n] + part

        pieces = [functools.partial(merge_piece, kt) for kt in range(D_MODEL // MERGE_COLS)]
        o_rows = pl.ds(pl.multiple_of((step % MERGE_CHUNKS) * C, C), C)

        def store_o(cols, val):
            oa_ref[group % 2, o_rows, cols] = val
    else:
        pieces = []

        def store_o(cols, val):
            o_ref[:, cols] = val

    def filler():
        if pieces:
            pieces.pop(0)()

    @pl.when(i == 0)
    def _():
        if has_state:
            S_ref[...] = s0_ref[0, 0]
            xe_ref[0:8, :] = cp_ref[0]
        else:
            S_ref[...] = jnp.zeros(S_ref.shape, jnp.float32)
            xe_ref[0:8, :] = jnp.zeros((8, xe_ref.shape[1]), jnp.float32)

    @pl.when(i > 0)
    def _():
        xe_ref[0:8, :] = xe_ref[C:C + 8, :]

    xe_ref[8:8 + C, :] = qkv_ref[...]

    sm = sm_ref[...]
    beta_all = jax.nn.sigmoid(sm)
    g_all = -jnp.exp(lrow_ref[...]) * _softplus(sm + drow_ref[...])
    rowc = lax.broadcasted_iota(jnp.int32, (C, LANES), 0)
    gc = g_all
    s = 1
    while s < C:
        gc = gc + jnp.where(rowc >= s, pltpu.roll(gc, s, 0), 0.0)
        s *= 2
    glast = gc[C - 1:C, :]
    egc_all = jnp.exp(gc)
    ekd_all = jnp.exp(glast - gc)
    egl_all = jnp.exp(glast)
    gc_t = gc.T

    ii = lax.broadcasted_iota(jnp.int32, (C, C), 0)
    jj = lax.broadcasted_iota(jnp.int32, (C, C), 1)
    eye = jnp.where(ii == jj, 1.0, 0.0)
    gw = gw_ref[...]


    def col(p, h):
        return slice(p * hw + h * GDN_DK, p * hw + (h + 1) * GDN_DK)

    def conv(p, h):
        w = wc_ref[:, col(p, h)]
        y = xe_ref[8:8 + C, col(p, h)] * w[CONV_W - 1:CONV_W]
        for s in range(1, CONV_W):
            y = y + xe_ref[8 - s:8 - s + C, col(p, h)] * w[CONV_W - 1 - s:CONV_W - s]
        return _silu(y)

    def lane(a, l):
        return a[:, l:l + 1]

    def run(heads):
        ks, qs = [], []
        for n, h in enumerate(heads):
            ks.append(_l2norm(conv(1, h)))
            if n % 4 == 3:
                filler()
        for n, h in enumerate(heads):
            qs.append(_l2norm(conv(0, h)) * (GDN_DK ** -0.5))
            if n % 4 == 3:
                filler()
        kbs = [k * lane(beta_all, SM_BA + h) for h, k in zip(heads, ks)]
        kqs = [_dot_nt(_bf(jnp.concatenate([kb, q], axis=0)), _bf(k)) for kb, q, k in zip(kbs, qs, ks)]
        decays = [jnp.where(ii >= jj,
                            jnp.exp(jnp.minimum(lane(gc, SM_AA + h) - gc_t[SM_AA + h:SM_AA + h + 1, :], 0.0)),
                            0.0)
                  for h in heads]
        As = [jnp.where(ii > jj, kq[:C] * d, 0.0) for kq, d in zip(kqs, decays)]
        qks = [_bf(kq[C:] * d) for kq, d in zip(kqs, decays)]
        Ps = _unit_lower_inverses(As, ii, jj, eye)
        vs = [conv(2, h) for h in heads]
        rhs = [_bf(jnp.concatenate([v * lane(beta_all, SM_BA + h), kb * lane(egc_all, SM_AA + h)], axis=1))
               for h, v, kb in zip(heads, vs, kbs)]
        uws = [_dot(_bf(P), r) for P, r in zip(Ps, rhs)]
        Ss = [S_ref[h] for h in heads]
        wqs = [_dot(_bf(jnp.concatenate([uw[:, GDN_DV:], q * lane(egc_all, SM_AA + h)], axis=0)), _bf(S))
               for h, uw, q, S in zip(heads, uws, qs, Ss)]
        vnbs = [_bf(uw[:, :GDN_DV] - wq[:C]) for uw, wq in zip(uws, wqs)]
        kdts = [_bf((k * lane(ekd_all, SM_AA + h)).T) for h, k in zip(heads, ks)]
        for h, S, kdt, vnb in zip(heads, Ss, kdts, vnbs):
            S_ref[h] = S * lane(egl_all, SM_AA + h) + _dot(kdt, vnb)
        os_ = [wq[C:] + _dot(qk, vnb) for wq, qk, vnb in zip(wqs, qks, vnbs)]
        for h, o in zip(heads, os_):
            o = o * lax.rsqrt(jnp.mean(o * o, axis=-1, keepdims=True) + EPS) * gw
            z = z_ref[:, col(0, h)]
            store_o(col(0, h), _bf(o * _silu(z)))

    for h0 in range(0, GDN_HEADS, HEADS_PER_PASS):
        run(range(h0, h0 + HEADS_PER_PASS))

    if merge is not None:
        while pieces:
            filler()
        mg_ref[...] = _bf(jax.nn.sigmoid(ga_ref[...]) * ysum[0] + jax.nn.sigmoid(gb_ref[...]) * ysum[1])

    @pl.when(last)
    def _():
        sout_ref[0] = S_ref[...]
        tout_ref[0] = xe_ref[C:C + 8, :]


def _gdn_call(proj, wconv8, lrow, drow, gw, *, nbatch, t, state=None):
    nt = t // CHUNK
    hh = GDN_HEADS
    const = lambda b, i: (0, 0)
    state_spec = pl.BlockSpec((1, hh, GDN_DK, GDN_DV), lambda b, i: (b, 0, 0, 0))
    tail_spec = pl.BlockSpec((1, 8, W_QKV), lambda b, i: (b, 0, 0))
    in_specs = [pl.BlockSpec((CHUNK, W_QKV), lambda b, i: (b * nt + i, OFF_QKV // W_QKV)),
                pl.BlockSpec((CHUNK, D_MODEL), lambda b, i: (b * nt + i, OFF_ZA // D_MODEL)),
                pl.BlockSpec((CHUNK, LANES), lambda b, i: (b * nt + i, OFF_SM // LANES)),
                pl.BlockSpec((8, W_QKV), const),
                pl.BlockSpec((1, LANES), const),
                pl.BlockSpec((1, LANES), const),
                pl.BlockSpec((1, GDN_DV), const)]
    args = [proj, proj, proj, wconv8, lrow, drow, gw]
    if state is not None:
        conv_prev8, s_all, layer = state
        in_specs += [tail_spec,
                     pl.BlockSpec((1, 1, hh, GDN_DK, GDN_DV), lambda b, i: (layer, b, 0, 0, 0))]
        args += [conv_prev8, s_all]
    return pl.pallas_call(
        functools.partial(_gdn_kernel, has_state=state is not None),
        grid=(nbatch, nt),
        in_specs=in_specs,
        out_specs=[pl.BlockSpec((CHUNK, hh * GDN_DV), lambda b, i: (b * nt + i, 0)),
                   state_spec, tail_spec],
        out_shape=[jax.ShapeDtypeStruct((nbatch * t, hh * GDN_DV), jnp.bfloat16),
                   jax.ShapeDtypeStruct((nbatch, hh, GDN_DK, GDN_DV), jnp.float32),
                   jax.ShapeDtypeStruct((nbatch, 8, W_QKV), jnp.float32)],
        scratch_shapes=[pltpu.VMEM((hh, GDN_DK, GDN_DV), jnp.float32),
                        pltpu.VMEM((8 + CHUNK, W_QKV), jnp.float32)],
        compiler_params=_cparams(("arbitrary", "arbitrary")),
        name="gdn",
    )(*args)


def _gdn_merge_call(proj, ob, wa_bf, wb_bf, wconv8, lrow, drow, gw, *, nbatch, t):
    nt = t // CHUNK
    hh = GDN_HEADS
    d = D_MODEL
    nsteps = nbatch * nt
    ngroups = nsteps // MERGE_CHUNKS

    def out_group(s):
        return jnp.where(s < MERGE_CHUNKS, ngroups, s // MERGE_CHUNKS - 1)
    rows = MERGE_CHUNKS * CHUNK
    assert nt % MERGE_CHUNKS == 0 and d == MERGE_CHUNKS * MERGE_COLS

    def chunk(s):
        return jnp.minimum(s, nsteps - 1)

    def group(s):
        return jnp.maximum(s // MERGE_CHUNKS - 1, 0)

    const = lambda s: (0, 0)
    gate_spec = lambda off: pl.BlockSpec(
        (rows, MERGE_COLS), lambda s: (group(s), off // MERGE_COLS + s % MERGE_CHUNKS))
    w_spec = pl.BlockSpec((d, MERGE_COLS), lambda s: (0, s % MERGE_CHUNKS))
    in_specs = [pl.BlockSpec((CHUNK, W_QKV), lambda s: (chunk(s), OFF_QKV // W_QKV)),
                pl.BlockSpec((CHUNK, D_MODEL), lambda s: (chunk(s), OFF_ZA // D_MODEL)),
                pl.BlockSpec((CHUNK, LANES), lambda s: (chunk(s), OFF_SM // LANES)),
                pl.BlockSpec((8, W_QKV), const),
                pl.BlockSpec((1, LANES), const),
                pl.BlockSpec((1, LANES), const),
                pl.BlockSpec((1, GDN_DV), const),
                pl.BlockSpec((rows, d), lambda s: (group(s), 0)),
                w_spec, w_spec, gate_spec(OFF_GLA), gate_spec(OFF_GLB)]
    return pl.pallas_call(
        functools.partial(_gdn_kernel, has_state=False, merge=(nt, nsteps)),
        grid=(nsteps + MERGE_CHUNKS,),
        in_specs=in_specs,
        out_specs=[pl.BlockSpec((rows, MERGE_COLS), lambda s: (out_group(s), s % MERGE_CHUNKS)),
                   pl.BlockSpec((1, hh, GDN_DK, GDN_DV), lambda s: (chunk(s) // nt, 0, 0, 0)),
                   pl.BlockSpec((1, 8, W_QKV), lambda s: (chunk(s) // nt, 0, 0))],
        out_shape=[jax.ShapeDtypeStruct((nbatch * t + rows, d), jnp.bfloat16),
                   jax.ShapeDtypeStruct((nbatch, hh, GDN_DK, GDN_DV), jnp.float32),
                   jax.ShapeDtypeStruct((nbatch, 8, W_QKV), jnp.float32)],
        scratch_shapes=[pltpu.VMEM((hh, GDN_DK, GDN_DV), jnp.float32),
                        pltpu.VMEM((8 + CHUNK, W_QKV), jnp.float32),
                        pltpu.VMEM((2, rows, d), jnp.bfloat16)],
        compiler_params=_cparams(("arbitrary",)),
        name="gdn_merge",
    )(proj, proj, proj, wconv8, lrow, drow, gw, ob, wa_bf, wb_bf, proj, proj)


_INT_MIN = -2147483648
_KEY_NEG_INF = -2139095041
ATT_GROUP = ATT_HEADS // KV_HEADS
CNT_VREGS = 8


def _visible_chunks(q0, rows, nreal, kc, nkc):
    lim_max = jnp.minimum(((q0 + rows - 1) // CHUNK + 1) * CHUNK, nreal)
    return jnp.minimum((lim_max + kc - 1) // kc, nkc)


def _sel_kernel(*refs, tq, rg, npart, t_cur, past, lp, kc, topk):
    if past:
        qi_ref, smq_ref, smk_ref, kip_ref, m_ref, kibf, key_ref, qis_ref = refs
    else:
        qi_ref, smq_ref, smk_ref, m_ref, kibf, key_ref, qis_ref = refs
    i = pl.program_id(1)
    nreal = past + t_cur
    nkc = lp // kc
    nl = kc // LANES
    groups = range(rg)
    pw = tq // npart

    @pl.when(i == 0)
    def _():
        for p in range(npart):
            if past:
                kibf[p, 0:past, :] = _bf(kip_ref[0, p])
            kibf[p, past:nreal, :] = _bf(smk_ref[p * t_cur:(p + 1) * t_cur, SM_KI:SM_KI + IDX_DIM])
            if lp > nreal:
                kibf[p, nreal:lp, :] = jnp.zeros((lp - nreal, IDX_DIM), jnp.bfloat16)

    q0 = past + i * (rg * pw)
    nvis = _visible_chunks(q0, rg * pw, nreal, kc, nkc)
    tlane = lax.broadcasted_iota(jnp.int32, (1, tq), 1)
    if npart > 1:
        tlane = tlane % pw
    lims = [jnp.minimum(((q0 + g * tq + tlane) // CHUNK + 1) * CHUNK, nreal) for g in groups]

    def rows_of(g):
        return slice(g * tq, (g + 1) * tq)

    def keys_at(start, w):
        return pl.ds(pl.multiple_of(start, LANES), w)

    def pos_at(start, n, width=tq):
        return start + lax.broadcasted_iota(jnp.int32, (n, width), 0)

    trim = npart == 1 and past == 0 and rg * tq == kc
    if trim:
        nfull = i
        diag = [(g + 1) * tq for g in groups]
    else:
        nfull = nvis
        diag = None

    for g in groups:
        for p in range(npart):
            part = slice(g * tq + p * pw, g * tq + (p + 1) * pw)
            w_t = (smq_ref[part, :] * ((IDX_HEADS ** -0.5) * (IDX_DIM ** -0.5))).T
            for hh in range(IDX_HEADS):
                qis_ref[hh * pw:(hh + 1) * pw, :] = _bf(qi_ref[part, hh * IDX_DIM:(hh + 1) * IDX_DIM])
            lim = lims[g][:, p * pw:(p + 1) * pw]

            def score(start, w, masked, p=p, part=part, w_t=w_t, lim=lim):
                d = _dot_nt(kibf[p, keys_at(start, w), :], qis_ref[0:IDX_HEADS * pw, :])
                acc = jnp.zeros((w, pw), jnp.float32)
                for hh in range(IDX_HEADS):
                    acc = acc + (w_t[SM_WI + hh:SM_WI + hh + 1, :]
                                 * jnp.maximum(d[:, hh * pw:(hh + 1) * pw], 0.0))
                if masked:
                    acc = jnp.where(pos_at(start, w, pw) < lim, acc, -jnp.inf)
                bits = pltpu.bitcast(acc, jnp.int32)
                key_ref[keys_at(start, w), part] = jnp.where(bits < 0, bits ^ 0x7FFFFFFF, bits)

            lax.fori_loop(0, nfull, lambda c, carry, score=score: (score(c * kc, kc, not trim), carry)[1], 0)
            if trim:
                score(q0, diag[g], True)

    all_visible = (not trim) and past >= (nkc - 1) * kc

    def count_ge(cands):
        def count(keys, w, accs, which=groups):
            accs = list(accs)
            for g in which:
                hit = jnp.where(key_ref[keys, rows_of(g)] >= cands[g], 1.0, 0.0)
                accs[g] = accs[g] + jnp.sum(hit.reshape(w // cnt_rows, cnt_rows, tq), axis=0)
            return tuple(accs)

        cnt_rows = 8 * max(1, CNT_VREGS // rg)
        accs = tuple(jnp.zeros((cnt_rows, tq), jnp.float32) for _ in groups)
        if all_visible:
            accs = count(slice(0, lp), lp, accs)
        else:
            accs = lax.fori_loop(0, nfull, lambda c, a: count(keys_at(c * kc, kc), kc, a), accs)
            if trim:
                for g in groups:
                    accs = count(keys_at(q0, diag[g]), diag[g], accs, which=[g])
        return [jnp.sum(a, axis=0, keepdims=True) for a in accs]

    def bit_step(it, taus_u):
        cands_u = [t | lax.shift_left(jnp.int32(1), 31 - it) for t in taus_u]
        cnts = count_ge([c ^ _INT_MIN for c in cands_u])
        return tuple(jnp.where(n >= float(topk), c, t) for n, c, t in zip(cnts, cands_u, taus_u))

    taus_u = lax.fori_loop(0, 32, bit_step, tuple(jnp.zeros((1, tq), jnp.int32) for _ in groups))
    taus = [t ^ _INT_MIN for t in taus_u]
    cnts_ge = count_ge(taus)
    cnts_gt = count_ge([t + 1 for t in taus])
    needs = [float(topk) - n for n in cnts_gt]
    any_excess = jnp.int32(0)
    for g in groups:
        excess = (cnts_ge[g] > float(topk)) & (taus[g] > _KEY_NEG_INF)
        any_excess = jnp.maximum(any_excess, jnp.max(jnp.where(excess, 1, 0)))

    ea = lax.broadcasted_iota(jnp.int32, (tq, tq), 0)
    eb = lax.broadcasted_iota(jnp.int32, (tq, tq), 1)
    eye = jnp.where(ea == eb, 1.0, 0.0).astype(jnp.bfloat16)

    def store_mask(g, start, w, sel_t):
        sel = _dot_nt(eye, jnp.where(sel_t, 1.0, 0.0).astype(jnp.bfloat16))
        m_ref[rows_of(g), keys_at(start, w)] = _bf(jnp.where(sel > 0.5, 0.0, -jnp.inf))

    @pl.when(any_excess == 0)
    def _():
        def span(g, start, w, masked):
            sel_t = key_ref[keys_at(start, w), rows_of(g)] >= taus[g]
            if masked:
                sel_t = sel_t & (pos_at(start, w) < lims[g])
            store_mask(g, start, w, sel_t)

        def body(c, carry):
            for g in groups:
                span(g, c * kc, kc, not trim)
            return carry

        lax.fori_loop(0, nfull, body, 0)
        if trim:
            for g in groups:
                span(g, q0, diag[g], True)

    @pl.when(any_excess != 0)
    def _():
        la = lax.broadcasted_iota(jnp.int32, (LANES, LANES), 0)
        lb = lax.broadcasted_iota(jnp.int32, (LANES, LANES), 1)
        lower = jnp.where(la >= lb, 1.0, 0.0).astype(jnp.bfloat16)

        def block(g, start, carry):
            key = key_ref[keys_at(start, LANES), rows_of(g)]
            eq = key == taus[g]
            pref = _dot(lower, jnp.where(eq, 1.0, 0.0).astype(jnp.bfloat16)) + carry
            sel_t = ((key > taus[g]) | (eq & (pref <= needs[g]))) & (pos_at(start, LANES) < lims[g])
            store_mask(g, start, LANES, sel_t)
            return pref[LANES - 1:LANES, :]

        def body(c, carries):
            carries = list(carries)
            for j in range(nl):
                for g in groups:
                    carries[g] = block(g, c * kc + j * LANES, carries[g])
            return tuple(carries)

        carries = lax.fori_loop(0, nfull, body, tuple(jnp.zeros((1, tq), jnp.float32) for _ in groups))
        if trim:
            for g in groups:
                carry = carries[g]
                for j in range(diag[g] // LANES):
                    carry = block(g, q0 + j * LANES, carry)

    if trim:
        for g in groups:
            if diag[g] < kc:
                m_ref[rows_of(g), keys_at(q0 + diag[g], kc - diag[g])] = jnp.full(
                    (tq, kc - diag[g]), -jnp.inf, jnp.bfloat16)

    def fill(c, carry):
        m_ref[:, keys_at(c * kc, kc)] = jnp.full((rg * tq, kc), -jnp.inf, jnp.bfloat16)
        return carry

    lax.fori_loop(nfull + 1 if trim else nvis, nkc, fill, 0)


def _sel_call(proj, *, nbatch, t, past_ki=None):
    past = 0 if past_ki is None else past_ki[0].shape[2]
    _, kc, lp, topk = _attn_geometry(t, past)
    npart = LANES // t if (t < LANES and nbatch % (LANES // t) == 0) else 1
    tq = min(LANES, t * npart)
    rg = min(SEL_GROUPS, (t * npart) // tq)
    rows = rg * tq
    nr = (t * npart) // rows
    qiw = IDX_HEADS * IDX_DIM
    in_specs = [pl.BlockSpec((rows, qiw), lambda b, i: (b * nr + i, OFF_QI // qiw)),
                pl.BlockSpec((rows, LANES), lambda b, i: (b * nr + i, OFF_SM // LANES)),
                pl.BlockSpec((npart * t, LANES), lambda b, i: (b, OFF_SM // LANES))]
    args = [proj, proj, proj]
    if past:
        ki_all, layer = past_ki
        in_specs.append(pl.BlockSpec((1, npart, past, IDX_DIM), lambda b, i: (layer, b, 0, 0)))
        args.append(ki_all)
    return pl.pallas_call(
        functools.partial(_sel_kernel, tq=tq, rg=rg, npart=npart, t_cur=t, past=past, lp=lp, kc=kc, topk=topk),
        grid=(nbatch // npart, nr),
        in_specs=in_specs,
        out_specs=pl.BlockSpec((rows, lp), lambda b, i: (b * nr + i, 0)),
        out_shape=jax.ShapeDtypeStruct((nbatch * t, lp), jnp.bfloat16),
        scratch_shapes=[pltpu.VMEM((npart, lp, IDX_DIM), jnp.bfloat16),
                        pltpu.VMEM((lp, rows), jnp.int32),
                        pltpu.VMEM((IDX_HEADS * tq, IDX_DIM), jnp.bfloat16)],
        compiler_params=_cparams(("arbitrary", "arbitrary")),
        name="sel_past" if past else "sel",
    )(*args)


def _attn_geometry(t, past):
    tq = min(LANES, t)
    nreal = past + t
    kc = KEY_CHUNK if nreal % KEY_CHUNK == 0 else KEY_CHUNK_ODD
    if nreal < kc:
        kc = LANES * (-(-nreal // LANES))
    lp = kc * (-(-nreal // kc))
    assert past % LANES == 0 and past + LANES * (-(-t // LANES)) <= lp
    return tq, kc, lp, min(TOPK_MAX, nreal // 4)


def _attn_kernel(*refs, tq, t_cur, past, lp, kc):
    if past:
        (qb_ref, zb_ref, madd_ref, k_ref, v_ref, kp_ref, vp_ref, na_ref, nb_ref,
         o_ref, ko_ref, vo_ref, kbf, vbf, lg_ref, qs_ref, acc_ref, den_ref, mb_ref) = refs
    else:
        (qb_ref, zb_ref, madd_ref, k_ref, v_ref, na_ref, nb_ref,
         o_ref, ko_ref, vo_ref, kbf, vbf, lg_ref, qs_ref, acc_ref, den_ref, mb_ref) = refs
    i = pl.program_id(1)
    nreal = past + t_cur
    G = ATT_GROUP
    gt = G * tq

    @pl.when(i == 0)
    def _():
        for n in range(KV_HEADS):
            ncol = slice(n * HEAD_DIM, (n + 1) * HEAD_DIM)
            if past:
                kbf[0:past, ncol] = _bf(kp_ref[pl.ds(n, past, stride=KV_HEADS), :])
                vbf[0:past, ncol] = _bf(vp_ref[pl.ds(n, past, stride=KV_HEADS), :])
            ko_ref[pl.ds(n, t_cur, stride=KV_HEADS), :] = k_ref[:, ncol]
            vo_ref[pl.ds(n, t_cur, stride=KV_HEADS), :] = v_ref[:, ncol]
        kbf[past:nreal, :] = _bf(k_ref[...])
        vbf[past:nreal, :] = _bf(v_ref[...])
        if lp > nreal:
            kbf[nreal:lp, :] = jnp.zeros((lp - nreal, KV_HEADS * HEAD_DIM), jnp.bfloat16)
            vbf[nreal:lp, :] = jnp.zeros((lp - nreal, KV_HEADS * HEAD_DIM), jnp.bfloat16)

    q0 = past + i * tq

    far_end = jnp.maximum(q0 - LANES, 0)
    nfull = far_end // kc
    nleft = (far_end - nfull * kc) // LANES
    tw = 2 * LANES
    first = q0 == 0

    def keys_at(off, w):
        return pl.ds(pl.multiple_of(off, LANES), w)

    scale = HEAD_DIM ** -0.5
    for hd in range(ATT_HEADS):
        qs_ref[hd * tq:(hd + 1) * tq, :] = _bf(qb_ref[:, hd * HEAD_DIM:(hd + 1) * HEAD_DIM] * scale)

    for n in range(KV_HEADS):
        ncol = slice(n * HEAD_DIM, (n + 1) * HEAD_DIM)
        grows = slice(n * gt, (n + 1) * gt)

        def logits(off, w, bias=None):
            sc = _dot_nt(qs_ref[grows, :], kbf[keys_at(off, w), ncol])
            ma = madd_ref[:, keys_at(off, w)].astype(jnp.float32)
            for g in range(G):
                r = slice(g * tq, (g + 1) * tq)
                s = sc[r] + ma
                if bias is not None:
                    sb = s[:, w - tw:] + bias(n * G + g)
                    s = sb if w == tw else jnp.concatenate([s[:, :w - tw], sb], axis=1)
                lg_ref[r, keys_at(off, w)] = s
                mt = mb_ref[r, :]
                for j in range(w // LANES):
                    mt = jnp.maximum(mt, s[:, j * LANES:(j + 1) * LANES])
                mb_ref[r, :] = mt

        def tail_bias(hd):
            zero = jnp.zeros((tq, LANES), jnp.float32)
            return jnp.concatenate([jnp.where(first, nb_ref[hd], na_ref[hd]),
                                    jnp.where(first, zero, nb_ref[hd])], axis=1)

        def weighted_values(off, w):
            p = jnp.exp(lg_ref[:, keys_at(off, w)] - jnp.concatenate([mb_ref[...]] * (w // LANES), axis=1))
            den = den_ref[...]
            for j in range(w // LANES):
                den = den + p[:, j * LANES:(j + 1) * LANES]
            den_ref[...] = den
            acc_ref[...] = acc_ref[...] + _dot(_bf(p), vbf[keys_at(off, w), ncol])

        def walk(fn, tail_kwargs):
            lax.fori_loop(0, nfull, lambda c, carry: (fn(c * kc, kc), carry)[1], 0)
            for v in range(min(kc, lp - tw + LANES) // LANES):
                @pl.when(nleft == v)
                def _(v=v):
                    fn(nfull * kc, v * LANES + tw, **tail_kwargs)

        mb_ref[...] = jnp.full((gt, LANES), -jnp.inf, jnp.float32)
        walk(logits, dict(bias=tail_bias))
        for g in range(G):
            r = slice(g * tq, (g + 1) * tq)
            mb_ref[r, :] = jnp.broadcast_to(jnp.max(mb_ref[r, :], axis=1, keepdims=True), (tq, LANES))

        acc_ref[...] = jnp.zeros((gt, HEAD_DIM), jnp.float32)
        den_ref[...] = jnp.zeros((gt, LANES), jnp.float32)
        walk(weighted_values, {})

        for g in range(G):
            r = slice(g * tq, (g + 1) * tq)
            hcol = slice((n * G + g) * HEAD_DIM, (n * G + g + 1) * HEAD_DIM)
            den = jnp.sum(den_ref[r, :], axis=1, keepdims=True)
            z = zb_ref[:, hcol]
            o_ref[:, hcol] = _bf((acc_ref[r, :] / den) * _silu(z))


def _attn_call(proj, madd, near_a, near_b, *, nbatch, t, past_kv=None):
    past = 0 if past_kv is None else past_kv[0].shape[2] // KV_HEADS
    tq, kc, lp, _ = _attn_geometry(t, past)
    nq = t // tq
    kvw = KV_HEADS * HEAD_DIM
    gt = ATT_GROUP * tq

    in_specs = [pl.BlockSpec((tq, D_MODEL), lambda b, i: (b * nq + i, OFF_QB // D_MODEL)),
                pl.BlockSpec((tq, D_MODEL), lambda b, i: (b * nq + i, OFF_ZB // D_MODEL)),
                pl.BlockSpec((tq, lp), lambda b, i: (b * nq + i, 0)),
                pl.BlockSpec((t, kvw), lambda b, i: (b, OFF_KB // kvw)),
                pl.BlockSpec((t, kvw), lambda b, i: (b, OFF_VB // kvw))]
    args = [proj, proj, madd, proj, proj]
    if past:
        k_all, v_all, layer = past_kv
        in_specs += [pl.BlockSpec((None, None, past * KV_HEADS, HEAD_DIM), lambda b, i: (layer, b, 0, 0)),
                     pl.BlockSpec((None, None, past * KV_HEADS, HEAD_DIM), lambda b, i: (layer, b, 0, 0))]
        args += [k_all, v_all]
    in_specs += [pl.BlockSpec((ATT_HEADS, tq, LANES), lambda b, i: (0, 0, 0)),
                 pl.BlockSpec((ATT_HEADS, tq, LANES), lambda b, i: (0, 0, 0))]
    args += [near_a, near_b]

    return pl.pallas_call(
        functools.partial(_attn_kernel, tq=tq, t_cur=t, past=past, lp=lp, kc=kc),
        grid=(nbatch, nq),
        in_specs=in_specs,
        out_specs=[pl.BlockSpec((tq, D_MODEL), lambda b, i: (b * nq + i, 0)),
                   pl.BlockSpec((None, t * KV_HEADS, HEAD_DIM), lambda b, i: (b, 0, 0)),
                   pl.BlockSpec((None, t * KV_HEADS, HEAD_DIM), lambda b, i: (b, 0, 0))],
        out_shape=[jax.ShapeDtypeStruct((nbatch * t, D_MODEL), jnp.bfloat16),
                   jax.ShapeDtypeStruct((nbatch, t * KV_HEADS, HEAD_DIM), jnp.float32),
                   jax.ShapeDtypeStruct((nbatch, t * KV_HEADS, HEAD_DIM), jnp.float32)],
        scratch_shapes=[pltpu.VMEM((lp, kvw), jnp.bfloat16),
                        pltpu.VMEM((lp, kvw), jnp.bfloat16),
                        pltpu.VMEM((gt, lp), jnp.float32),
                        pltpu.VMEM((ATT_HEADS * tq, HEAD_DIM), jnp.bfloat16),
                        pltpu.VMEM((gt, HEAD_DIM), jnp.float32),
                        pltpu.VMEM((gt, LANES), jnp.float32),
                        pltpu.VMEM((gt, LANES), jnp.float32)],
        compiler_params=_cparams(("arbitrary", "arbitrary")),
        name="attn_past" if past else "attn",
    )(*args)


def _merge_kernel(oa_ref, ob_ref, wa_ref, wb_ref, ga_ref, gb_ref, o_ref):
    ya = _dot(oa_ref[...], wa_ref[...])
    yb = _dot(ob_ref[...], wb_ref[...])
    o_ref[...] = _bf(jax.nn.sigmoid(ga_ref[...]) * ya + jax.nn.sigmoid(gb_ref[...]) * yb)


def _merge_call(oa, ob, wa_bf, wb_bf, proj):
    m, d = oa.shape
    tm = min(ROW_TILE, m)
    tn = MERGE_COL_TILE
    return pl.pallas_call(
        _merge_kernel,
        grid=(m // tm, d // tn),
        in_specs=[pl.BlockSpec((tm, d), lambda i, j: (i, 0)),
                  pl.BlockSpec((tm, d), lambda i, j: (i, 0)),
                  pl.BlockSpec((d, tn), lambda i, j: (0, j)),
                  pl.BlockSpec((d, tn), lambda i, j: (0, j)),
                  pl.BlockSpec((tm, tn), lambda i, j: (i, OFF_GLA // tn + j)),
                  pl.BlockSpec((tm, tn), lambda i, j: (i, OFF_GLB // tn + j))],
        out_specs=pl.BlockSpec((tm, tn), lambda i, j: (i, j)),
        out_shape=jax.ShapeDtypeStruct((m, d), jnp.bfloat16),
        compiler_params=_cparams(("arbitrary", "arbitrary")),
        name="merge",
    )(oa, ob, wa_bf, wb_bf, proj, proj)


def _outproj_kernel(*refs, tm, final_norm):
    if final_norm:
        mg_ref, w_ref, x_ref, gate_ref, nw_ref, o_ref = refs
    else:
        mg_ref, w_ref, x_ref, gate_ref, o_ref = refs
    y = _dot(mg_ref[...], w_ref[...])
    for g in range(tm // GROUP):
        r = slice(g * GROUP, (g + 1) * GROUP)
        xn = x_ref[r, :] + gate_ref[g:g + 1, :] * y[r]
        if final_norm:
            xn = xn * lax.rsqrt(jnp.mean(xn * xn, axis=-1, keepdims=True) + EPS) * nw_ref[...]
        o_ref[r, :] = xn


def _outproj_call(merged, wo_bf, x, gate_g, final_norm_w=None):
    m, d = x.shape
    tm = min(OUT_ROW_TILE, m)
    gpt = tm // GROUP
    in_specs = [pl.BlockSpec((tm, d), lambda i: (i, 0)),
                pl.BlockSpec((d, d), lambda i: (0, 0)),
                pl.BlockSpec((tm, d), lambda i: (i, 0)),
                pl.BlockSpec((gpt, d), lambda i: (i, 0))]
    args = [merged, wo_bf, x, gate_g]
    if final_norm_w is not None:
        in_specs.append(pl.BlockSpec((1, d), lambda i: (0, 0)))
        args.append(final_norm_w.reshape(1, d))
    return pl.pallas_call(
        functools.partial(_outproj_kernel, tm=tm, final_norm=final_norm_w is not None),
        grid=(m // tm,),
        in_specs=in_specs,
        out_specs=pl.BlockSpec((tm, d), lambda i: (i, 0)),
        out_shape=jax.ShapeDtypeStruct((m, d), jnp.float32),
        compiler_params=_cparams(("arbitrary",)),
        name="outproj",
    )(*args)


def _relayout_w_in(w):
    offs = [0]
    for s in _IN_SIZES:
        offs.append(offs[-1] + s)
    (qkv, za, ba, aa, qb, kb, vb, zb, qi, ki, wi, gla, glb) = [w[:, offs[n]:offs[n + 1]] for n in range(13)]
    d = w.shape[0]
    pad_sm = jnp.zeros((d, LANES - (IDX_DIM + 3 * GDN_HEADS)), w.dtype)
    cols = [qkv, za, qb, zb, gla, glb, qi, kb, vb, ki, ba, aa, wi, pad_sm]
    out = jnp.concatenate(cols, axis=1)
    pad = jnp.zeros((d, N_PROJ - out.shape[1]), w.dtype)
    return jnp.concatenate([out, pad], axis=1).astype(jnp.bfloat16)


def _rel_bucket(rel):
    nb = REL_BUCKETS // 2
    max_exact = nb // 2
    n = jnp.abs(rel)
    nf = jnp.maximum(n, 1).astype(jnp.float32)
    large = max_exact + (jnp.log(nf / max_exact) / math.log(REL_MAX_DIST / max_exact)
                         * (nb - max_exact)).astype(jnp.int32)
    large = jnp.minimum(large, nb - 1)
    return jnp.where(rel > 0, nb, 0) + jnp.where(n < max_exact, n, large)


def _bias_tables(rel_bias):
    tq = LANES
    trow = jnp.arange(tq)[:, None]
    col = jnp.arange(2 * LANES)[None, :]
    rel = (col - LANES) - trow
    bucket = _rel_bucket(rel)
    tab = sum(jnp.where(bucket == b, rel_bias[b][:, None, None], 0.0) for b in range(REL_BUCKETS))
    far = rel_bias[REL_BUCKETS // 2 - 1]
    tab = tab - far[:, None, None]
    return tab[:, :, :LANES], tab[:, :, LANES:]


def _pad_rows8(a):
    z = jnp.zeros(a.shape[:-2] + (8 - a.shape[-2], a.shape[-1]), a.dtype)
    return jnp.concatenate([z, a], axis=-2)


def _lane_row(vals, off):
    r = jnp.zeros((1, LANES), jnp.float32)
    return r.at[0, off:off + vals.shape[0]].set(vals)


def _layer(x, mod, lw, tables, *, nbatch, t, caches=None, final_norm_w=None):
    (norm_w, w_in_bf, wconv8, lrow, drow, gw, wa_bf, wb_bf, wo_bf) = lw
    near_a, near_b = tables
    d = D_MODEL
    gpb = t // GROUP

    def per_group(a):
        return jnp.broadcast_to(a[:, None, :], (nbatch, gpb, d)).reshape(nbatch * gpb, d)

    shift_g, scale_g, gate_g = [per_group(mod[:, n * d:(n + 1) * d]) for n in range(3)]
    if caches is None:
        state = past_ki = past_kv = None
    else:
        layer, k_all, v_all, ki_all, s_all, conv_prev = caches
        state = (_pad_rows8(conv_prev), s_all, layer)
        past_ki = (ki_all, layer)
        past_kv = (k_all, v_all, layer)

    proj = _inproj_call(x, norm_w, scale_g, shift_g, w_in_bf)
    tq = min(LANES, t)
    madd = _sel_call(proj, nbatch=nbatch, t=t, past_ki=past_ki)
    ob, k_rows, v_rows = _attn_call(proj, madd, near_a[:, :tq], near_b[:, :tq], nbatch=nbatch, t=t,
                                    past_kv=past_kv)
    if state is None and (t // CHUNK) % MERGE_CHUNKS == 0:
        merged, s_new, tails = _gdn_merge_call(proj, ob, wa_bf, wb_bf, wconv8, lrow, drow, gw, nbatch=nbatch, t=t)
    else:
        oa, s_new, tails = _gdn_call(proj, wconv8, lrow, drow, gw, nbatch=nbatch, t=t, state=state)
        merged = _merge_call(oa, ob, wa_bf, wb_bf, proj)
    x_new = _outproj_call(merged, wo_bf, x, gate_g, final_norm_w)

    k_new = k_rows.reshape(nbatch, t, KV_HEADS, HEAD_DIM)
    v_new = v_rows.reshape(nbatch, t, KV_HEADS, HEAD_DIM)
    ki_new = proj[:, OFF_SM + SM_KI:OFF_SM + SM_KI + IDX_DIM].reshape(nbatch, t, IDX_DIM)
    conv_new = tails[:, 8 - (CONV_W - 1):, :]
    return x_new, (k_new, v_new, ki_new, s_new, conv_new)


def kernel(x_prompt, x_sample, c_prompt, c_sample, cache_k, cache_v, cache_idx_k, state_gdn, state_conv,
           norm_w, w_ada, b_ada, w_in, w_conv, a_log, dt_bias, gdn_norm_w, w_branch_a, w_branch_b,
           w_out, rel_bias, final_norm_w):
    depth = w_in.shape[0]
    bp, tp, d = x_prompt.shape
    bs, ts, _ = x_sample.shape
    past = cache_k.shape[2]
    kvw = KV_HEADS * HEAD_DIM

    mod = _ada_call(jnp.concatenate([c_prompt, c_sample], axis=0), w_ada, b_ada)
    tables = _bias_tables(rel_bias)

    xp = x_prompt.reshape(bp * tp, d)
    xs = x_sample.reshape(bs * ts, d)
    new_p, new_s = [], []
    for l in range(depth):
        wconv8 = jnp.concatenate([w_conv[l], jnp.zeros((8 - CONV_W, w_conv.shape[2]), w_conv.dtype)], axis=0)
        lw = (norm_w[l], _relayout_w_in(w_in[l]), wconv8,
              _lane_row(a_log[l], SM_AA), _lane_row(dt_bias[l], SM_AA), gdn_norm_w[l].reshape(1, GDN_DV),
              w_branch_a[l].astype(jnp.bfloat16), w_branch_b[l].astype(jnp.bfloat16),
              w_out[l].astype(jnp.bfloat16))
        fnw = final_norm_w if l == depth - 1 else None
        xp, sp = _layer(xp, mod[l, :bp], lw, tables, nbatch=bp, t=tp, final_norm_w=fnw)
        caches = (l, cache_k.reshape(depth, bs, past * KV_HEADS, HEAD_DIM),
                  cache_v.reshape(depth, bs, past * KV_HEADS, HEAD_DIM), cache_idx_k, state_gdn, state_conv[l])
        xs, ss = _layer(xs, mod[l, bp:], lw, tables, nbatch=bs, t=ts, caches=caches, final_norm_w=fnw)
        new_p.append(sp)
        new_s.append(ss)

    y_prompt = xp.reshape(bp, tp, d)
    y_sample = xs.reshape(bs, ts, d)
    outs_p = [jnp.stack([s[n] for s in new_p]) for n in range(5)]
    outs_s = [jnp.stack([s[n] for s in new_s]) for n in range(5)]
    return (y_prompt, y_sample, *outs_p, *outs_s)
```
